```python
import jax, jax.numpy as jnp
from jax import lax
import numpy as np

D_MODEL = 2048
BATCH = 1
SEQ = 16384
DEPTH = 4

PLE_DIM = 256
RET_WIDTH = D_MODEL // 2
RET_HEADS = 4
RET_HEAD_DIM = RET_WIDTH // RET_HEADS
RET_CHUNK = 128
RET_ROT_BASE = 10000.0
ATT_HEAD_DIM = 64
ATT_WIDTH = D_MODEL - RET_WIDTH
ATT_HEADS = ATT_WIDTH // ATT_HEAD_DIM
ATT_KV_HEADS = ATT_HEADS // 8
KV_WIDTH = ATT_KV_HEADS * ATT_HEAD_DIM
WINDOW = 128
ROPE_THETA = 10000.0
MIX_WIDTH = RET_WIDTH + ATT_WIDTH
SPLIT_SIZES = [RET_WIDTH, RET_WIDTH, RET_WIDTH, RET_WIDTH, ATT_WIDTH, KV_WIDTH, KV_WIDTH]
IN_WIDTH = sum(SPLIT_SIZES)
SPLIT_POINTS = np.cumsum(SPLIT_SIZES)[:-1].tolist()
D_FF = 5632
CONV_WIDTH = 3
LN_EPS = 1e-5
GN_EPS = 1e-6
DEEPNORM_ALPHA = (2 * DEPTH) ** 0.25
DEEPNORM_BETA = (8 * DEPTH) ** -0.25

kernel_name = "hybrid_retention_swa_sink_deepnorm"

F32 = jnp.float32


def layer_norm(x, g, b):
    xf = x.astype(F32)
    mu = xf.mean(-1, keepdims=True)
    var = jnp.square(xf - mu).mean(-1, keepdims=True)
    y = (xf - mu) * lax.rsqrt(var + LN_EPS) * g.astype(F32) + b.astype(F32)
    return y.astype(x.dtype)


def rotary(x, positions, inv_freq):
    ang = positions.astype(F32)[..., None] * inv_freq
    cos = jnp.cos(ang)[:, :, None, :]
    sin = jnp.sin(ang)[:, :, None, :]
    x1, x2 = jnp.split(x.astype(F32), 2, axis=-1)
    return jnp.concatenate([x1 * cos - x2 * sin, x1 * sin + x2 * cos], axis=-1).astype(x.dtype)


def retention_chunkwise(q, k, v):
    B, S, H, dk = q.shape
    dv = v.shape[-1]
    C = RET_CHUNK
    N = S // C
    log_g = jnp.log1p(-jnp.exp2(-5.0 - jnp.arange(H, dtype=F32)))
    q = q.astype(F32).reshape(B, N, C, H, dk)
    k = k.astype(F32).reshape(B, N, C, H, dk) * (dk ** -0.5)
    v = v.astype(F32).reshape(B, N, C, H, dv)
    idx = jnp.arange(C, dtype=F32)
    diff = idx[:, None] - idx[None, :]
    decay = jnp.where(diff[None] >= 0, jnp.exp(log_g[:, None, None] * jnp.maximum(diff, 0.0)[None]), 0.0)
    scores = jnp.einsum('bnihd,bnjhd->bnhij', q, k) * decay
    inner = jnp.einsum('bnhij,bnjhe->bnihe', scores, v)
    q_decay = jnp.exp(log_g[None, :] * (idx[:, None] + 1.0))
    k_decay = jnp.exp(log_g[None, :] * (C - 1.0 - idx[:, None]))
    chunk_decay = jnp.exp(log_g * C)

    def step(state, inp):
        qc, kc, vc = inp
        cross = jnp.einsum('bihd,bhde->bihe', qc, state) * q_decay[None, :, :, None]
        state = state * chunk_decay[None, :, None, None] + jnp.einsum(
            'bjhd,bjhe->bhde', kc * k_decay[None, :, :, None], vc)
        return state, cross

    init = jnp.zeros((B, H, dk, dv), F32)
    _, cross = lax.scan(step, init, (jnp.moveaxis(q, 1, 0), jnp.moveaxis(k, 1, 0), jnp.moveaxis(v, 1, 0)))
    cross = jnp.moveaxis(cross, 0, 1)
    return (inner + cross).reshape(B, S, H, dv)


def head_group_norm(y, g, b):
    mu = y.mean(-1, keepdims=True)
    var = jnp.square(y - mu).mean(-1, keepdims=True)
    y = (y - mu) * lax.rsqrt(var + GN_EPS)
    B, S, H, dv = y.shape
    return y.reshape(B, S, H * dv) * g.astype(F32) + b.astype(F32)


def sliding_window_attention_sinks(q, k, v, sinks):
    B, S, Hq, d = q.shape
    Hkv = k.shape[2]
    G = Hq // Hkv
    W = WINDOW
    N = S // W
    qb = q.astype(F32).reshape(B, N, W, Hkv, G, d)

    def band(t):
        t = t.astype(F32)
        prev = jnp.pad(t, ((0, 0), (W, 0), (0, 0), (0, 0)))[:, :S].reshape(B, N, W, Hkv, d)
        return jnp.concatenate([prev, t.reshape(B, N, W, Hkv, d)], axis=2)

    kb = band(k)
    vb = band(v)
    blk = jnp.arange(N)[:, None, None]
    qi = jnp.arange(W)[None, :, None]
    kj = jnp.arange(2 * W)[None, None, :]
    delta = qi + W - kj
    mask = (delta >= 0) & (delta < W) & (blk * W - W + kj >= 0)
    s = jnp.einsum('bnihgd,bnjhd->bnhgij', qb, kb) * (d ** -0.5)
    s = jnp.where(mask[None, :, None, None], s, -jnp.inf)
    sink = sinks.astype(F32).reshape(Hkv, G)[None, None, :, :, None, None]
    m = jnp.maximum(s.max(-1, keepdims=True), sink)
    pr = jnp.exp(s - m)
    pr = pr / (pr.sum(-1, keepdims=True) + jnp.exp(sink - m))
    o = jnp.einsum('bnhgij,bnjhd->bnihgd', pr, vb)
    return o.reshape(B, S, Hq * d)


def causal_depthwise_conv(h, w, b):
    S = h.shape[1]
    hp = jnp.pad(h, ((0, 0), (CONV_WIDTH - 1, 0), (0, 0)))
    out = b
    for kk in range(CONV_WIDTH):
        out = out + hp[:, kk:kk + S] * w[kk]
    return out


def setup_inputs(seed: int = 0) -> dict:
    key = jax.random.key(seed)
    ks = jax.random.split(key, 20)
    nrm = jax.random.normal
    beta = DEEPNORM_BETA
    x = nrm(ks[0], (BATCH, SEQ, D_MODEL), F32)
    p = nrm(ks[1], (DEPTH, BATCH, SEQ, PLE_DIM), F32)
    positions = jnp.broadcast_to(jnp.arange(SEQ, dtype=jnp.int32)[None, :], (BATCH, SEQ))
    col_scale = jnp.concatenate([
        jnp.ones((2 * RET_WIDTH,), F32), jnp.full((RET_WIDTH,), beta, F32), jnp.ones((RET_WIDTH,), F32),
        jnp.ones((ATT_WIDTH + KV_WIDTH,), F32), jnp.full((KV_WIDTH,), beta, F32)])
    w_in = nrm(ks[2], (DEPTH, D_MODEL, IN_WIDTH), F32) * (D_MODEL ** -0.5) * col_scale
    w_out = nrm(ks[3], (DEPTH, MIX_WIDTH, D_MODEL), F32) * (MIX_WIDTH ** -0.5) * beta
    ret_norm_g = 1.0 + 0.02 * nrm(ks[4], (DEPTH, RET_WIDTH), F32)
    ret_norm_b = 0.02 * nrm(ks[5], (DEPTH, RET_WIDTH), F32)
    attn_sinks = nrm(ks[6], (DEPTH, ATT_HEADS), F32)
    ln1_g = 1.0 + 0.02 * nrm(ks[7], (DEPTH, D_MODEL), F32)
    ln1_b = 0.02 * nrm(ks[8], (DEPTH, D_MODEL), F32)
    w_ffn_up = nrm(ks[9], (DEPTH, D_MODEL, 2 * D_FF), F32) * (D_MODEL ** -0.5) * beta
    ffn_conv_w = nrm(ks[10], (DEPTH, CONV_WIDTH, 2 * D_FF), F32) * (CONV_WIDTH ** -0.5)
    ffn_conv_b = 0.02 * nrm(ks[11], (DEPTH, 2 * D_FF), F32)
    w_ffn_down = nrm(ks[12], (DEPTH, D_FF, D_MODEL), F32) * (D_FF ** -0.5) * beta
    ln2_g = 1.0 + 0.02 * nrm(ks[13], (DEPTH, D_MODEL), F32)
    ln2_b = 0.02 * nrm(ks[14], (DEPTH, D_MODEL), F32)
    w_ple_gate = nrm(ks[15], (DEPTH, D_MODEL, D_MODEL), F32) * (D_MODEL ** -0.5)
    b_ple_gate = 0.01 * nrm(ks[16], (DEPTH, D_MODEL), F32)
    w_ple_proj = nrm(ks[17], (DEPTH, PLE_DIM, D_MODEL), F32) * (PLE_DIM ** -0.5) * beta
    return {"x": x, "p": p, "positions": positions, "w_in": w_in, "w_out": w_out,
            "ret_norm_g": ret_norm_g, "ret_norm_b": ret_norm_b, "attn_sinks": attn_sinks,
            "ln1_g": ln1_g, "ln1_b": ln1_b, "w_ffn_up": w_ffn_up, "ffn_conv_w": ffn_conv_w,
            "ffn_conv_b": ffn_conv_b, "w_ffn_down": w_ffn_down, "ln2_g": ln2_g, "ln2_b": ln2_b,
            "w_ple_gate": w_ple_gate, "b_ple_gate": b_ple_gate, "w_ple_proj": w_ple_proj}


def reference(x, p, positions, w_in, w_out, ret_norm_g, ret_norm_b, attn_sinks, ln1_g, ln1_b,
              w_ffn_up, ffn_conv_w, ffn_conv_b, w_ffn_down, ln2_g, ln2_b, w_ple_gate, b_ple_gate,
              w_ple_proj):
    B, S, _ = x.shape
    ret_inv_freq = 1.0 / (RET_ROT_BASE ** jnp.linspace(0.0, 1.0, RET_HEAD_DIM // 2, dtype=F32))
    att_inv_freq = ROPE_THETA ** (-jnp.arange(0, ATT_HEAD_DIM, 2, dtype=F32) / ATT_HEAD_DIM)
    for i in range(DEPTH):
        proj = x @ w_in[i]
        rq, rk, rv, rg, aq, ak, av = jnp.split(proj, SPLIT_POINTS, axis=-1)
        rq = rotary(rq.reshape(B, S, RET_HEADS, RET_HEAD_DIM), positions, ret_inv_freq)
        rk = rotary(rk.reshape(B, S, RET_HEADS, RET_HEAD_DIM), positions, ret_inv_freq)
        ret = retention_chunkwise(rq, rk, rv.reshape(B, S, RET_HEADS, RET_HEAD_DIM))
        ret = jax.nn.silu(rg.astype(F32)) * head_group_norm(ret, ret_norm_g[i], ret_norm_b[i])
        aq = rotary(aq.reshape(B, S, ATT_HEADS, ATT_HEAD_DIM), positions, att_inv_freq)
        ak = rotary(ak.reshape(B, S, ATT_KV_HEADS, ATT_HEAD_DIM), positions, att_inv_freq)
        att = sliding_window_attention_sinks(aq, ak, av.reshape(B, S, ATT_KV_HEADS, ATT_HEAD_DIM),
                                             attn_sinks[i])
        mix = jnp.concatenate([ret, att], axis=-1).astype(x.dtype) @ w_out[i]
        x = layer_norm(DEEPNORM_ALPHA * x + mix, ln1_g[i], ln1_b[i])
        h = causal_depthwise_conv(x @ w_ffn_up[i], ffn_conv_w[i], ffn_conv_b[i])
        gate, up = jnp.split(h, 2, axis=-1)
        ffn = (jax.nn.silu(gate) * up) @ w_ffn_down[i]
        ple = (p[i] @ w_ple_proj[i]) * jax.nn.sigmoid(x @ w_ple_gate[i] + b_ple_gate[i])
        x = layer_norm(DEEPNORM_ALPHA * x + ffn + ple, ln2_g[i], ln2_b[i])
    return x
```

```python
import functools

import jax
import jax.numpy as jnp
from jax import lax
from jax.experimental import pallas as pl
from jax.experimental.pallas import tpu as pltpu

F32 = jnp.float32
BF16 = jnp.bfloat16

D_MODEL = 2048
DEPTH = 4
PLE_DIM = 256
RET_WIDTH = 1024
RET_HEADS = 4
RET_HEAD_DIM = 256
RET_CHUNK = 128
RET_ROT_BASE = 10000.0
ATT_HEAD_DIM = 64
ATT_WIDTH = 1024
ATT_HEADS = 16
ATT_KV_HEADS = 2
ATT_GROUP = ATT_HEADS // ATT_KV_HEADS
KV_WIDTH = 128
WINDOW = 128
ROPE_THETA = 10000.0
D_FF = 5632
CONV_WIDTH = 3
LN_EPS = 1e-5
GN_EPS = 1e-6
DEEPNORM_ALPHA = (2 * DEPTH) ** 0.25

LANES = 128
SUBLANES = 8
VMEM_LIMIT_CAP = 56 * 1024 * 1024

RET_COLS = 4 * RET_WIDTH
ATT_IN_COLS = ATT_WIDTH + 2 * KV_WIDTH
ATT_OUT_COLS = ATT_WIDTH + 4 * KV_WIDTH
HALF = LANES // 2


def _params(semantics, vmem_bytes):
    return pltpu.CompilerParams(dimension_semantics=semantics,
                                vmem_limit_bytes=int(min(vmem_bytes, VMEM_LIMIT_CAP)))


def _layer_norm(z, g, b):
    mu = jnp.mean(z, axis=-1, keepdims=True)
    d = z - mu
    var = jnp.mean(d * d, axis=-1, keepdims=True)
    return d * lax.rsqrt(var + LN_EPS) * g + b


def _sigmoid(v):
    return 1.0 / (1.0 + jnp.exp(-v))


def _rot_tables_kernel(pos_ref, fr_ref, fa_ref, cr_ref, sr_ref, ca_ref, sa_ref, sb_ref):
    pos = pos_ref[...].astype(F32)
    ang_r = pos * fr_ref[...]
    cr_ref[...] = jnp.cos(ang_r)
    sr_ref[...] = jnp.sin(ang_r)
    ang_a = pos * fa_ref[...]
    c = jnp.cos(ang_a)
    s = jnp.sin(ang_a)
    lane = lax.broadcasted_iota(jnp.int32, c.shape, 1)
    first_half = (lane & (ATT_HEAD_DIM // 2)) == 0
    ca_ref[...] = c
    sa_ref[...] = jnp.where(first_half, -s, 0.0)
    sb_ref[...] = jnp.where(first_half, 0.0, s)


def _rot_tables(positions, tm):
    s = positions.shape[0]
    ret_inv_freq = 1.0 / (RET_ROT_BASE ** jnp.linspace(0.0, 1.0, RET_HEAD_DIM // 2, dtype=F32))
    att_inv_freq = ROPE_THETA ** (-jnp.arange(0, ATT_HEAD_DIM, 2, dtype=F32) / ATT_HEAD_DIM)
    fr = ret_inv_freq.reshape(1, LANES)
    fa = jnp.tile(att_inv_freq, LANES // (ATT_HEAD_DIM // 2)).reshape(1, LANES)
    tab = jax.ShapeDtypeStruct((s, LANES), F32)
    row = pl.BlockSpec((tm, LANES), lambda i: (i, 0))
    const = pl.BlockSpec((1, LANES), lambda i: (0, 0))
    return pl.pallas_call(
        _rot_tables_kernel,
        grid=(s // tm,),
        in_specs=[pl.BlockSpec((tm, 1), lambda i: (i, 0)), const, const],
        out_specs=[row] * 5,
        out_shape=[tab] * 5,
        compiler_params=_params(("arbitrary",), 32 * 1024 * 1024),
        name="rot_tables",
    )(positions.reshape(s, 1), fr, fa)


RET_TN = 512
RET_ROT_BLOCKS = 2 * RET_WIDTH // RET_TN
RET_K_BLOCK0 = RET_WIDTH // RET_TN


def _ret_proj_kernel(x_ref, w_ref, c_ref, s_ref, o_ref):
    j = pl.program_id(1)
    acc = jnp.dot(x_ref[...], w_ref[...], preferred_element_type=F32)

    @pl.when(j < RET_ROT_BLOCKS)
    def _():
        scale = jnp.where(j >= RET_K_BLOCK0, RET_HEAD_DIM ** -0.5, 1.0).astype(F32)
        c = c_ref[...]
        s = s_ref[...]
        for h in range(RET_TN // RET_HEAD_DIM):
            lo = h * RET_HEAD_DIM
            mid = lo + RET_HEAD_DIM // 2
            hi = lo + RET_HEAD_DIM
            a1 = acc[:, lo:mid]
            a2 = acc[:, mid:hi]
            o_ref[:, lo:mid] = ((a1 * c - a2 * s) * scale).astype(BF16)
            o_ref[:, mid:hi] = ((a1 * s + a2 * c) * scale).astype(BF16)

    @pl.when(j >= RET_ROT_BLOCKS)
    def _():
        o_ref[...] = acc.astype(BF16)


def _ret_proj(xb, w, cos_r, sin_r, tm):
    s = xb.shape[0]
    vmem = 2 * (tm * D_MODEL * 2 + D_MODEL * RET_TN * 2 + tm * RET_TN * 2 + 2 * tm * LANES * 4) \
        + 4 * tm * RET_TN * 4
    return pl.pallas_call(
        _ret_proj_kernel,
        grid=(s // tm, RET_COLS // RET_TN),
        in_specs=[pl.BlockSpec((tm, D_MODEL), lambda i, j: (i, 0)),
                  pl.BlockSpec((D_MODEL, RET_TN), lambda i, j: (0, j)),
                  pl.BlockSpec((tm, LANES), lambda i, j: (i, 0)),
                  pl.BlockSpec((tm, LANES), lambda i, j: (i, 0))],
        out_specs=pl.BlockSpec((tm, RET_TN), lambda i, j: (i, j)),
        out_shape=jax.ShapeDtypeStruct((s, RET_COLS), BF16),
        compiler_params=_params(("arbitrary", "arbitrary"), vmem),
        name="ret_proj",
    )(xb, w, cos_r, sin_r)


def _att_proj_kernel(x_ref, w_ref, c_ref, sa_ref, sb_ref, o_ref):
    acc = jnp.dot(x_ref[...], w_ref[...], preferred_element_type=F32)
    c = c_ref[...]
    sa = sa_ref[...]
    sb = sb_ref[...]

    def rot(a):
        return a * c + pltpu.roll(a, LANES - ATT_HEAD_DIM // 2, 1) * sa + pltpu.roll(a, ATT_HEAD_DIM // 2, 1) * sb

    for ch in range(ATT_WIDTH // LANES):
        lo = ch * LANES
        o_ref[:, lo:lo + LANES] = (rot(acc[:, lo:lo + LANES]) * ATT_HEAD_DIM ** -0.5).astype(BF16)
    k = rot(acc[:, ATT_WIDTH:ATT_WIDTH + KV_WIDTH])
    v = acc[:, ATT_WIDTH + KV_WIDTH:ATT_WIDTH + 2 * KV_WIDTH]
    base = ATT_WIDTH
    o_ref[:, base:base + LANES] = k.astype(BF16)
    o_ref[:, base + LANES:base + 2 * LANES] = v.astype(BF16)
    o_ref[:, base + 2 * LANES:base + 3 * LANES] = pltpu.roll(k, HALF, 1).astype(BF16)
    o_ref[:, base + 3 * LANES:base + 4 * LANES] = pltpu.roll(v, HALF, 1).astype(BF16)


def _att_proj(xb, w, cos_a, sin_a, sin_b, tm):
    s = xb.shape[0]
    vmem = 2 * (tm * D_MODEL * 2 + D_MODEL * ATT_IN_COLS * 2 + tm * ATT_OUT_COLS * 2 + 3 * tm * LANES * 4) \
        + 3 * tm * ATT_IN_COLS * 4
    row = pl.BlockSpec((tm, LANES), lambda i: (i, 0))
    return pl.pallas_call(
        _att_proj_kernel,
        grid=(s // tm,),
        in_specs=[pl.BlockSpec((tm, D_MODEL), lambda i: (i, 0)),
                  pl.BlockSpec((D_MODEL, ATT_IN_COLS), lambda i: (0, 0)),
                  row, row, row],
        out_specs=pl.BlockSpec((tm, ATT_OUT_COLS), lambda i: (i, 0)),
        out_shape=jax.ShapeDtypeStruct((s, ATT_OUT_COLS), BF16),
        compiler_params=_params(("arbitrary",), vmem),
        name="att_proj",
    )(xb, w, cos_a, sin_a, sin_b)


def _mixer_kernel(cd_ref, sink_ref, r_ref, a_ref, dec_ref, qd_ref, kd_ref, g_ref, b_ref, o_ref,
                  state_ref, kv_ref, *, tm):
    i = pl.program_id(0)
    n_chunks = tm // RET_CHUNK
    kv_cols = 4 * KV_WIDTH

    @pl.when(i == 0)
    def _():
        state_ref[...] = jnp.zeros_like(state_ref)
        kv_ref[0:WINDOW, :] = jnp.zeros((WINDOW, kv_cols), BF16)

    @pl.when(i > 0)
    def _():
        kv_ref[0:WINDOW, :] = kv_ref[tm:tm + WINDOW, :]

    kv_ref[WINDOW:tm + WINDOW, :] = a_ref[:, ATT_WIDTH:ATT_WIDTH + kv_cols]

    contract_last = (((1,), (1,)), ((), ()))
    contract_first = (((0,), (0,)), ((), ()))
    qi = lax.broadcasted_iota(jnp.int32, (WINDOW, 2 * WINDOW), 0)
    kj = lax.broadcasted_iota(jnp.int32, (WINDOW, 2 * WINDOW), 1)
    band = (kj > qi) & (kj <= qi + WINDOW)
    low_lanes = lax.broadcasted_iota(jnp.int32, (2 * WINDOW, LANES), 1) < HALF

    def chunk(c, carry):
        r0 = pl.multiple_of(c * RET_CHUNK, RET_CHUNK)
        rows = pl.ds(r0, RET_CHUNK)

        for h in range(RET_HEADS):
            lo = h * RET_HEAD_DIM
            hi = lo + RET_HEAD_DIM
            q = r_ref[rows, lo:hi]
            k = r_ref[rows, RET_WIDTH + lo:RET_WIDTH + hi]
            v = r_ref[rows, 2 * RET_WIDTH + lo:2 * RET_WIDTH + hi]
            gate = r_ref[rows, 3 * RET_WIDTH + lo:3 * RET_WIDTH + hi].astype(F32)
            st = state_ref[h]
            sc = lax.dot_general(q, k, contract_last, preferred_element_type=F32) * dec_ref[h]
            inner = jnp.dot(sc.astype(BF16), v, preferred_element_type=F32)
            cross = jnp.dot(q, st.astype(BF16), preferred_element_type=F32) * qd_ref[h]
            k_dec = (k.astype(F32) * kd_ref[h]).astype(BF16)
            state_ref[h] = st * cd_ref[h] + lax.dot_general(k_dec, v, contract_first,
                                                            preferred_element_type=F32)
            y = inner + cross
            mu = jnp.mean(y, axis=-1, keepdims=True)
            d = y - mu
            var = jnp.mean(d * d, axis=-1, keepdims=True)
            yn = d * lax.rsqrt(var + GN_EPS) * g_ref[:, lo:hi] + b_ref[:, lo:hi]
            o_ref[rows, lo:hi] = (gate * _sigmoid(gate) * yn).astype(BF16)

        kk = kv_ref[pl.ds(r0, 2 * WINDOW), :]
        k_nat = kk[:, 0:LANES]
        v_nat = kk[:, LANES:2 * LANES]
        k_swp = kk[:, 2 * LANES:3 * LANES]
        v_swp = kk[:, 3 * LANES:4 * LANES]
        zero = jnp.zeros_like(k_nat)
        k_ext = ((jnp.where(low_lanes, k_nat, zero), jnp.where(low_lanes, zero, k_swp)),
                 (jnp.where(low_lanes, k_swp, zero), jnp.where(low_lanes, zero, k_nat)))
        v_ext = ((jnp.where(low_lanes, v_nat, zero), jnp.where(low_lanes, zero, v_swp)),
                 (jnp.where(low_lanes, v_swp, zero), jnp.where(low_lanes, zero, v_nat)))
        first_block = (i * n_chunks + c) == 0
        mask = band & ((kj >= WINDOW) | jnp.logical_not(first_block))
        pairs_per_kv = ATT_GROUP // 2
        for kvh in range(ATT_KV_HEADS):
            for pr in range(pairs_per_kv):
                pair = kvh * pairs_per_kv + pr
                q_pair = a_ref[rows, pair * LANES:(pair + 1) * LANES]
                o_pair = None
                for par in range(2):
                    sink = sink_ref[2 * pair + par]
                    sc = lax.dot_general(q_pair, k_ext[kvh][par], contract_last, preferred_element_type=F32)
                    sc = jnp.where(mask, sc, -jnp.inf)
                    m = jnp.maximum(jnp.max(sc, axis=-1, keepdims=True), sink)
                    p = jnp.exp(sc - m)
                    den = jnp.sum(p, axis=-1, keepdims=True) + jnp.exp(sink - m)
                    o = jnp.dot(p.astype(BF16), v_ext[kvh][par], preferred_element_type=F32) * (1.0 / den)
                    o_pair = o if o_pair is None else o_pair + o
                o_ref[rows, RET_WIDTH + pair * LANES:RET_WIDTH + (pair + 1) * LANES] = o_pair.astype(BF16)
        return carry

    lax.fori_loop(0, n_chunks, chunk, 0)


def _mixer(rp, ap, ret_consts, gn_g, gn_b, sinks, tm):
    s = rp.shape[0]
    decay, q_decay, k_decay, chunk_decay = ret_consts
    smem = pl.BlockSpec(memory_space=pltpu.SMEM)
    const3 = lambda shape: pl.BlockSpec(shape, lambda i: (0, 0, 0))
    vec = pl.BlockSpec((1, RET_WIDTH), lambda i: (0, 0))
    vmem = 2 * (tm * RET_COLS * 2 + tm * ATT_OUT_COLS * 2 + tm * D_MODEL * 2) \
        + 2 * 4 * (decay.size + q_decay.size + k_decay.size) \
        + RET_HEADS * RET_HEAD_DIM * RET_HEAD_DIM * 4 + (tm + WINDOW) * 4 * KV_WIDTH * 2 + (8 << 20)
    return pl.pallas_call(
        functools.partial(_mixer_kernel, tm=tm),
        grid=(s // tm,),
        in_specs=[smem, smem,
                  pl.BlockSpec((tm, RET_COLS), lambda i: (i, 0)),
                  pl.BlockSpec((tm, ATT_OUT_COLS), lambda i: (i, 0)),
                  const3(decay.shape), const3(q_decay.shape), const3(k_decay.shape), vec, vec],
        out_specs=pl.BlockSpec((tm, D_MODEL), lambda i: (i, 0)),
        out_shape=jax.ShapeDtypeStruct((s, D_MODEL), BF16),
        scratch_shapes=[pltpu.VMEM((RET_HEADS, RET_HEAD_DIM, RET_HEAD_DIM), F32),
                        pltpu.VMEM((tm + WINDOW, 4 * KV_WIDTH), BF16)],
        compiler_params=_params(("arbitrary",), vmem),
        name="mixer",
    )(chunk_decay, sinks, rp, ap, decay, q_decay, k_decay, gn_g, gn_b)


def _retention_constants():
    c = RET_CHUNK
    log_g = jnp.log1p(-jnp.exp2(-5.0 - jnp.arange(RET_HEADS, dtype=F32)))
    idx = jnp.arange(c, dtype=F32)
    diff = idx[:, None] - idx[None, :]
    decay = jnp.where(diff[None] >= 0, jnp.exp(log_g[:, None, None] * jnp.maximum(diff, 0.0)[None]), 0.0)
    q_decay = jnp.exp(log_g[:, None] * (idx[None, :] + 1.0))
    k_decay = jnp.exp(log_g[:, None] * (c - 1.0 - idx[None, :]))
    chunk_decay = jnp.exp(log_g * c)
    wide = lambda t: jnp.broadcast_to(t[:, :, None], (RET_HEADS, c, RET_HEAD_DIM))
    return decay, wide(q_decay), wide(k_decay), chunk_decay


def _mix_out_kernel(cat_ref, x_ref, p_ref, wo_ref, wg_ref, wp_ref, g_ref, b_ref, bg_ref, r_ref, xb_ref):
    z = DEEPNORM_ALPHA * x_ref[...] + jnp.dot(cat_ref[...], wo_ref[...], preferred_element_type=F32)
    x1 = _layer_norm(z, g_ref[...], b_ref[...])
    x1b = x1.astype(BF16)
    gate = jnp.dot(x1b, wg_ref[...], preferred_element_type=F32) + bg_ref[...]
    ple = jnp.dot(p_ref[...].astype(BF16), wp_ref[...], preferred_element_type=F32) * _sigmoid(gate)
    r_ref[...] = DEEPNORM_ALPHA * x1 + ple
    xb_ref[...] = x1b


def _mix_out(cat, x, p, layer, w_out, w_gate, w_proj, ln_g, ln_b, b_gate, tm):
    s = x.shape[0]
    row = lambda cols: pl.BlockSpec((tm, cols), lambda i: (i, 0))
    resident = lambda shape: pl.BlockSpec(shape, lambda i: (0, 0), pipeline_mode=pl.Buffered(1))
    vec = pl.BlockSpec((1, D_MODEL), lambda i: (0, 0))
    vmem = (2 * D_MODEL * D_MODEL + PLE_DIM * D_MODEL) * 2 \
        + 2 * tm * (D_MODEL * 2 + D_MODEL * 4 + PLE_DIM * 4 + D_MODEL * 4 + D_MODEL * 2) \
        + 5 * tm * D_MODEL * 4
    return pl.pallas_call(
        _mix_out_kernel,
        grid=(s // tm,),
        in_specs=[row(D_MODEL), row(D_MODEL),
                  pl.BlockSpec((None, tm, PLE_DIM), lambda i: (layer, i, 0)),
                  resident((D_MODEL, D_MODEL)), resident((D_MODEL, D_MODEL)), resident((PLE_DIM, D_MODEL)),
                  vec, vec, vec],
        out_specs=[row(D_MODEL), row(D_MODEL)],
        out_shape=[jax.ShapeDtypeStruct((s, D_MODEL), F32), jax.ShapeDtypeStruct((s, D_MODEL), BF16)],
        compiler_params=_params(("arbitrary",), vmem),
        name="mix_out",
    )(cat, x, p, w_out, w_gate, w_proj, ln_g, ln_b, b_gate)


FFN_TN = 512
FFN_BLOCKS = D_FF // FFN_TN


def _ffn_up_kernel(x_ref, wg_ref, wu_ref, cwg_ref, cwu_ref, cbg_ref, cbu_ref, o_ref, carry_ref, *, tm):
    i = pl.program_id(0)
    j = pl.program_id(1)

    @pl.when(i == 0)
    def _():
        carry_ref[j] = jnp.zeros((2, SUBLANES, FFN_TN), F32)

    x = x_ref[...]

    def conv(w_ref, cw_ref, cb_ref, slot):
        h = jnp.dot(x, w_ref[...], preferred_element_type=F32)
        ext = jnp.concatenate([carry_ref[j, slot], h], axis=0)
        carry_ref[j, slot] = h[tm - SUBLANES:tm]
        y = (ext * cw_ref[2:3, :] + pltpu.roll(ext, 1, 0) * cw_ref[1:2, :]
             + pltpu.roll(ext, 2, 0) * cw_ref[0:1, :] + cb_ref[...])
        return y[SUBLANES:]

    yg = conv(wg_ref, cwg_ref, cbg_ref, 0)
    yu = conv(wu_ref, cwu_ref, cbu_ref, 1)
    o_ref[...] = (yg * _sigmoid(yg) * yu).astype(BF16)


def _ffn_up(xb, w_up, conv_w, conv_b, tm):
    s = xb.shape[0]
    nb = FFN_BLOCKS
    vmem = 2 * (tm * D_MODEL * 2 + 2 * D_MODEL * FFN_TN * 2 + tm * FFN_TN * 2) + 10 * tm * FFN_TN * 4
    return pl.pallas_call(
        functools.partial(_ffn_up_kernel, tm=tm),
        grid=(s // tm, nb),
        in_specs=[pl.BlockSpec((tm, D_MODEL), lambda i, j: (i, 0)),
                  pl.BlockSpec((D_MODEL, FFN_TN), lambda i, j: (0, j)),
                  pl.BlockSpec((D_MODEL, FFN_TN), lambda i, j: (0, j + nb)),
                  pl.BlockSpec((CONV_WIDTH, FFN_TN), lambda i, j: (0, j)),
                  pl.BlockSpec((CONV_WIDTH, FFN_TN), lambda i, j: (0, j + nb)),
                  pl.BlockSpec((1, FFN_TN), lambda i, j: (0, j)),
                  pl.BlockSpec((1, FFN_TN), lambda i, j: (0, j + nb))],
        out_specs=pl.BlockSpec((tm, FFN_TN), lambda i, j: (i, j)),
        out_shape=jax.ShapeDtypeStruct((s, D_FF), BF16),
        scratch_shapes=[pltpu.VMEM((nb, 2, SUBLANES, FFN_TN), F32)],
        compiler_params=_params(("arbitrary", "arbitrary"), vmem),
        name="ffn_up",
    )(xb, w_up, w_up, conv_w, conv_w, conv_b, conv_b)


def _ffn_down_kernel(a_ref, r_ref, w_ref, g_ref, b_ref, x_ref, xb_ref):
    z = r_ref[...] + jnp.dot(a_ref[...], w_ref[...], preferred_element_type=F32)
    x2 = _layer_norm(z, g_ref[...], b_ref[...])
    x_ref[...] = x2
    xb_ref[...] = x2.astype(BF16)


def _ffn_down(act, r, w_down, ln_g, ln_b, tm):
    s = r.shape[0]
    row = lambda cols: pl.BlockSpec((tm, cols), lambda i: (i, 0))
    vec = pl.BlockSpec((1, D_MODEL), lambda i: (0, 0))
    vmem = D_FF * D_MODEL * 2 + 2 * tm * (D_FF * 2 + D_MODEL * 4 + D_MODEL * 4 + D_MODEL * 2) \
        + 4 * tm * D_MODEL * 4
    return pl.pallas_call(
        _ffn_down_kernel,
        grid=(s // tm,),
        in_specs=[row(D_FF), row(D_MODEL),
                  pl.BlockSpec((D_FF, D_MODEL), lambda i: (0, 0), pipeline_mode=pl.Buffered(1)),
                  vec, vec],
        out_specs=[row(D_MODEL), row(D_MODEL)],
        out_shape=[jax.ShapeDtypeStruct((s, D_MODEL), F32), jax.ShapeDtypeStruct((s, D_MODEL), BF16)],
        compiler_params=_params(("arbitrary",), vmem),
        name="ffn_down",
    )(act, r, w_down, ln_g, ln_b)


def _tiles(s):
    pick = lambda want: min(want, s)
    return dict(tables=pick(1024), ret_proj=pick(1024), att_proj=pick(512), mixer=pick(512),
                mix_out=pick(256), ffn_up=pick(1024), ffn_down=pick(256))


def kernel(x, p, positions, w_in, w_out, ret_norm_g, ret_norm_b, attn_sinks, ln1_g, ln1_b, w_ffn_up,
           ffn_conv_w, ffn_conv_b, w_ffn_down, ln2_g, ln2_b, w_ple_gate, b_ple_gate, w_ple_proj):
    batch, s, d = x.shape
    assert batch == 1 and d == D_MODEL and s % RET_CHUNK == 0
    t = _tiles(s)
    assert all(s % v == 0 for v in t.values())

    xf = x.reshape(s, d)
    xb = xf.astype(BF16)
    pf = p.reshape(DEPTH, s, PLE_DIM)
    cos_r, sin_r, cos_a, sin_a, sin_b = _rot_tables(positions.reshape(s), t["tables"])
    ret_consts = _retention_constants()
    row = lambda v: v.reshape(1, -1)

    for l in range(DEPTH):
        w_in_b = w_in[l].astype(BF16)
        rp = _ret_proj(xb, w_in_b[:, :RET_COLS], cos_r, sin_r, t["ret_proj"])
        ap = _att_proj(xb, w_in_b[:, RET_COLS:], cos_a, sin_a, sin_b, t["att_proj"])
        cat = _mixer(rp, ap, ret_consts, row(ret_norm_g[l]), row(ret_norm_b[l]), attn_sinks[l], t["mixer"])
        r, xb = _mix_out(cat, xf, pf, l, w_out[l].astype(BF16), w_ple_gate[l].astype(BF16),
                         w_ple_proj[l].astype(BF16), row(ln1_g[l]), row(ln1_b[l]), row(b_ple_gate[l]),
                         t["mix_out"])
        act = _ffn_up(xb, w_ffn_up[l].astype(BF16), ffn_conv_w[l], row(ffn_conv_b[l]), t["ffn_up"])
        xf, xb = _ffn_down(act, r, w_ffn_down[l].astype(BF16), row(ln2_g[l]), row(ln2_b[l]), t["ffn_down"])
    return xf.reshape(batch, s, d)
```

```python
import functools

import jax
import jax.numpy as jnp
from jax import lax
from jax.experimental import pallas as pl
from jax.experimental.pallas import tpu as pltpu

F32 = jnp.float32
BF16 = jnp.bfloat16

D_MODEL = 2048
DEPTH = 4
PLE_DIM = 256
RET_WIDTH = 1024
RET_HEADS = 4
RET_HEAD_DIM = 256
RET_CHUNK = 128
RET_ROT_BASE = 10000.0
ATT_HEAD_DIM = 64
ATT_WIDTH = 1024
ATT_HEADS = 16
ATT_KV_HEADS = 2
ATT_GROUP = ATT_HEADS // ATT_KV_HEADS
KV_WIDTH = 128
WINDOW = 128
ROPE_THETA = 10000.0
D_FF = 5632
CONV_WIDTH = 3
LN_EPS = 1e-5
GN_EPS = 1e-6
DEEPNORM_ALPHA = (2 * DEPTH) ** 0.25

LANES = 128
SUBLANES = 8
MXU_DIM = 256
VMEM_LIMIT_CAP = 58 * 1024 * 1024

RET_COLS = 4 * RET_WIDTH
ATT_IN_COLS = ATT_WIDTH + 2 * KV_WIDTH
ATT_OUT_COLS = ATT_WIDTH + 4 * KV_WIDTH
HALF = LANES // 2


def _params(semantics, vmem_bytes):
    return pltpu.CompilerParams(dimension_semantics=semantics,
                                vmem_limit_bytes=int(min(vmem_bytes, VMEM_LIMIT_CAP)))


def _layer_norm(z, g, b):
    mu = jnp.mean(z, axis=-1, keepdims=True)
    d = z - mu
    var = jnp.mean(d * d, axis=-1, keepdims=True)
    return d * lax.rsqrt(var + LN_EPS) * g + b


def _sigmoid(v):
    return 1.0 / (1.0 + jnp.exp(-v))


def _layer_vec(layer, cols):
    return pl.BlockSpec((None, 1, cols), lambda *_: (layer, 0, 0))


def _rot_tables_kernel(pos_ref, fr_ref, fa_ref, cr_ref, sr_ref, ca_ref, sa_ref, sb_ref):
    pos = pos_ref[...].astype(F32)
    ang_r = pos * fr_ref[...]
    cr_ref[...] = jnp.cos(ang_r)
    sr_ref[...] = jnp.sin(ang_r)
    ang_a = pos * fa_ref[...]
    c = jnp.cos(ang_a)
    s = jnp.sin(ang_a)
    lane = lax.broadcasted_iota(jnp.int32, c.shape, 1)
    first_half = (lane & (ATT_HEAD_DIM // 2)) == 0
    ca_ref[...] = c
    sa_ref[...] = jnp.where(first_half, -s, 0.0)
    sb_ref[...] = jnp.where(first_half, 0.0, s)


def _rot_tables(positions, tm):
    s = positions.shape[0]
    ret_inv_freq = 1.0 / (RET_ROT_BASE ** jnp.linspace(0.0, 1.0, RET_HEAD_DIM // 2, dtype=F32))
    att_inv_freq = ROPE_THETA ** (-jnp.arange(0, ATT_HEAD_DIM, 2, dtype=F32) / ATT_HEAD_DIM)
    fr = ret_inv_freq.reshape(1, LANES)
    fa = jnp.tile(att_inv_freq, LANES // (ATT_HEAD_DIM // 2)).reshape(1, LANES)
    tab = jax.ShapeDtypeStruct((s, LANES), F32)
    row = pl.BlockSpec((tm, LANES), lambda i: (i, 0))
    const = pl.BlockSpec((1, LANES), lambda i: (0, 0))
    return pl.pallas_call(
        _rot_tables_kernel,
        grid=(s // tm,),
        in_specs=[pl.BlockSpec((tm, 1), lambda i: (i, 0)), const, const],
        out_specs=[row] * 5,
        out_shape=[tab] * 5,
        compiler_params=_params(("arbitrary",), 32 * 1024 * 1024),
        name="rot_tables",
    )(positions.reshape(s, 1), fr, fa)


RET_TN = 512
RET_ROT_BLOCKS = 2 * RET_WIDTH // RET_TN
RET_K_BLOCK0 = RET_WIDTH // RET_TN


def _ret_proj_kernel(x_ref, w_ref, c_ref, s_ref, o_ref, *, tm, sub):
    j = pl.program_id(1)

    @pl.when(j < RET_ROT_BLOCKS)
    def _():
        scale = jnp.where(j >= RET_K_BLOCK0, RET_HEAD_DIM ** -0.5, 1.0).astype(F32)
        for r in range(0, tm, sub):
            acc = jnp.dot(x_ref[r:r + sub, :], w_ref[...], preferred_element_type=F32)
            c = c_ref[r:r + sub, :] * scale
            s = s_ref[r:r + sub, :] * scale
            for h in range(RET_TN // RET_HEAD_DIM):
                lo = h * RET_HEAD_DIM
                mid = lo + RET_HEAD_DIM // 2
                hi = lo + RET_HEAD_DIM
                a1 = acc[:, lo:mid]
                a2 = acc[:, mid:hi]
                o_ref[r:r + sub, lo:mid] = (a1 * c - a2 * s).astype(BF16)
                o_ref[r:r + sub, mid:hi] = (a1 * s + a2 * c).astype(BF16)

    @pl.when(j >= RET_ROT_BLOCKS)
    def _():
        for r in range(0, tm, sub):
            acc = jnp.dot(x_ref[r:r + sub, :], w_ref[...], preferred_element_type=F32)
            o_ref[r:r + sub, :] = acc.astype(BF16)


def _ret_proj(xb, w_in_b, layer, cos_r, sin_r, tm, sub):
    s = xb.shape[0]
    vmem = 2 * (tm * D_MODEL * 2 + D_MODEL * RET_TN * 2 + tm * RET_TN * 2 + 2 * tm * LANES * 4) + (12 << 20)
    return pl.pallas_call(
        functools.partial(_ret_proj_kernel, tm=tm, sub=sub),
        grid=(s // tm, RET_COLS // RET_TN),
        in_specs=[pl.BlockSpec((tm, D_MODEL), lambda i, j: (i, 0)),
                  pl.BlockSpec((None, D_MODEL, RET_TN), lambda i, j: (layer, 0, j)),
                  pl.BlockSpec((tm, LANES), lambda i, j: (i, 0)),
                  pl.BlockSpec((tm, LANES), lambda i, j: (i, 0))],
        out_specs=pl.BlockSpec((tm, RET_TN), lambda i, j: (i, j)),
        out_shape=jax.ShapeDtypeStruct((s, RET_COLS), BF16),
        compiler_params=_params(("arbitrary", "arbitrary"), vmem),
        name="ret_proj",
    )(xb, w_in_b, cos_r, sin_r)


ATT_W_BLOCKS = ATT_IN_COLS // MXU_DIM
ATT_W_BLOCK0 = RET_COLS // MXU_DIM


def _att_proj_kernel(x_ref, *refs, tm, sub):
    w_refs = refs[:ATT_W_BLOCKS]
    c_ref, sa_ref, sb_ref, o_ref = refs[ATT_W_BLOCKS:]

    for r in range(0, tm, sub):
        rows = slice(r, r + sub)
        c = c_ref[rows, :]
        sa = sa_ref[rows, :]
        sb = sb_ref[rows, :]

        def rot(a):
            return (a * c + pltpu.roll(a, LANES - ATT_HEAD_DIM // 2, 1) * sa
                    + pltpu.roll(a, ATT_HEAD_DIM // 2, 1) * sb)

        x = x_ref[rows, :]
        for wb in range(ATT_W_BLOCKS):
            acc = jnp.dot(x, w_refs[wb][...], preferred_element_type=F32)
            for half in range(MXU_DIM // LANES):
                col = wb * MXU_DIM + half * LANES
                a = acc[:, half * LANES:(half + 1) * LANES]
                if col < ATT_WIDTH:
                    o_ref[rows, col:col + LANES] = (rot(a) * ATT_HEAD_DIM ** -0.5).astype(BF16)
                elif col < ATT_WIDTH + KV_WIDTH:
                    k = rot(a)
                    o_ref[rows, col:col + LANES] = k.astype(BF16)
                    o_ref[rows, col + 2 * LANES:col + 3 * LANES] = pltpu.roll(k, HALF, 1).astype(BF16)
                else:
                    o_ref[rows, col:col + LANES] = a.astype(BF16)
                    o_ref[rows, col + 2 * LANES:col + 3 * LANES] = pltpu.roll(a, HALF, 1).astype(BF16)


def _att_proj(xb, w_in_b, layer, cos_a, sin_a, sin_b, tm, sub):
    s = xb.shape[0]
    vmem = 2 * (tm * D_MODEL * 2 + D_MODEL * ATT_IN_COLS * 2 + tm * ATT_OUT_COLS * 2 + 3 * tm * LANES * 4) + (12 << 20)
    row = pl.BlockSpec((tm, LANES), lambda i: (i, 0))
    w_specs = [pl.BlockSpec((None, D_MODEL, MXU_DIM), functools.partial(lambda i, wb: (layer, 0, ATT_W_BLOCK0 + wb), wb=wb))
               for wb in range(ATT_W_BLOCKS)]
    return pl.pallas_call(
        functools.partial(_att_proj_kernel, tm=tm, sub=sub),
        grid=(s // tm,),
        in_specs=[pl.BlockSpec((tm, D_MODEL), lambda i: (i, 0))] + w_specs + [row, row, row],
        out_specs=pl.BlockSpec((tm, ATT_OUT_COLS), lambda i: (i, 0)),
        out_shape=jax.ShapeDtypeStruct((s, ATT_OUT_COLS), BF16),
        compiler_params=_params(("arbitrary",), vmem),
        name="att_proj",
    )(xb, *([w_in_b] * ATT_W_BLOCKS), cos_a, sin_a, sin_b)


def _mixer_kernel(cd_ref, sink_ref, r_ref, a_ref, dec_ref, qd_ref, kd_ref, g_ref, b_ref, o_ref,
                  state_ref, kv_ref, *, tm, layer):
    i = pl.program_id(0)
    n_chunks = tm // RET_CHUNK
    kv_cols = 4 * KV_WIDTH

    @pl.when(i == 0)
    def _():
        state_ref[...] = jnp.zeros_like(state_ref)
        kv_ref[0:WINDOW, :] = jnp.zeros((WINDOW, kv_cols), BF16)

    @pl.when(i > 0)
    def _():
        kv_ref[0:WINDOW, :] = kv_ref[tm:tm + WINDOW, :]

    kv_ref[WINDOW:tm + WINDOW, :] = a_ref[:, ATT_WIDTH:ATT_WIDTH + kv_cols]

    contract_last = (((1,), (1,)), ((), ()))
    contract_first = (((0,), (0,)), ((), ()))
    qi = lax.broadcasted_iota(jnp.int32, (WINDOW, 2 * WINDOW), 0)
    kj = lax.broadcasted_iota(jnp.int32, (WINDOW, 2 * WINDOW), 1)
    band = (kj > qi) & (kj <= qi + WINDOW)
    low_lanes = lax.broadcasted_iota(jnp.int32, (2 * WINDOW, LANES), 1) < HALF

    def chunk(c, carry):
        r0 = pl.multiple_of(c * RET_CHUNK, RET_CHUNK)
        rows = pl.ds(r0, RET_CHUNK)

        for h in range(RET_HEADS):
            lo = h * RET_HEAD_DIM
            hi = lo + RET_HEAD_DIM
            q = r_ref[rows, lo:hi]
            k = r_ref[rows, RET_WIDTH + lo:RET_WIDTH + hi]
            v = r_ref[rows, 2 * RET_WIDTH + lo:2 * RET_WIDTH + hi]
            gate = r_ref[rows, 3 * RET_WIDTH + lo:3 * RET_WIDTH + hi].astype(F32)
            st = state_ref[h]
            sc = lax.dot_general(q, k, contract_last, preferred_element_type=F32) * dec_ref[h]
            inner = jnp.dot(sc.astype(BF16), v, preferred_element_type=F32)
            cross = jnp.dot(q, st.astype(BF16), preferred_element_type=F32) * qd_ref[h]
            k_dec = (k.astype(F32) * kd_ref[h]).astype(BF16)
            state_ref[h] = st * cd_ref[h] + lax.dot_general(k_dec, v, contract_first,
                                                            preferred_element_type=F32)
            y = inner + cross
            mu = jnp.mean(y, axis=-1, keepdims=True)
            d = y - mu
            var = jnp.mean(d * d, axis=-1, keepdims=True)
            yn = d * lax.rsqrt(var + GN_EPS) * g_ref[:, lo:hi] + b_ref[:, lo:hi]
            o_ref[rows, lo:hi] = (gate * _sigmoid(gate) * yn).astype(BF16)

        kk = kv_ref[pl.ds(r0, 2 * WINDOW), :]
        k_nat = kk[:, 0:LANES]
        v_nat = kk[:, LANES:2 * LANES]
        k_swp = kk[:, 2 * LANES:3 * LANES]
        v_swp = kk[:, 3 * LANES:4 * LANES]
        zero = jnp.zeros_like(k_nat)
        k_ext = ((jnp.where(low_lanes, k_nat, zero), jnp.where(low_lanes, zero, k_swp)),
                 (jnp.where(low_lanes, k_swp, zero), jnp.where(low_lanes, zero, k_nat)))
        v_ext = ((jnp.where(low_lanes, v_nat, zero), jnp.where(low_lanes, zero, v_swp)),
                 (jnp.where(low_lanes, v_swp, zero), jnp.where(low_lanes, zero, v_nat)))
        first_block = (i * n_chunks + c) == 0
        mask = band & ((kj >= WINDOW) | jnp.logical_not(first_block))
        pairs_per_kv = ATT_GROUP // 2
        for kvh in range(ATT_KV_HEADS):
            for pr in range(pairs_per_kv):
                pair = kvh * pairs_per_kv + pr
                q_pair = a_ref[rows, pair * LANES:(pair + 1) * LANES]
                o_pair = None
                for par in range(2):
                    sink = sink_ref[layer, 2 * pair + par]
                    sc = lax.dot_general(q_pair, k_ext[kvh][par], contract_last, preferred_element_type=F32)
                    sc = jnp.where(mask, sc, -jnp.inf)
                    m = jnp.maximum(jnp.max(sc, axis=-1, keepdims=True), sink)
                    p = jnp.exp(sc - m)
                    den = jnp.sum(p, axis=-1, keepdims=True) + jnp.exp(sink - m)
                    o = jnp.dot(p.astype(BF16), v_ext[kvh][par], preferred_element_type=F32) * (1.0 / den)
                    o_pair = o if o_pair is None else o_pair + o
                o_ref[rows, RET_WIDTH + pair * LANES:RET_WIDTH + (pair + 1) * LANES] = o_pair.astype(BF16)
        return carry

    lax.fori_loop(0, n_chunks, chunk, 0)


def _mixer(rp, ap, ret_consts, gn_g, gn_b, sinks, layer, tm):
    s = rp.shape[0]
    decay, q_decay, k_decay, chunk_decay = ret_consts
    smem = pl.BlockSpec(memory_space=pltpu.SMEM)
    const3 = lambda shape: pl.BlockSpec(shape, lambda i: (0, 0, 0))
    vmem = 2 * (tm * RET_COLS * 2 + tm * ATT_OUT_COLS * 2 + tm * D_MODEL * 2) \
        + 2 * 4 * (decay.size + q_decay.size + k_decay.size) \
        + RET_HEADS * RET_HEAD_DIM * RET_HEAD_DIM * 4 + (tm + WINDOW) * 4 * KV_WIDTH * 2 + (8 << 20)
    return pl.pallas_call(
        functools.partial(_mixer_kernel, tm=tm, layer=layer),
        grid=(s // tm,),
        in_specs=[smem, smem,
                  pl.BlockSpec((tm, RET_COLS), lambda i: (i, 0)),
                  pl.BlockSpec((tm, ATT_OUT_COLS), lambda i: (i, 0)),
                  const3(decay.shape), const3(q_decay.shape), const3(k_decay.shape),
                  _layer_vec(layer, RET_WIDTH), _layer_vec(layer, RET_WIDTH)],
        out_specs=pl.BlockSpec((tm, D_MODEL), lambda i: (i, 0)),
        out_shape=jax.ShapeDtypeStruct((s, D_MODEL), BF16),
        scratch_shapes=[pltpu.VMEM((RET_HEADS, RET_HEAD_DIM, RET_HEAD_DIM), F32),
                        pltpu.VMEM((tm + WINDOW, 4 * KV_WIDTH), BF16)],
        compiler_params=_params(("arbitrary",), vmem),
        name="mixer",
    )(chunk_decay, sinks, rp, ap, decay, q_decay, k_decay, gn_g, gn_b)


def _retention_constants():
    c = RET_CHUNK
    log_g = jnp.log1p(-jnp.exp2(-5.0 - jnp.arange(RET_HEADS, dtype=F32)))
    idx = jnp.arange(c, dtype=F32)
    diff = idx[:, None] - idx[None, :]
    decay = jnp.where(diff[None] >= 0, jnp.exp(log_g[:, None, None] * jnp.maximum(diff, 0.0)[None]), 0.0)
    q_decay = jnp.exp(log_g[:, None] * (idx[None, :] + 1.0))
    k_decay = jnp.exp(log_g[:, None] * (c - 1.0 - idx[None, :]))
    chunk_decay = jnp.exp(log_g * c)
    wide = lambda t: jnp.broadcast_to(t[:, :, None], (RET_HEADS, c, RET_HEAD_DIM))
    return decay, wide(q_decay), wide(k_decay), chunk_decay


def _mix_out_kernel(cat_ref, x_ref, p_ref, wo_ref, wg_ref, wp_ref, g_ref, b_ref, bg_ref, r_ref, xb_ref,
                    *, tm, sub):
    for r in range(0, tm, sub):
        rows = slice(r, r + sub)
        z = DEEPNORM_ALPHA * x_ref[rows, :] + jnp.dot(cat_ref[rows, :], wo_ref[...], preferred_element_type=F32)
        x1 = _layer_norm(z, g_ref[...], b_ref[...])
        x1b = x1.astype(BF16)
        gate = jnp.dot(x1b, wg_ref[...], preferred_element_type=F32) + bg_ref[...]
        ple = jnp.dot(p_ref[rows, :].astype(BF16), wp_ref[...], preferred_element_type=F32) * _sigmoid(gate)
        r_ref[rows, :] = DEEPNORM_ALPHA * x1 + ple
        xb_ref[rows, :] = x1b


def _mix_out(cat, x, p, layer, w_out_b, w_gate_b, w_proj_b, ln_g, ln_b, b_gate, tm, sub):
    s = x.shape[0]
    row = lambda cols: pl.BlockSpec((tm, cols), lambda i: (i, 0))
    resident = lambda rows_: pl.BlockSpec((None, rows_, D_MODEL), lambda i: (layer, 0, 0),
                                          pipeline_mode=pl.Buffered(1))
    vmem = (2 * D_MODEL * D_MODEL + PLE_DIM * D_MODEL) * 2 \
        + 2 * tm * (D_MODEL * 2 + D_MODEL * 4 + PLE_DIM * 4 + D_MODEL * 4 + D_MODEL * 2) + (12 << 20)
    return pl.pallas_call(
        functools.partial(_mix_out_kernel, tm=tm, sub=sub),
        grid=(s // tm,),
        in_specs=[row(D_MODEL), row(D_MODEL),
                  pl.BlockSpec((None, tm, PLE_DIM), lambda i: (layer, i, 0)),
                  resident(D_MODEL), resident(D_MODEL), resident(PLE_DIM),
                  _layer_vec(layer, D_MODEL), _layer_vec(layer, D_MODEL), _layer_vec(layer, D_MODEL)],
        out_specs=[row(D_MODEL), row(D_MODEL)],
        out_shape=[jax.ShapeDtypeStruct((s, D_MODEL), F32), jax.ShapeDtypeStruct((s, D_MODEL), BF16)],
        compiler_params=_params(("arbitrary",), vmem),
        name="mix_out",
    )(cat, x, p, w_out_b, w_gate_b, w_proj_b, ln_g, ln_b, b_gate)


FFN_TN = 512
FFN_BLOCKS = D_FF // FFN_TN


def _ffn_up_kernel(x_ref, wg_ref, wu_ref, cwg_ref, cwu_ref, cbg_ref, cbu_ref, o_ref, carry_ref, *, tm, sub):
    i = pl.program_id(0)
    j = pl.program_id(1)

    @pl.when(i == 0)
    def _():
        carry_ref[j] = jnp.zeros((2, SUBLANES, FFN_TN), F32)

    def conv(prev, h, cw_ref, cb_ref):
        ext = jnp.concatenate([prev, h], axis=0)
        y = (ext * cw_ref[2:3, :] + pltpu.roll(ext, 1, 0) * cw_ref[1:2, :]
             + pltpu.roll(ext, 2, 0) * cw_ref[0:1, :] + cb_ref[...])
        return y[SUBLANES:]

    prev_g = carry_ref[j, 0]
    prev_u = carry_ref[j, 1]
    for r in range(0, tm, sub):
        x = x_ref[r:r + sub, :]
        hg = jnp.dot(x, wg_ref[...], preferred_element_type=F32)
        hu = jnp.dot(x, wu_ref[...], preferred_element_type=F32)
        yg = conv(prev_g, hg, cwg_ref, cbg_ref)
        yu = conv(prev_u, hu, cwu_ref, cbu_ref)
        prev_g = hg[sub - SUBLANES:sub]
        prev_u = hu[sub - SUBLANES:sub]
        o_ref[r:r + sub, :] = (yg * _sigmoid(yg) * yu).astype(BF16)
    carry_ref[j, 0] = prev_g
    carry_ref[j, 1] = prev_u


def _ffn_up(xb, w_up_b, conv_w, conv_b, layer, tm, sub):
    s = xb.shape[0]
    nb = FFN_BLOCKS
    vmem = 2 * (tm * D_MODEL * 2 + 2 * D_MODEL * FFN_TN * 2 + tm * FFN_TN * 2) + (12 << 20)
    wspec = lambda off: pl.BlockSpec((None, D_MODEL, FFN_TN), lambda i, j: (layer, 0, j + off))
    cwspec = lambda off: pl.BlockSpec((None, CONV_WIDTH, FFN_TN), lambda i, j: (layer, 0, j + off))
    cbspec = lambda off: pl.BlockSpec((None, 1, FFN_TN), lambda i, j: (layer, 0, j + off))
    return pl.pallas_call(
        functools.partial(_ffn_up_kernel, tm=tm, sub=sub),
        grid=(s // tm, nb),
        in_specs=[pl.BlockSpec((tm, D_MODEL), lambda i, j: (i, 0)),
                  wspec(0), wspec(nb), cwspec(0), cwspec(nb), cbspec(0), cbspec(nb)],
        out_specs=pl.BlockSpec((tm, FFN_TN), lambda i, j: (i, j)),
        out_shape=jax.ShapeDtypeStruct((s, D_FF), BF16),
        scratch_shapes=[pltpu.VMEM((nb, 2, SUBLANES, FFN_TN), F32)],
        compiler_params=_params(("arbitrary", "arbitrary"), vmem),
        name="ffn_up",
    )(xb, w_up_b, w_up_b, conv_w, conv_w, conv_b, conv_b)


def _ffn_down_kernel(a_ref, r_ref, w_ref, g_ref, b_ref, x_ref, xb_ref, *, tm, sub):
    for r in range(0, tm, sub):
        rows = slice(r, r + sub)
        z = r_ref[rows, :] + jnp.dot(a_ref[rows, :], w_ref[...], preferred_element_type=F32)
        x2 = _layer_norm(z, g_ref[...], b_ref[...])
        x_ref[rows, :] = x2
        xb_ref[rows, :] = x2.astype(BF16)


def _ffn_down(act, r, w_down_b, ln_g, ln_b, layer, tm, sub):
    s = r.shape[0]
    row = lambda cols: pl.BlockSpec((tm, cols), lambda i: (i, 0))
    vmem = D_FF * D_MODEL * 2 + 2 * tm * (D_FF * 2 + D_MODEL * 4 + D_MODEL * 4 + D_MODEL * 2) + (8 << 20)
    return pl.pallas_call(
        functools.partial(_ffn_down_kernel, tm=tm, sub=sub),
        grid=(s // tm,),
        in_specs=[row(D_FF), row(D_MODEL),
                  pl.BlockSpec((None, D_FF, D_MODEL), lambda i: (layer, 0, 0), pipeline_mode=pl.Buffered(1)),
                  _layer_vec(layer, D_MODEL), _layer_vec(layer, D_MODEL)],
        out_specs=[row(D_MODEL), row(D_MODEL)],
        out_shape=[jax.ShapeDtypeStruct((s, D_MODEL), F32), jax.ShapeDtypeStruct((s, D_MODEL), BF16)],
        compiler_params=_params(("arbitrary",), vmem),
        name="ffn_down",
    )(act, r, w_down_b, ln_g, ln_b)


def _tiles(s):
    pick = lambda tm, sub: (min(tm, s), min(sub, s))
    return dict(tables=pick(1024, 1024), ret_proj=pick(2048, 512), att_proj=pick(1024, 256),
                mixer=pick(512, 128), mix_out=pick(512, 256), ffn_up=pick(1024, 1024), ffn_down=pick(512, 256))


def kernel(x, p, positions, w_in, w_out, ret_norm_g, ret_norm_b, attn_sinks, ln1_g, ln1_b, w_ffn_up,
           ffn_conv_w, ffn_conv_b, w_ffn_down, ln2_g, ln2_b, w_ple_gate, b_ple_gate, w_ple_proj):
    batch, s, d = x.shape
    assert batch == 1 and d == D_MODEL and s % RET_CHUNK == 0
    t = _tiles(s)
    assert all(s % tm == 0 and tm % sub == 0 for tm, sub in t.values())

    xf = x.reshape(s, d)
    xb = xf.astype(BF16)
    pf = p.reshape(DEPTH, s, PLE_DIM)
    cos_r, sin_r, cos_a, sin_a, sin_b = _rot_tables(positions.reshape(s), t["tables"][0])
    ret_consts = _retention_constants()
    vecs = lambda v: v.reshape(DEPTH, 1, -1)
    w_in_b, w_out_b, w_up_b, w_down_b = (w.astype(BF16) for w in (w_in, w_out, w_ffn_up, w_ffn_down))
    w_gate_b, w_proj_b = w_ple_gate.astype(BF16), w_ple_proj.astype(BF16)
    gn_g, gn_b, g1, b1, g2, b2 = (vecs(v) for v in (ret_norm_g, ret_norm_b, ln1_g, ln1_b, ln2_g, ln2_b))
    bg, conv_b = vecs(b_ple_gate), vecs(ffn_conv_b)

    for l in range(DEPTH):
        rp = _ret_proj(xb, w_in_b, l, cos_r, sin_r, *t["ret_proj"])
        ap = _att_proj(xb, w_in_b, l, cos_a, sin_a, sin_b, *t["att_proj"])
        cat = _mixer(rp, ap, ret_consts, gn_g, gn_b, attn_sinks, l, t["mixer"][0])
        r, xb = _mix_out(cat, xf, pf, l, w_out_b, w_gate_b, w_proj_b, g1, b1, bg, *t["mix_out"])
        act = _ffn_up(xb, w_up_b, ffn_conv_w, conv_b, l, *t["ffn_up"])
        xf, xb = _ffn_down(act, r, w_down_b, g2, b2, l, *t["ffn_down"])
    return xf.reshape(batch, s, d)
```

```python
import functools

import jax
import jax.numpy as jnp
from jax import lax
from jax.experimental import pallas as pl
from jax.experimental.pallas import tpu as pltpu

F32 = jnp.float32
BF16 = jnp.bfloat16

D_MODEL = 2048
DEPTH = 4
PLE_DIM = 256
RET_WIDTH = 1024
RET_HEADS = 4
RET_HEAD_DIM = 256
RET_CHUNK = 128
RET_ROT_BASE = 10000.0
ATT_HEAD_DIM = 64
ATT_WIDTH = 1024
ATT_HEADS = 16
ATT_KV_HEADS = 2
ATT_GROUP = ATT_HEADS // ATT_KV_HEADS
KV_WIDTH = 128
WINDOW = 128
ROPE_THETA = 10000.0
D_FF = 5632
CONV_WIDTH = 3
LN_EPS = 1e-5
GN_EPS = 1e-6
DEEPNORM_ALPHA = (2 * DEPTH) ** 0.25

LANES = 128
SUBLANES = 8
MXU_DIM = 256
VMEM_LIMIT_CAP = 58 * 1024 * 1024

RET_COLS = 4 * RET_WIDTH
ATT_IN_COLS = ATT_WIDTH + 2 * KV_WIDTH
ATT_OUT_COLS = ATT_WIDTH + 4 * KV_WIDTH
HALF = LANES // 2


def _params(semantics, vmem_bytes):
    return pltpu.CompilerParams(dimension_semantics=semantics,
                                vmem_limit_bytes=int(min(vmem_bytes, VMEM_LIMIT_CAP)))


def _layer_norm(z, g, b):
    mu = jnp.mean(z, axis=-1, keepdims=True)
    d = z - mu
    var = jnp.mean(d * d, axis=-1, keepdims=True)
    return d * lax.rsqrt(var + LN_EPS) * g + b


def _sigmoid(v):
    return 1.0 / (1.0 + jnp.exp(-v))


def _layer_vec(layer, cols):
    return pl.BlockSpec((None, 1, cols), lambda *_: (layer, 0, 0))


def _rot_tables_kernel(pos_ref, fr_ref, fa_ref, cr_ref, sr_ref, ca_ref, sa_ref, sb_ref):
    pos = pos_ref[...].astype(F32)
    ang_r = pos * fr_ref[...]
    cr_ref[...] = jnp.cos(ang_r)
    sr_ref[...] = jnp.sin(ang_r)
    ang_a = pos * fa_ref[...]
    c = jnp.cos(ang_a)
    s = jnp.sin(ang_a)
    lane = lax.broadcasted_iota(jnp.int32, c.shape, 1)
    first_half = (lane & (ATT_HEAD_DIM // 2)) == 0
    ca_ref[...] = c
    sa_ref[...] = jnp.where(first_half, -s, 0.0)
    sb_ref[...] = jnp.where(first_half, 0.0, s)


def _rot_tables(positions, tm):
    s = positions.shape[0]
    ret_inv_freq = 1.0 / (RET_ROT_BASE ** jnp.linspace(0.0, 1.0, RET_HEAD_DIM // 2, dtype=F32))
    att_inv_freq = ROPE_THETA ** (-jnp.arange(0, ATT_HEAD_DIM, 2, dtype=F32) / ATT_HEAD_DIM)
    fr = ret_inv_freq.reshape(1, LANES)
    fa = jnp.tile(att_inv_freq, LANES // (ATT_HEAD_DIM // 2)).reshape(1, LANES)
    tab = jax.ShapeDtypeStruct((s, LANES), F32)
    row = pl.BlockSpec((tm, LANES), lambda i: (i, 0))
    const = pl.BlockSpec((1, LANES), lambda i: (0, 0))
    return pl.pallas_call(
        _rot_tables_kernel,
        grid=(s // tm,),
        in_specs=[pl.BlockSpec((tm, 1), lambda i: (i, 0)), const, const],
        out_specs=[row] * 5,
        out_shape=[tab] * 5,
        compiler_params=_params(("arbitrary",), 32 * 1024 * 1024),
        name="rot_tables",
    )(positions.reshape(s, 1), fr, fa)


RET_TN = 512
RET_BLOCKS_PER_PART = RET_WIDTH // RET_TN
RET_HEADS_PER_BLOCK = RET_TN // RET_HEAD_DIM


def _ret_proj_kernel(x_ref, w_ref, c_ref, s_ref, kd_ref, o_ref, kdec_ref, wb_ref, *, tm, sub):
    j = pl.program_id(1)
    part = j // RET_BLOCKS_PER_PART

    def sub_blocks():
        wb_ref[...] = w_ref[...].astype(BF16)
        for r in range(0, tm, sub):
            rows = slice(r, r + sub)
            yield rows, jnp.dot(x_ref[rows, :], wb_ref[...], preferred_element_type=F32)

    def rotary(rows, acc, scale):
        c = c_ref[rows, :] * scale
        s = s_ref[rows, :] * scale
        for h in range(RET_HEADS_PER_BLOCK):
            lo = h * RET_HEAD_DIM
            mid = lo + RET_HEAD_DIM // 2
            a1 = acc[:, lo:mid]
            a2 = acc[:, mid:lo + RET_HEAD_DIM]
            yield h, lo, a1 * c - a2 * s
            yield h, mid, a1 * s + a2 * c

    @pl.when(part == 0)
    def _():
        for rows, acc in sub_blocks():
            for _, col, val in rotary(rows, acc, 1.0):
                o_ref[rows, col:col + LANES] = val.astype(BF16)

    @pl.when(part == 1)
    def _():
        for rows, acc in sub_blocks():
            for h, col, val in rotary(rows, acc, RET_HEAD_DIM ** -0.5):
                o_ref[rows, col:col + LANES] = val.astype(BF16)
                chunks = val.reshape(sub // RET_CHUNK, RET_CHUNK, LANES) * kd_ref[h][None]
                kdec_ref[rows, col:col + LANES] = chunks.reshape(sub, LANES).astype(BF16)

    @pl.when(part == 2)
    def _():
        for rows, acc in sub_blocks():
            o_ref[rows, :] = acc.astype(BF16)

    @pl.when(part == 3)
    def _():
        for rows, acc in sub_blocks():
            o_ref[rows, :] = (acc * _sigmoid(acc)).astype(BF16)


def _ret_proj(xb, w_in, layer, cos_r, sin_r, k_decay, tm, sub):
    s = xb.shape[0]
    vmem = 2 * (tm * D_MODEL * 2 + D_MODEL * RET_TN * 4 + 2 * tm * RET_TN * 2 + 2 * tm * LANES * 4) \
        + D_MODEL * RET_TN * 2 + (12 << 20)
    k_block = lambda j: jnp.clip(j - RET_BLOCKS_PER_PART, 0, RET_BLOCKS_PER_PART - 1)
    return pl.pallas_call(
        functools.partial(_ret_proj_kernel, tm=tm, sub=sub),
        grid=(s // tm, RET_COLS // RET_TN),
        in_specs=[pl.BlockSpec((tm, D_MODEL), lambda i, j: (i, 0)),
                  pl.BlockSpec((None, D_MODEL, RET_TN), lambda i, j: (layer, 0, j)),
                  pl.BlockSpec((tm, LANES), lambda i, j: (i, 0)),
                  pl.BlockSpec((tm, LANES), lambda i, j: (i, 0)),
                  pl.BlockSpec((RET_HEADS_PER_BLOCK, RET_CHUNK, LANES), lambda i, j: (k_block(j), 0, 0))],
        out_specs=[pl.BlockSpec((tm, RET_TN), lambda i, j: (i, j)),
                   pl.BlockSpec((tm, RET_TN), lambda i, j: (i, k_block(j)))],
        out_shape=[jax.ShapeDtypeStruct((s, RET_COLS), BF16), jax.ShapeDtypeStruct((s, RET_WIDTH), BF16)],
        scratch_shapes=[pltpu.VMEM((D_MODEL, RET_TN), BF16)],
        compiler_params=_params(("arbitrary", "arbitrary"), vmem),
        name="ret_proj",
    )(xb, w_in, cos_r, sin_r, k_decay)


ATT_W_BLOCKS = ATT_IN_COLS // MXU_DIM


def _att_proj_kernel(x_ref, *refs, tm, sub):
    w_refs = refs[:ATT_W_BLOCKS]
    c_ref, sa_ref, sb_ref, o_ref = refs[ATT_W_BLOCKS:]

    for r in range(0, tm, sub):
        rows = slice(r, r + sub)
        c = c_ref[rows, :]
        sa = sa_ref[rows, :]
        sb = sb_ref[rows, :]

        def rot(a):
            return (a * c + pltpu.roll(a, LANES - ATT_HEAD_DIM // 2, 1) * sa
                    + pltpu.roll(a, ATT_HEAD_DIM // 2, 1) * sb)

        x = x_ref[rows, :]
        for wb in range(ATT_W_BLOCKS):
            acc = jnp.dot(x, w_refs[wb][...], preferred_element_type=F32)
            for half in range(MXU_DIM // LANES):
                col = wb * MXU_DIM + half * LANES
                a = acc[:, half * LANES:(half + 1) * LANES]
                if col < ATT_WIDTH:
                    o_ref[rows, col:col + LANES] = (rot(a) * ATT_HEAD_DIM ** -0.5).astype(BF16)
                elif col < ATT_WIDTH + KV_WIDTH:
                    k = rot(a)
                    o_ref[rows, col:col + LANES] = k.astype(BF16)
                    o_ref[rows, col + 2 * LANES:col + 3 * LANES] = pltpu.roll(k, HALF, 1).astype(BF16)
                else:
                    o_ref[rows, col:col + LANES] = a.astype(BF16)
                    o_ref[rows, col + 2 * LANES:col + 3 * LANES] = pltpu.roll(a, HALF, 1).astype(BF16)


def _att_proj(xb, w_att_b, layer, cos_a, sin_a, sin_b, tm, sub):
    s = xb.shape[0]
    vmem = 2 * (tm * D_MODEL * 2 + D_MODEL * ATT_IN_COLS * 2 + tm * ATT_OUT_COLS * 2 + 3 * tm * LANES * 4) + (12 << 20)
    row = pl.BlockSpec((tm, LANES), lambda i: (i, 0))
    w_specs = [pl.BlockSpec((None, D_MODEL, MXU_DIM), functools.partial(lambda i, wb: (layer, 0, wb), wb=wb))
               for wb in range(ATT_W_BLOCKS)]
    return pl.pallas_call(
        functools.partial(_att_proj_kernel, tm=tm, sub=sub),
        grid=(s // tm,),
        in_specs=[pl.BlockSpec((tm, D_MODEL), lambda i: (i, 0))] + w_specs + [row, row, row],
        out_specs=pl.BlockSpec((tm, ATT_OUT_COLS), lambda i: (i, 0)),
        out_shape=jax.ShapeDtypeStruct((s, ATT_OUT_COLS), BF16),
        compiler_params=_params(("arbitrary",), vmem),
        name="att_proj",
    )(xb, *([w_att_b] * ATT_W_BLOCKS), cos_a, sin_a, sin_b)


def _mixer_kernel(cd_ref, sink_ref, r_ref, kdec_ref, a_ref, dec_ref, qd_ref, g_ref, b_ref, o_ref,
                  state_ref, kv_ref, *, tm, layer):
    i = pl.program_id(0)
    n_chunks = tm // RET_CHUNK
    kv_cols = 4 * KV_WIDTH

    @pl.when(i == 0)
    def _():
        state_ref[...] = jnp.zeros_like(state_ref)
        kv_ref[0:WINDOW, :] = jnp.zeros((WINDOW, kv_cols), BF16)

    @pl.when(i > 0)
    def _():
        kv_ref[0:WINDOW, :] = kv_ref[tm:tm + WINDOW, :]

    kv_ref[WINDOW:tm + WINDOW, :] = a_ref[:, ATT_WIDTH:ATT_WIDTH + kv_cols]

    contract_last = (((1,), (1,)), ((), ()))
    contract_first = (((0,), (0,)), ((), ()))
    qi = lax.broadcasted_iota(jnp.int32, (WINDOW, 2 * WINDOW), 0)
    kj = lax.broadcasted_iota(jnp.int32, (WINDOW, 2 * WINDOW), 1)
    band = (kj > qi) & (kj <= qi + WINDOW)
    low_lanes = lax.broadcasted_iota(jnp.int32, (2 * WINDOW, LANES), 1) < HALF

    def chunk(c, carry):
        r0 = pl.multiple_of(c * RET_CHUNK, RET_CHUNK)
        rows = pl.ds(r0, RET_CHUNK)

        for h in range(RET_HEADS):
            lo = h * RET_HEAD_DIM
            hi = lo + RET_HEAD_DIM
            q = r_ref[rows, lo:hi]
            k = r_ref[rows, RET_WIDTH + lo:RET_WIDTH + hi]
            v = r_ref[rows, 2 * RET_WIDTH + lo:2 * RET_WIDTH + hi]
            silu_gate = r_ref[rows, 3 * RET_WIDTH + lo:3 * RET_WIDTH + hi].astype(F32)
            st = state_ref[h]
            sc = lax.dot_general(q, k, contract_last, preferred_element_type=F32) * dec_ref[h]
            inner = jnp.dot(sc.astype(BF16), v, preferred_element_type=F32)
            cross = jnp.dot(q, st.astype(BF16), preferred_element_type=F32) * qd_ref[h]
            state_ref[h] = st * cd_ref[h] + lax.dot_general(kdec_ref[rows, lo:hi], v, contract_first,
                                                            preferred_element_type=F32)
            y = inner + cross
            mu = jnp.mean(y, axis=-1, keepdims=True)
            d = y - mu
            var = jnp.mean(d * d, axis=-1, keepdims=True)
            yn = d * lax.rsqrt(var + GN_EPS) * g_ref[:, lo:hi] + b_ref[:, lo:hi]
            o_ref[rows, lo:hi] = (silu_gate * yn).astype(BF16)

        kk = kv_ref[pl.ds(r0, 2 * WINDOW), :]
        k_nat = kk[:, 0:LANES]
        v_nat = kk[:, LANES:2 * LANES]
        k_swp = kk[:, 2 * LANES:3 * LANES]
        v_swp = kk[:, 3 * LANES:4 * LANES]
        zero = jnp.zeros_like(k_nat)
        k_ext = ((jnp.where(low_lanes, k_nat, zero), jnp.where(low_lanes, zero, k_swp)),
                 (jnp.where(low_lanes, k_swp, zero), jnp.where(low_lanes, zero, k_nat)))
        v_ext = ((jnp.where(low_lanes, v_nat, zero), jnp.where(low_lanes, zero, v_swp)),
                 (jnp.where(low_lanes, v_swp, zero), jnp.where(low_lanes, zero, v_nat)))
        first_block = (i * n_chunks + c) == 0
        mask = band & ((kj >= WINDOW) | jnp.logical_not(first_block))
        pairs_per_kv = ATT_GROUP // 2
        for kvh in range(ATT_KV_HEADS):
            for pr in range(pairs_per_kv):
                pair = kvh * pairs_per_kv + pr
                q_pair = a_ref[rows, pair * LANES:(pair + 1) * LANES]
                o_pair = None
                for par in range(2):
                    sink = sink_ref[layer, 2 * pair + par]
                    sc = lax.dot_general(q_pair, k_ext[kvh][par], contract_last, preferred_element_type=F32)
                    sc = jnp.where(mask, sc, -jnp.inf)
                    m = jnp.maximum(jnp.max(sc, axis=-1, keepdims=True), sink)
                    p = jnp.exp(sc - m)
                    den = jnp.sum(p, axis=-1, keepdims=True) + jnp.exp(sink - m)
                    o = jnp.dot(p.astype(BF16), v_ext[kvh][par], preferred_element_type=F32) * (1.0 / den)
                    o_pair = o if o_pair is None else o_pair + o
                o_ref[rows, RET_WIDTH + pair * LANES:RET_WIDTH + (pair + 1) * LANES] = o_pair.astype(BF16)
        return carry

    lax.fori_loop(0, n_chunks, chunk, 0)


def _mixer(rp, kdec, ap, ret_consts, gn_g, gn_b, sinks, layer, tm):
    s = rp.shape[0]
    decay, q_decay, chunk_decay = ret_consts
    smem = pl.BlockSpec(memory_space=pltpu.SMEM)
    const3 = lambda shape: pl.BlockSpec(shape, lambda i: (0, 0, 0))
    vmem = 2 * (tm * (RET_COLS + RET_WIDTH) * 2 + tm * ATT_OUT_COLS * 2 + tm * D_MODEL * 2) \
        + 2 * 4 * (decay.size + q_decay.size) \
        + RET_HEADS * RET_HEAD_DIM * RET_HEAD_DIM * 4 + (tm + WINDOW) * 4 * KV_WIDTH * 2 + (8 << 20)
    return pl.pallas_call(
        functools.partial(_mixer_kernel, tm=tm, layer=layer),
        grid=(s // tm,),
        in_specs=[smem, smem,
                  pl.BlockSpec((tm, RET_COLS), lambda i: (i, 0)),
                  pl.BlockSpec((tm, RET_WIDTH), lambda i: (i, 0)),
                  pl.BlockSpec((tm, ATT_OUT_COLS), lambda i: (i, 0)),
                  const3(decay.shape), const3(q_decay.shape),
                  _layer_vec(layer, RET_WIDTH), _layer_vec(layer, RET_WIDTH)],
        out_specs=pl.BlockSpec((tm, D_MODEL), lambda i: (i, 0)),
        out_shape=jax.ShapeDtypeStruct((s, D_MODEL), BF16),
        scratch_shapes=[pltpu.VMEM((RET_HEADS, RET_HEAD_DIM, RET_HEAD_DIM), F32),
                        pltpu.VMEM((tm + WINDOW, 4 * KV_WIDTH), BF16)],
        compiler_params=_params(("arbitrary",), vmem),
        name="mixer",
    )(chunk_decay, sinks, rp, kdec, ap, decay, q_decay, gn_g, gn_b)


def _retention_constants():
    c = RET_CHUNK
    log_g = jnp.log1p(-jnp.exp2(-5.0 - jnp.arange(RET_HEADS, dtype=F32)))
    idx = jnp.arange(c, dtype=F32)
    diff = idx[:, None] - idx[None, :]
    decay = jnp.where(diff[None] >= 0, jnp.exp(log_g[:, None, None] * jnp.maximum(diff, 0.0)[None]), 0.0)
    q_decay = jnp.exp(log_g[:, None] * (idx[None, :] + 1.0))
    k_decay = jnp.exp(log_g[:, None] * (c - 1.0 - idx[None, :]))
    chunk_decay = jnp.exp(log_g * c)
    wide = lambda t, lanes: jnp.broadcast_to(t[:, :, None], (RET_HEADS, c, lanes))
    return (decay, wide(q_decay, RET_HEAD_DIM), chunk_decay), wide(k_decay, LANES)


def _mix_out_kernel(cat_ref, x_ref, p_ref, wo_ref, wg_ref, wp_ref, g_ref, b_ref, bg_ref, r_ref, xb_ref,
                    *, tm, sub):
    for r in range(0, tm, sub):
        rows = slice(r, r + sub)
        z = DEEPNORM_ALPHA * x_ref[rows, :] + jnp.dot(cat_ref[rows, :], wo_ref[...], preferred_element_type=F32)
        x1 = _layer_norm(z, g_ref[...], b_ref[...])
        x1b = x1.astype(BF16)
        gate = jnp.dot(x1b, wg_ref[...], preferred_element_type=F32) + bg_ref[...]
        ple = jnp.dot(p_ref[rows, :].astype(BF16), wp_ref[...], preferred_element_type=F32) * _sigmoid(gate)
        r_ref[rows, :] = DEEPNORM_ALPHA * x1 + ple
        xb_ref[rows, :] = x1b


def _mix_out(cat, x, p, layer, w_out_b, w_gate_b, w_proj_b, ln_g, ln_b, b_gate, tm, sub):
    s = x.shape[0]
    row = lambda cols: pl.BlockSpec((tm, cols), lambda i: (i, 0))
    resident = lambda rows_: pl.BlockSpec((None, rows_, D_MODEL), lambda i: (layer, 0, 0),
                                          pipeline_mode=pl.Buffered(1))
    vmem = (2 * D_MODEL * D_MODEL + PLE_DIM * D_MODEL) * 2 \
        + 2 * tm * (D_MODEL * 2 + D_MODEL * 4 + PLE_DIM * 4 + D_MODEL * 4 + D_MODEL * 2) + (12 << 20)
    return pl.pallas_call(
        functools.partial(_mix_out_kernel, tm=tm, sub=sub),
        grid=(s // tm,),
        in_specs=[row(D_MODEL), row(D_MODEL),
                  pl.BlockSpec((None, tm, PLE_DIM), lambda i: (layer, i, 0)),
                  resident(D_MODEL), resident(D_MODEL), resident(PLE_DIM),
                  _layer_vec(layer, D_MODEL), _layer_vec(layer, D_MODEL), _layer_vec(layer, D_MODEL)],
        out_specs=[row(D_MODEL), row(D_MODEL)],
        out_shape=[jax.ShapeDtypeStruct((s, D_MODEL), F32), jax.ShapeDtypeStruct((s, D_MODEL), BF16)],
        compiler_params=_params(("arbitrary",), vmem),
        name="mix_out",
    )(cat, x, p, w_out_b, w_gate_b, w_proj_b, ln_g, ln_b, b_gate)


FFN_TN = 512
FFN_BLOCKS = D_FF // FFN_TN


def _ffn_up_kernel(x_ref, wg_ref, wu_ref, cwg_ref, cwu_ref, cbg_ref, cbu_ref, o_ref, carry_ref, wgb_ref, wub_ref,
                   *, tm, sub):
    i = pl.program_id(0)
    j = pl.program_id(1)

    @pl.when(i == 0)
    def _():
        carry_ref[j] = jnp.zeros((2, SUBLANES, FFN_TN), F32)

    def conv(prev, h, cw_ref, cb_ref):
        ext = jnp.concatenate([prev, h], axis=0)
        y = (ext * cw_ref[2:3, :] + pltpu.roll(ext, 1, 0) * cw_ref[1:2, :]
             + pltpu.roll(ext, 2, 0) * cw_ref[0:1, :] + cb_ref[...])
        return y[SUBLANES:]

    wgb_ref[...] = wg_ref[...].astype(BF16)
    wub_ref[...] = wu_ref[...].astype(BF16)
    prev_g = carry_ref[j, 0]
    prev_u = carry_ref[j, 1]
    for r in range(0, tm, sub):
        x = x_ref[r:r + sub, :]
        hg = jnp.dot(x, wgb_ref[...], preferred_element_type=F32)
        hu = jnp.dot(x, wub_ref[...], preferred_element_type=F32)
        yg = conv(prev_g, hg, cwg_ref, cbg_ref)
        yu = conv(prev_u, hu, cwu_ref, cbu_ref)
        prev_g = hg[sub - SUBLANES:sub]
        prev_u = hu[sub - SUBLANES:sub]
        o_ref[r:r + sub, :] = (yg * _sigmoid(yg) * yu).astype(BF16)
    carry_ref[j, 0] = prev_g
    carry_ref[j, 1] = prev_u


def _ffn_up(xb, w_up, conv_w, conv_b, layer, tm, sub):
    s = xb.shape[0]
    nb = FFN_BLOCKS
    vmem = 2 * (tm * D_MODEL * 2 + 2 * D_MODEL * FFN_TN * 4 + tm * FFN_TN * 2) + 2 * D_MODEL * FFN_TN * 2 + (12 << 20)
    wspec = lambda off: pl.BlockSpec((None, D_MODEL, FFN_TN), lambda i, j: (layer, 0, j + off))
    cwspec = lambda off: pl.BlockSpec((None, CONV_WIDTH, FFN_TN), lambda i, j: (layer, 0, j + off))
    cbspec = lambda off: pl.BlockSpec((None, 1, FFN_TN), lambda i, j: (layer, 0, j + off))
    return pl.pallas_call(
        functools.partial(_ffn_up_kernel, tm=tm, sub=sub),
        grid=(s // tm, nb),
        in_specs=[pl.BlockSpec((tm, D_MODEL), lambda i, j: (i, 0)),
                  wspec(0), wspec(nb), cwspec(0), cwspec(nb), cbspec(0), cbspec(nb)],
        out_specs=pl.BlockSpec((tm, FFN_TN), lambda i, j: (i, j)),
        out_shape=jax.ShapeDtypeStruct((s, D_FF), BF16),
        scratch_shapes=[pltpu.VMEM((nb, 2, SUBLANES, FFN_TN), F32),
                        pltpu.VMEM((D_MODEL, FFN_TN), BF16), pltpu.VMEM((D_MODEL, FFN_TN), BF16)],
        compiler_params=_params(("arbitrary", "arbitrary"), vmem),
        name="ffn_up",
    )(xb, w_up, w_up, conv_w, conv_w, conv_b, conv_b)


def _ffn_down_kernel(a_ref, r_ref, w_ref, g_ref, b_ref, x_ref, xb_ref, *, tm, sub):
    for r in range(0, tm, sub):
        rows = slice(r, r + sub)
        z = r_ref[rows, :] + jnp.dot(a_ref[rows, :], w_ref[...], preferred_element_type=F32)
        x2 = _layer_norm(z, g_ref[...], b_ref[...])
        x_ref[rows, :] = x2
        xb_ref[rows, :] = x2.astype(BF16)


def _ffn_down(act, r, w_down_b, ln_g, ln_b, layer, tm, sub):
    s = r.shape[0]
    row = lambda cols: pl.BlockSpec((tm, cols), lambda i: (i, 0))
    vmem = D_FF * D_MODEL * 2 + 2 * tm * (D_FF * 2 + D_MODEL * 4 + D_MODEL * 4 + D_MODEL * 2) + (8 << 20)
    return pl.pallas_call(
        functools.partial(_ffn_down_kernel, tm=tm, sub=sub),
        grid=(s // tm,),
        in_specs=[row(D_FF), row(D_MODEL),
                  pl.BlockSpec((None, D_FF, D_MODEL), lambda i: (layer, 0, 0), pipeline_mode=pl.Buffered(1)),
                  _layer_vec(layer, D_MODEL), _layer_vec(layer, D_MODEL)],
        out_specs=[row(D_MODEL), row(D_MODEL)],
        out_shape=[jax.ShapeDtypeStruct((s, D_MODEL), F32), jax.ShapeDtypeStruct((s, D_MODEL), BF16)],
        compiler_params=_params(("arbitrary",), vmem),
        name="ffn_down",
    )(act, r, w_down_b, ln_g, ln_b)


def _tiles(s):
    pick = lambda tm, sub: (min(tm, s), min(sub, s))
    return dict(tables=pick(1024, 1024), ret_proj=pick(2048, 512), att_proj=pick(1024, 256),
                mixer=pick(512, 128), mix_out=pick(512, 256), ffn_up=pick(1024, 1024), ffn_down=pick(512, 256))


def kernel(x, p, positions, w_in, w_out, ret_norm_g, ret_norm_b, attn_sinks, ln1_g, ln1_b, w_ffn_up,
           ffn_conv_w, ffn_conv_b, w_ffn_down, ln2_g, ln2_b, w_ple_gate, b_ple_gate, w_ple_proj):
    batch, s, d = x.shape
    assert batch == 1 and d == D_MODEL and s % RET_CHUNK == 0
    t = _tiles(s)
    assert all(s % tm == 0 and tm % sub == 0 for tm, sub in t.values())

    xf = x.reshape(s, d)
    xb = xf.astype(BF16)
    pf = p.reshape(DEPTH, s, PLE_DIM)
    cos_r, sin_r, cos_a, sin_a, sin_b = _rot_tables(positions.reshape(s), t["tables"][0])
    ret_consts, k_decay = _retention_constants()
    vecs = lambda v: v.reshape(DEPTH, 1, -1)
    w_att_b = w_in[:, :, RET_COLS:].astype(BF16)
    w_out_b, w_down_b = w_out.astype(BF16), w_ffn_down.astype(BF16)
    w_gate_b, w_proj_b = w_ple_gate.astype(BF16), w_ple_proj.astype(BF16)
    gn_g, gn_b, g1, b1, g2, b2 = (vecs(v) for v in (ret_norm_g, ret_norm_b, ln1_g, ln1_b, ln2_g, ln2_b))
    bg, conv_b = vecs(b_ple_gate), vecs(ffn_conv_b)

    for l in range(DEPTH):
        rp, kdec = _ret_proj(xb, w_in, l, cos_r, sin_r, k_decay, *t["ret_proj"])
        ap = _att_proj(xb, w_att_b, l, cos_a, sin_a, sin_b, *t["att_proj"])
        cat = _mixer(rp, kdec, ap, ret_consts, gn_g, gn_b, attn_sinks, l, t["mixer"][0])
        r, xb = _mix_out(cat, xf, pf, l, w_out_b, w_gate_b, w_proj_b, g1, b1, bg, *t["mix_out"])
        act = _ffn_up(xb, w_ffn_up, ffn_conv_w, conv_b, l, *t["ffn_up"])
        xf, xb = _ffn_down(act, r, w_down_b, g2, b2, l, *t["ffn_down"])
    return xf.reshape(batch, s, d)
```

```python
import functools

import jax
import jax.numpy as jnp
from jax import lax
from jax.experimental import pallas as pl
from jax.experimental.pallas import tpu as pltpu

F32 = jnp.float32
BF16 = jnp.bfloat16

D_MODEL = 2048
DEPTH = 4
PLE_DIM = 256
RET_WIDTH = 1024
RET_HEADS = 4
RET_HEAD_DIM = 256
RET_CHUNK = 128
RET_ROT_BASE = 10000.0
ATT_HEAD_DIM = 64
ATT_WIDTH = 1024
ATT_HEADS = 16
ATT_KV_HEADS = 2
ATT_GROUP = ATT_HEADS // ATT_KV_HEADS
KV_WIDTH = 128
WINDOW = 128
ROPE_THETA = 10000.0
D_FF = 5632
CONV_WIDTH = 3
LN_EPS = 1e-5
GN_EPS = 1e-6
DEEPNORM_ALPHA = (2 * DEPTH) ** 0.25

LANES = 128
SUBLANES = 8
MXU_DIM = 256
VMEM_LIMIT_CAP = 58 * 1024 * 1024

RET_COLS = 4 * RET_WIDTH
ATT_IN_COLS = ATT_WIDTH + 2 * KV_WIDTH
ATT_OUT_COLS = ATT_WIDTH + 4 * KV_WIDTH
HALF = LANES // 2
LOG2_E = 1.4426950408889634
ATT_SCORE_SCALE = ATT_HEAD_DIM ** -0.5 * LOG2_E


def _params(semantics, vmem_bytes):
    return pltpu.CompilerParams(dimension_semantics=semantics,
                                vmem_limit_bytes=int(min(vmem_bytes, VMEM_LIMIT_CAP)))


def _layer_norm(z, g, b):
    mu = jnp.mean(z, axis=-1, keepdims=True)
    d = z - mu
    var = jnp.mean(d * d, axis=-1, keepdims=True)
    return d * lax.rsqrt(var + LN_EPS) * g + b


def _sigmoid(v):
    return 1.0 / (1.0 + jnp.exp(-v))


def _layer_vec(layer, cols):
    return pl.BlockSpec((None, 1, cols), lambda *_: (layer, 0, 0))


def _rot_tables_kernel(pos_ref, fr_ref, fa_ref, x_ref, cr_ref, sr_ref, ca_ref, sa_ref, sb_ref, xb_ref):
    xb_ref[...] = x_ref[...].astype(BF16)
    pos = pos_ref[...].astype(F32)
    ang_r = pos * fr_ref[...]
    cr_ref[...] = jnp.cos(ang_r)
    sr_ref[...] = jnp.sin(ang_r)
    ang_a = pos * fa_ref[...]
    c = jnp.cos(ang_a)
    s = jnp.sin(ang_a)
    lane = lax.broadcasted_iota(jnp.int32, c.shape, 1)
    first_half = (lane & (ATT_HEAD_DIM // 2)) == 0
    ca_ref[...] = c
    sa_ref[...] = jnp.where(first_half, -s, 0.0)
    sb_ref[...] = jnp.where(first_half, 0.0, s)


def _rot_tables(positions, xf, tm):
    s = positions.shape[0]
    ret_inv_freq = 1.0 / (RET_ROT_BASE ** jnp.linspace(0.0, 1.0, RET_HEAD_DIM // 2, dtype=F32))
    att_inv_freq = ROPE_THETA ** (-jnp.arange(0, ATT_HEAD_DIM, 2, dtype=F32) / ATT_HEAD_DIM)
    fr = ret_inv_freq.reshape(1, LANES)
    fa = jnp.tile(att_inv_freq, LANES // (ATT_HEAD_DIM // 2)).reshape(1, LANES)
    tab = jax.ShapeDtypeStruct((s, LANES), F32)
    row = pl.BlockSpec((tm, LANES), lambda i: (i, 0))
    const = pl.BlockSpec((1, LANES), lambda i: (0, 0))
    wide = pl.BlockSpec((tm, D_MODEL), lambda i: (i, 0))
    return pl.pallas_call(
        _rot_tables_kernel,
        grid=(s // tm,),
        in_specs=[pl.BlockSpec((tm, 1), lambda i: (i, 0)), const, const, wide],
        out_specs=[row] * 5 + [wide],
        out_shape=[tab] * 5 + [jax.ShapeDtypeStruct((s, D_MODEL), BF16)],
        compiler_params=_params(("arbitrary",), 2 * tm * D_MODEL * 6 + (16 << 20)),
        name="rot_tables",
    )(positions.reshape(s, 1), fr, fa, xf)


RET_TN = 512
RET_BLOCKS_PER_PART = RET_WIDTH // RET_TN
RET_HEADS_PER_BLOCK = RET_TN // RET_HEAD_DIM


def _ret_proj_kernel(x_ref, w_ref, c_ref, s_ref, kd_ref, o_ref, kdec_ref, wb_ref, *, tm, sub):
    j = pl.program_id(1)
    part = j // RET_BLOCKS_PER_PART

    def sub_blocks():
        wb_ref[...] = w_ref[...].astype(BF16)
        for r in range(0, tm, sub):
            rows = slice(r, r + sub)
            yield rows, jnp.dot(x_ref[rows, :], wb_ref[...], preferred_element_type=F32)

    def rotary(rows, acc, scale):
        c = c_ref[rows, :] * scale
        s = s_ref[rows, :] * scale
        for h in range(RET_HEADS_PER_BLOCK):
            lo = h * RET_HEAD_DIM
            mid = lo + RET_HEAD_DIM // 2
            a1 = acc[:, lo:mid]
            a2 = acc[:, mid:lo + RET_HEAD_DIM]
            yield h, lo, a1 * c - a2 * s
            yield h, mid, a1 * s + a2 * c

    @pl.when(part == 0)
    def _():
        for rows, acc in sub_blocks():
            for _, col, val in rotary(rows, acc, 1.0):
                o_ref[rows, col:col + LANES] = val.astype(BF16)

    @pl.when(part == 1)
    def _():
        for rows, acc in sub_blocks():
            for h, col, val in rotary(rows, acc, RET_HEAD_DIM ** -0.5):
                o_ref[rows, col:col + LANES] = val.astype(BF16)
                chunks = val.reshape(sub // RET_CHUNK, RET_CHUNK, LANES) * kd_ref[h][None]
                kdec_ref[rows, col:col + LANES] = chunks.reshape(sub, LANES).astype(BF16)

    @pl.when(part == 2)
    def _():
        for rows, acc in sub_blocks():
            o_ref[rows, :] = acc.astype(BF16)

    @pl.when(part == 3)
    def _():
        for rows, acc in sub_blocks():
            o_ref[rows, :] = (acc * _sigmoid(acc)).astype(BF16)


def _ret_proj(xb, w_in, layer, cos_r, sin_r, k_decay, tm, sub):
    s = xb.shape[0]
    vmem = 2 * (tm * D_MODEL * 2 + D_MODEL * RET_TN * 4 + 2 * tm * RET_TN * 2 + 2 * tm * LANES * 4) \
        + D_MODEL * RET_TN * 2 + (12 << 20)
    k_block = lambda j: jnp.clip(j - RET_BLOCKS_PER_PART, 0, RET_BLOCKS_PER_PART - 1)
    return pl.pallas_call(
        functools.partial(_ret_proj_kernel, tm=tm, sub=sub),
        grid=(s // tm, RET_COLS // RET_TN),
        in_specs=[pl.BlockSpec((tm, D_MODEL), lambda i, j: (i, 0)),
                  pl.BlockSpec((None, D_MODEL, RET_TN), lambda i, j: (layer, 0, j)),
                  pl.BlockSpec((tm, LANES), lambda i, j: (i, 0)),
                  pl.BlockSpec((tm, LANES), lambda i, j: (i, 0)),
                  pl.BlockSpec((RET_HEADS_PER_BLOCK, RET_CHUNK, LANES), lambda i, j: (k_block(j), 0, 0))],
        out_specs=[pl.BlockSpec((tm, RET_TN), lambda i, j: (i, j)),
                   pl.BlockSpec((tm, RET_TN), lambda i, j: (i, k_block(j)))],
        out_shape=[jax.ShapeDtypeStruct((s, RET_COLS), BF16), jax.ShapeDtypeStruct((s, RET_WIDTH), BF16)],
        scratch_shapes=[pltpu.VMEM((D_MODEL, RET_TN), BF16)],
        compiler_params=_params(("arbitrary", "arbitrary"), vmem),
        name="ret_proj",
    )(xb, w_in, cos_r, sin_r, k_decay)


ATT_W_BLOCKS = ATT_IN_COLS // MXU_DIM


def _att_proj_kernel(x_ref, *refs, tm, sub):
    w_refs = refs[:ATT_W_BLOCKS]
    c_ref, sa_ref, sb_ref, o_ref = refs[ATT_W_BLOCKS:]

    for r in range(0, tm, sub):
        rows = slice(r, r + sub)
        c = c_ref[rows, :]
        sa = sa_ref[rows, :]
        sb = sb_ref[rows, :]

        def rot(a):
            return (a * c + pltpu.roll(a, LANES - ATT_HEAD_DIM // 2, 1) * sa
                    + pltpu.roll(a, ATT_HEAD_DIM // 2, 1) * sb)

        x = x_ref[rows, :]
        for wb in range(ATT_W_BLOCKS):
            acc = jnp.dot(x, w_refs[wb][...], preferred_element_type=F32)
            for half in range(MXU_DIM // LANES):
                col = wb * MXU_DIM + half * LANES
                a = acc[:, half * LANES:(half + 1) * LANES]
                if col < ATT_WIDTH:
                    o_ref[rows, col:col + LANES] = (rot(a) * ATT_SCORE_SCALE).astype(BF16)
                elif col < ATT_WIDTH + KV_WIDTH:
                    k = rot(a)
                    o_ref[rows, col:col + LANES] = k.astype(BF16)
                    o_ref[rows, col + 2 * LANES:col + 3 * LANES] = pltpu.roll(k, HALF, 1).astype(BF16)
                else:
                    o_ref[rows, col:col + LANES] = a.astype(BF16)
                    o_ref[rows, col + 2 * LANES:col + 3 * LANES] = pltpu.roll(a, HALF, 1).astype(BF16)


def _att_proj(xb, w_att_b, layer, cos_a, sin_a, sin_b, tm, sub):
    s = xb.shape[0]
    vmem = 2 * (tm * D_MODEL * 2 + D_MODEL * ATT_IN_COLS * 2 + tm * ATT_OUT_COLS * 2 + 3 * tm * LANES * 4) + (12 << 20)
    row = pl.BlockSpec((tm, LANES), lambda i: (i, 0))
    w_specs = [pl.BlockSpec((None, D_MODEL, MXU_DIM), functools.partial(lambda i, wb: (layer, 0, wb), wb=wb))
               for wb in range(ATT_W_BLOCKS)]
    return pl.pallas_call(
        functools.partial(_att_proj_kernel, tm=tm, sub=sub),
        grid=(s // tm,),
        in_specs=[pl.BlockSpec((tm, D_MODEL), lambda i: (i, 0))] + w_specs + [row, row, row],
        out_specs=pl.BlockSpec((tm, ATT_OUT_COLS), lambda i: (i, 0)),
        out_shape=jax.ShapeDtypeStruct((s, ATT_OUT_COLS), BF16),
        compiler_params=_params(("arbitrary",), vmem),
        name="att_proj",
    )(xb, *([w_att_b] * ATT_W_BLOCKS), cos_a, sin_a, sin_b)


def _mixer_kernel(cd_ref, sink_ref, r_ref, kdec_ref, a_ref, dec_ref, qd_ref, g_ref, b_ref, o_ref,
                  state_ref, kv_ref, *, tm, layer):
    i = pl.program_id(0)
    n_chunks = tm // RET_CHUNK
    kv_cols = 4 * KV_WIDTH

    @pl.when(i == 0)
    def _():
        state_ref[...] = jnp.zeros_like(state_ref)
        kv_ref[0:WINDOW, :] = jnp.zeros((WINDOW, kv_cols), BF16)

    @pl.when(i > 0)
    def _():
        kv_ref[0:WINDOW, :] = kv_ref[tm:tm + WINDOW, :]

    kv_ref[WINDOW:tm + WINDOW, :] = a_ref[:, ATT_WIDTH:ATT_WIDTH + kv_cols]

    contract_last = (((1,), (1,)), ((), ()))
    contract_first = (((0,), (0,)), ((), ()))
    qi = lax.broadcasted_iota(jnp.int32, (WINDOW, 2 * WINDOW), 0)
    kj = lax.broadcasted_iota(jnp.int32, (WINDOW, 2 * WINDOW), 1)
    band = (kj > qi) & (kj <= qi + WINDOW)
    low_lanes = lax.broadcasted_iota(jnp.int32, (2 * WINDOW, LANES), 1) < HALF

    def chunk(c, carry):
        r0 = pl.multiple_of(c * RET_CHUNK, RET_CHUNK)
        rows = pl.ds(r0, RET_CHUNK)

        for h in range(RET_HEADS):
            lo = h * RET_HEAD_DIM
            hi = lo + RET_HEAD_DIM
            q = r_ref[rows, lo:hi]
            k = r_ref[rows, RET_WIDTH + lo:RET_WIDTH + hi]
            v = r_ref[rows, 2 * RET_WIDTH + lo:2 * RET_WIDTH + hi]
            silu_gate = r_ref[rows, 3 * RET_WIDTH + lo:3 * RET_WIDTH + hi].astype(F32)
            st = state_ref[h]
            sc = lax.dot_general(q, k, contract_last, preferred_element_type=F32) * dec_ref[h]
            inner = jnp.dot(sc.astype(BF16), v, preferred_element_type=F32)
            cross = jnp.dot(q, st.astype(BF16), preferred_element_type=F32) * qd_ref[h]
            state_ref[h] = st * cd_ref[h] + lax.dot_general(kdec_ref[rows, lo:hi], v, contract_first,
                                                            preferred_element_type=F32)
            y = inner + cross
            mu = jnp.mean(y, axis=-1, keepdims=True)
            d = y - mu
            var = jnp.mean(d * d, axis=-1, keepdims=True)
            yn = d * lax.rsqrt(var + GN_EPS) * g_ref[:, lo:hi] + b_ref[:, lo:hi]
            o_ref[rows, lo:hi] = (silu_gate * yn).astype(BF16)

        kk = kv_ref[pl.ds(r0, 2 * WINDOW), :]
        k_nat = kk[:, 0:LANES]
        v_nat = kk[:, LANES:2 * LANES]
        k_swp = kk[:, 2 * LANES:3 * LANES]
        v_swp = kk[:, 3 * LANES:4 * LANES]
        zero = jnp.zeros_like(k_nat)
        k_ext = ((jnp.where(low_lanes, k_nat, zero), jnp.where(low_lanes, zero, k_swp)),
                 (jnp.where(low_lanes, k_swp, zero), jnp.where(low_lanes, zero, k_nat)))
        v_ext = ((jnp.where(low_lanes, v_nat, zero), jnp.where(low_lanes, zero, v_swp)),
                 (jnp.where(low_lanes, v_swp, zero), jnp.where(low_lanes, zero, v_nat)))
        first_block = (i * n_chunks + c) == 0
        mask = band & ((kj >= WINDOW) | jnp.logical_not(first_block))
        pairs_per_kv = ATT_GROUP // 2
        for kvh in range(ATT_KV_HEADS):
            for pr in range(pairs_per_kv):
                pair = kvh * pairs_per_kv + pr
                q_pair = a_ref[rows, pair * LANES:(pair + 1) * LANES]
                o_pair = None
                for par in range(2):
                    sink = sink_ref[layer, 2 * pair + par] * LOG2_E
                    sc = lax.dot_general(q_pair, k_ext[kvh][par], contract_last, preferred_element_type=F32)
                    sc = jnp.where(mask, sc, -jnp.inf)
                    m = jnp.maximum(jnp.max(sc, axis=-1, keepdims=True), sink)
                    p = jnp.exp2(sc - m)
                    den = jnp.sum(p, axis=-1, keepdims=True) + jnp.exp2(sink - m)
                    o = jnp.dot(p.astype(BF16), v_ext[kvh][par], preferred_element_type=F32) * (1.0 / den)
                    o_pair = o if o_pair is None else o_pair + o
                o_ref[rows, RET_WIDTH + pair * LANES:RET_WIDTH + (pair + 1) * LANES] = o_pair.astype(BF16)
        return carry

    lax.fori_loop(0, n_chunks, chunk, 0, unroll=2)


def _mixer(rp, kdec, ap, ret_consts, gn_g, gn_b, sinks, layer, tm):
    s = rp.shape[0]
    decay, q_decay, chunk_decay = ret_consts
    smem = pl.BlockSpec(memory_space=pltpu.SMEM)
    const3 = lambda shape: pl.BlockSpec(shape, lambda i: (0, 0, 0))
    vmem = 2 * (tm * (RET_COLS + RET_WIDTH) * 2 + tm * ATT_OUT_COLS * 2 + tm * D_MODEL * 2) \
        + 2 * 4 * (decay.size + q_decay.size) \
        + RET_HEADS * RET_HEAD_DIM * RET_HEAD_DIM * 4 + (tm + WINDOW) * 4 * KV_WIDTH * 2 + (8 << 20)
    return pl.pallas_call(
        functools.partial(_mixer_kernel, tm=tm, layer=layer),
        grid=(s // tm,),
        in_specs=[smem, smem,
                  pl.BlockSpec((tm, RET_COLS), lambda i: (i, 0)),
                  pl.BlockSpec((tm, RET_WIDTH), lambda i: (i, 0)),
                  pl.BlockSpec((tm, ATT_OUT_COLS), lambda i: (i, 0)),
                  const3(decay.shape), const3(q_decay.shape),
                  _layer_vec(layer, RET_WIDTH), _layer_vec(layer, RET_WIDTH)],
        out_specs=pl.BlockSpec((tm, D_MODEL), lambda i: (i, 0)),
        out_shape=jax.ShapeDtypeStruct((s, D_MODEL), BF16),
        scratch_shapes=[pltpu.VMEM((RET_HEADS, RET_HEAD_DIM, RET_HEAD_DIM), F32),
                        pltpu.VMEM((tm + WINDOW, 4 * KV_WIDTH), BF16)],
        compiler_params=_params(("arbitrary",), vmem),
        name="mixer",
    )(chunk_decay, sinks, rp, kdec, ap, decay, q_decay, gn_g, gn_b)


def _retention_constants():
    c = RET_CHUNK
    log_g = jnp.log1p(-jnp.exp2(-5.0 - jnp.arange(RET_HEADS, dtype=F32)))
    idx = jnp.arange(c, dtype=F32)
    diff = idx[:, None] - idx[None, :]
    decay = jnp.where(diff[None] >= 0, jnp.exp(log_g[:, None, None] * jnp.maximum(diff, 0.0)[None]), 0.0)
    q_decay = jnp.exp(log_g[:, None] * (idx[None, :] + 1.0))
    k_decay = jnp.exp(log_g[:, None] * (c - 1.0 - idx[None, :]))
    chunk_decay = jnp.exp(log_g * c)
    wide = lambda t, lanes: jnp.broadcast_to(t[:, :, None], (RET_HEADS, c, lanes))
    return (decay, wide(q_decay, RET_HEAD_DIM), chunk_decay), wide(k_decay, LANES)


def _mix_out_kernel(cat_ref, x_ref, p_ref, wo_ref, wg_ref, wp_ref, g_ref, b_ref, bg_ref, r_ref, xb_ref,
                    *, tm, sub):
    for r in range(0, tm, sub):
        rows = slice(r, r + sub)
        z = DEEPNORM_ALPHA * x_ref[rows, :] + jnp.dot(cat_ref[rows, :], wo_ref[...], preferred_element_type=F32)
        x1 = _layer_norm(z, g_ref[...], b_ref[...])
        x1b = x1.astype(BF16)
        gate = jnp.dot(x1b, wg_ref[...], preferred_element_type=F32) + bg_ref[...]
        ple = jnp.dot(p_ref[rows, :].astype(BF16), wp_ref[...], preferred_element_type=F32) * _sigmoid(gate)
        r_ref[rows, :] = DEEPNORM_ALPHA * x1 + ple
        xb_ref[rows, :] = x1b


def _mix_out(cat, x, p, layer, w_out_b, w_gate_b, w_proj_b, ln_g, ln_b, b_gate, tm, sub):
    s = x.shape[0]
    row = lambda cols: pl.BlockSpec((tm, cols), lambda i: (i, 0))
    resident = lambda rows_: pl.BlockSpec((None, rows_, D_MODEL), lambda i: (layer, 0, 0),
                                          pipeline_mode=pl.Buffered(1))
    vmem = (2 * D_MODEL * D_MODEL + PLE_DIM * D_MODEL) * 2 \
        + 2 * tm * (D_MODEL * 2 + D_MODEL * 4 + PLE_DIM * 4 + D_MODEL * 4 + D_MODEL * 2) + (12 << 20)
    return pl.pallas_call(
        functools.partial(_mix_out_kernel, tm=tm, sub=sub),
        grid=(s // tm,),
        in_specs=[row(D_MODEL), row(D_MODEL),
                  pl.BlockSpec((None, tm, PLE_DIM), lambda i: (layer, i, 0)),
                  resident(D_MODEL), resident(D_MODEL), resident(PLE_DIM),
                  _layer_vec(layer, D_MODEL), _layer_vec(layer, D_MODEL), _layer_vec(layer, D_MODEL)],
        out_specs=[row(D_MODEL), row(D_MODEL)],
        out_shape=[jax.ShapeDtypeStruct((s, D_MODEL), F32), jax.ShapeDtypeStruct((s, D_MODEL), BF16)],
        compiler_params=_params(("arbitrary",), vmem),
        name="mix_out",
    )(cat, x, p, w_out_b, w_gate_b, w_proj_b, ln_g, ln_b, b_gate)


FFN_TN = 512
FFN_BLOCKS = D_FF // FFN_TN


def _ffn_up_kernel(x_ref, wg_ref, wu_ref, cwg_ref, cwu_ref, cbg_ref, cbu_ref, o_ref, carry_ref, wgb_ref, wub_ref,
                   *, tm, sub):
    i = pl.program_id(0)
    j = pl.program_id(1)

    @pl.when(i == 0)
    def _():
        carry_ref[j] = jnp.zeros((2, SUBLANES, FFN_TN), F32)

    def conv(prev, h, cw_ref, cb_ref):
        ext = jnp.concatenate([prev, h], axis=0)
        y = (ext * cw_ref[2:3, :] + pltpu.roll(ext, 1, 0) * cw_ref[1:2, :]
             + pltpu.roll(ext, 2, 0) * cw_ref[0:1, :] + cb_ref[...])
        return y[SUBLANES:]

    wgb_ref[...] = wg_ref[...].astype(BF16)
    wub_ref[...] = wu_ref[...].astype(BF16)
    prev_g = carry_ref[j, 0]
    prev_u = carry_ref[j, 1]
    for r in range(0, tm, sub):
        x = x_ref[r:r + sub, :]
        hg = jnp.dot(x, wgb_ref[...], preferred_element_type=F32)
        hu = jnp.dot(x, wub_ref[...], preferred_element_type=F32)
        yg = conv(prev_g, hg, cwg_ref, cbg_ref)
        yu = conv(prev_u, hu, cwu_ref, cbu_ref)
        prev_g = hg[sub - SUBLANES:sub]
        prev_u = hu[sub - SUBLANES:sub]
        o_ref[r:r + sub, :] = (yg * _sigmoid(yg) * yu).astype(BF16)
    carry_ref[j, 0] = prev_g
    carry_ref[j, 1] = prev_u


def _ffn_up(xb, w_up, conv_w, conv_b, layer, tm, sub):
    s = xb.shape[0]
    nb = FFN_BLOCKS
    vmem = 2 * (tm * D_MODEL * 2 + 2 * D_MODEL * FFN_TN * 4 + tm * FFN_TN * 2) + 2 * D_MODEL * FFN_TN * 2 + (12 << 20)
    wspec = lambda off: pl.BlockSpec((None, D_MODEL, FFN_TN), lambda i, j: (layer, 0, j + off))
    cwspec = lambda off: pl.BlockSpec((None, CONV_WIDTH, FFN_TN), lambda i, j: (layer, 0, j + off))
    cbspec = lambda off: pl.BlockSpec((None, 1, FFN_TN), lambda i, j: (layer, 0, j + off))
    return pl.pallas_call(
        functools.partial(_ffn_up_kernel, tm=tm, sub=sub),
        grid=(s // tm, nb),
        in_specs=[pl.BlockSpec((tm, D_MODEL), lambda i, j: (i, 0)),
                  wspec(0), wspec(nb), cwspec(0), cwspec(nb), cbspec(0), cbspec(nb)],
        out_specs=pl.BlockSpec((tm, FFN_TN), lambda i, j: (i, j)),
        out_shape=jax.ShapeDtypeStruct((s, D_FF), BF16),
        scratch_shapes=[pltpu.VMEM((nb, 2, SUBLANES, FFN_TN), F32),
                        pltpu.VMEM((D_MODEL, FFN_TN), BF16), pltpu.VMEM((D_MODEL, FFN_TN), BF16)],
        compiler_params=_params(("arbitrary", "arbitrary"), vmem),
        name="ffn_up",
    )(xb, w_up, w_up, conv_w, conv_w, conv_b, conv_b)


def _ffn_down_kernel(a_ref, r_ref, w_ref, g_ref, b_ref, x_ref, xb_ref, *, tm, sub):
    for r in range(0, tm, sub):
        rows = slice(r, r + sub)
        z = r_ref[rows, :] + jnp.dot(a_ref[rows, :], w_ref[...], preferred_element_type=F32)
        x2 = _layer_norm(z, g_ref[...], b_ref[...])
        x_ref[rows, :] = x2
        xb_ref[rows, :] = x2.astype(BF16)


def _ffn_down(act, r, w_down_b, ln_g, ln_b, layer, tm, sub):
    s = r.shape[0]
    row = lambda cols: pl.BlockSpec((tm, cols), lambda i: (i, 0))
    vmem = D_FF * D_MODEL * 2 + 2 * tm * (D_FF * 2 + D_MODEL * 4 + D_MODEL * 4 + D_MODEL * 2) + (8 << 20)
    return pl.pallas_call(
        functools.partial(_ffn_down_kernel, tm=tm, sub=sub),
        grid=(s // tm,),
        in_specs=[row(D_FF), row(D_MODEL),
                  pl.BlockSpec((None, D_FF, D_MODEL), lambda i: (layer, 0, 0), pipeline_mode=pl.Buffered(1)),
                  _layer_vec(layer, D_MODEL), _layer_vec(layer, D_MODEL)],
        out_specs=[row(D_MODEL), row(D_MODEL)],
        out_shape=[jax.ShapeDtypeStruct((s, D_MODEL), F32), jax.ShapeDtypeStruct((s, D_MODEL), BF16)],
        compiler_params=_params(("arbitrary",), vmem),
        name="ffn_down",
    )(act, r, w_down_b, ln_g, ln_b)


def _tiles(s):
    pick = lambda tm, sub: (min(tm, s), min(sub, s))
    return dict(tables=pick(1024, 1024), ret_proj=pick(2048, 512), att_proj=pick(1024, 256),
                mixer=pick(1024, 128), mix_out=pick(512, 256), ffn_up=pick(1024, 1024), ffn_down=pick(512, 256))


def kernel(x, p, positions, w_in, w_out, ret_norm_g, ret_norm_b, attn_sinks, ln1_g, ln1_b, w_ffn_up,
           ffn_conv_w, ffn_conv_b, w_ffn_down, ln2_g, ln2_b, w_ple_gate, b_ple_gate, w_ple_proj):
    batch, s, d = x.shape
    assert batch == 1 and d == D_MODEL and s % RET_CHUNK == 0
    t = _tiles(s)
    assert all(s % tm == 0 and tm % sub == 0 for tm, sub in t.values())

    xf = x.reshape(s, d)
    pf = p.reshape(DEPTH, s, PLE_DIM)
    cos_r, sin_r, cos_a, sin_a, sin_b, xb = _rot_tables(positions.reshape(s), xf, t["tables"][0])
    ret_consts, k_decay = _retention_constants()
    vecs = lambda v: v.reshape(DEPTH, 1, -1)
    w_att_b = w_in[:, :, RET_COLS:].astype(BF16)
    w_out_b, w_down_b = w_out.astype(BF16), w_ffn_down.astype(BF16)
    w_gate_b, w_proj_b = w_ple_gate.astype(BF16), w_ple_proj.astype(BF16)
    gn_g, gn_b, g1, b1, g2, b2 = (vecs(v) for v in (ret_norm_g, ret_norm_b, ln1_g, ln1_b, ln2_g, ln2_b))
    bg, conv_b = vecs(b_ple_gate), vecs(ffn_conv_b)

    for l in range(DEPTH):
        rp, kdec = _ret_proj(xb, w_in, l, cos_r, sin_r, k_decay, *t["ret_proj"])
        ap = _att_proj(xb, w_att_b, l, cos_a, sin_a, sin_b, *t["att_proj"])
        cat = _mixer(rp, kdec, ap, ret_consts, gn_g, gn_b, attn_sinks, l, t["mixer"][0])
        r, xb = _mix_out(cat, xf, pf, l, w_out_b, w_gate_b, w_proj_b, g1, b1, bg, *t["mix_out"])
        act = _ffn_up(xb, w_ffn_up, ffn_conv_w, conv_b, l, *t["ffn_up"])
        xf, xb = _ffn_down(act, r, w_down_b, g2, b2, l, *t["ffn_down"])
    return xf.reshape(batch, s, d)
```

```python
import functools

import jax
import jax.numpy as jnp
from jax import lax
from jax.experimental import pallas as pl
from jax.experimental.pallas import tpu as pltpu

F32 = jnp.float32
BF16 = jnp.bfloat16

D_MODEL = 2048
DEPTH = 4
PLE_DIM = 256
RET_WIDTH = 1024
RET_HEADS = 4
RET_HEAD_DIM = 256
RET_CHUNK = 128
RET_ROT_BASE = 10000.0
ATT_HEAD_DIM = 64
ATT_WIDTH = 1024
ATT_HEADS = 16
ATT_KV_HEADS = 2
ATT_GROUP = ATT_HEADS // ATT_KV_HEADS
KV_WIDTH = 128
WINDOW = 128
ROPE_THETA = 10000.0
D_FF = 5632
CONV_WIDTH = 3
LN_EPS = 1e-5
GN_EPS = 1e-6
DEEPNORM_ALPHA = (2 * DEPTH) ** 0.25

LANES = 128
SUBLANES = 8
MXU_DIM = 256
VMEM_LIMIT_CAP = 58 * 1024 * 1024

RET_COLS = 4 * RET_WIDTH
ATT_IN_COLS = ATT_WIDTH + 2 * KV_WIDTH
ATT_OUT_COLS = ATT_WIDTH + 4 * KV_WIDTH
HALF = LANES // 2
LOG2_E = 1.4426950408889634
ATT_SCORE_SCALE = ATT_HEAD_DIM ** -0.5 * LOG2_E


def _params(semantics, vmem_bytes):
    return pltpu.CompilerParams(dimension_semantics=semantics,
                                vmem_limit_bytes=int(min(vmem_bytes, VMEM_LIMIT_CAP)))


def _layer_norm(z, g, b):
    mu = jnp.mean(z, axis=-1, keepdims=True)
    d = z - mu
    var = jnp.mean(d * d, axis=-1, keepdims=True)
    return d * lax.rsqrt(var + LN_EPS) * g + b


def _sigmoid(v):
    return 1.0 / (1.0 + jnp.exp(-v))


BF16_SUBLANE_ROWS = 16


def _slab_rows(rows, steps):
    assert rows % steps == 0 and (rows // steps) % BF16_SUBLANE_ROWS == 0, (rows, steps)
    return rows // steps


def _layer_vec(layer, cols):
    return pl.BlockSpec((None, 1, cols), lambda *_: (layer, 0, 0))


def _rot_tables_kernel(pos_ref, fr_ref, fa_ref, x_ref, cr_ref, sr_ref, ca_ref, sa_ref, sb_ref, xb_ref):
    xb_ref[...] = x_ref[...].astype(BF16)
    pos = pos_ref[...].astype(F32)
    ang_r = pos * fr_ref[...]
    cr_ref[...] = jnp.cos(ang_r)
    sr_ref[...] = jnp.sin(ang_r)
    ang_a = pos * fa_ref[...]
    c = jnp.cos(ang_a)
    s = jnp.sin(ang_a)
    lane = lax.broadcasted_iota(jnp.int32, c.shape, 1)
    first_half = (lane & (ATT_HEAD_DIM // 2)) == 0
    ca_ref[...] = c
    sa_ref[...] = jnp.where(first_half, -s, 0.0)
    sb_ref[...] = jnp.where(first_half, 0.0, s)


def _rot_tables(positions, xf, tm):
    s = positions.shape[0]
    ret_inv_freq = 1.0 / (RET_ROT_BASE ** jnp.linspace(0.0, 1.0, RET_HEAD_DIM // 2, dtype=F32))
    att_inv_freq = ROPE_THETA ** (-jnp.arange(0, ATT_HEAD_DIM, 2, dtype=F32) / ATT_HEAD_DIM)
    fr = ret_inv_freq.reshape(1, LANES)
    fa = jnp.tile(att_inv_freq, LANES // (ATT_HEAD_DIM // 2)).reshape(1, LANES)
    tab = jax.ShapeDtypeStruct((s, LANES), F32)
    row = pl.BlockSpec((tm, LANES), lambda i: (i, 0))
    const = pl.BlockSpec((1, LANES), lambda i: (0, 0))
    wide = pl.BlockSpec((tm, D_MODEL), lambda i: (i, 0))
    return pl.pallas_call(
        _rot_tables_kernel,
        grid=(s // tm,),
        in_specs=[pl.BlockSpec((tm, 1), lambda i: (i, 0)), const, const, wide],
        out_specs=[row] * 5 + [wide],
        out_shape=[tab] * 5 + [jax.ShapeDtypeStruct((s, D_MODEL), BF16)],
        compiler_params=_params(("arbitrary",), 2 * tm * D_MODEL * 6 + (16 << 20)),
        name="rot_tables",
    )(positions.reshape(s, 1), fr, fa, xf)


RET_TN = 512
RET_BLOCKS_PER_PART = RET_WIDTH // RET_TN
RET_HEADS_PER_BLOCK = RET_TN // RET_HEAD_DIM


def _ret_proj_kernel(x_ref, w_ref, c_ref, s_ref, kd_ref, wrow_ref, o_ref, kdec_ref, watt_ref, wb_ref, *, tm, sub):
    j = pl.program_id(0)
    part = j // RET_BLOCKS_PER_PART
    watt_ref[...] = wrow_ref[:, RET_COLS:].astype(BF16)

    @pl.when(pl.program_id(1) == 0)
    def _():
        wb_ref[...] = w_ref[...].astype(BF16)

    def sub_blocks():
        for r in range(0, tm, sub):
            rows = slice(r, r + sub)
            yield rows, jnp.dot(x_ref[rows, :], wb_ref[...], preferred_element_type=F32)

    def rotary(rows, acc, scale):
        c = c_ref[rows, :] * scale
        s = s_ref[rows, :] * scale
        for h in range(RET_HEADS_PER_BLOCK):
            lo = h * RET_HEAD_DIM
            mid = lo + RET_HEAD_DIM // 2
            a1 = acc[:, lo:mid]
            a2 = acc[:, mid:lo + RET_HEAD_DIM]
            yield h, lo, a1 * c - a2 * s
            yield h, mid, a1 * s + a2 * c

    @pl.when(part == 0)
    def _():
        for rows, acc in sub_blocks():
            for _, col, val in rotary(rows, acc, 1.0):
                o_ref[rows, col:col + LANES] = val.astype(BF16)

    @pl.when(part == 1)
    def _():
        for rows, acc in sub_blocks():
            for h, col, val in rotary(rows, acc, RET_HEAD_DIM ** -0.5):
                o_ref[rows, col:col + LANES] = val.astype(BF16)
                chunks = val.reshape(sub // RET_CHUNK, RET_CHUNK, LANES) * kd_ref[h][None]
                kdec_ref[rows, col:col + LANES] = chunks.reshape(sub, LANES).astype(BF16)

    @pl.when(part == 2)
    def _():
        for rows, acc in sub_blocks():
            o_ref[rows, :] = acc.astype(BF16)

    @pl.when(part == 3)
    def _():
        for rows, acc in sub_blocks():
            o_ref[rows, :] = (acc * _sigmoid(acc)).astype(BF16)


def _ret_proj(xb, w_in, layer, cos_r, sin_r, k_decay, tm, sub):
    s = xb.shape[0]
    vmem = 2 * (tm * D_MODEL * 2 + D_MODEL * RET_TN * 4 + 2 * tm * RET_TN * 2 + 2 * tm * LANES * 4) \
        + D_MODEL * RET_TN * 2 + (12 << 20)
    nj = RET_COLS // RET_TN
    ni = s // tm
    k_col = lambda j: jnp.clip(j - RET_BLOCKS_PER_PART, 0, RET_BLOCKS_PER_PART - 1)
    k_row = lambda j, i: jnp.where(j < RET_BLOCKS_PER_PART, 0, jnp.where(j < 2 * RET_BLOCKS_PER_PART, i, ni - 1))
    slab = _slab_rows(D_MODEL, ni * nj)
    return pl.pallas_call(
        functools.partial(_ret_proj_kernel, tm=tm, sub=sub),
        grid=(nj, ni),
        in_specs=[pl.BlockSpec((tm, D_MODEL), lambda j, i: (i, 0)),
                  pl.BlockSpec((None, D_MODEL, RET_TN), lambda j, i: (layer, 0, j)),
                  pl.BlockSpec((tm, LANES), lambda j, i: (i, 0)),
                  pl.BlockSpec((tm, LANES), lambda j, i: (i, 0)),
                  pl.BlockSpec((RET_HEADS_PER_BLOCK, RET_CHUNK, LANES), lambda j, i: (k_col(j), 0, 0)),
                  pl.BlockSpec((None, slab, RET_COLS + ATT_IN_COLS), lambda j, i: (layer, j * ni + i, 0))],
        out_specs=[pl.BlockSpec((tm, RET_TN), lambda j, i: (i, j)),
                   pl.BlockSpec((tm, RET_TN), lambda j, i: (k_row(j, i), k_col(j))),
                   pl.BlockSpec((slab, ATT_IN_COLS), lambda j, i: (j * ni + i, 0))],
        out_shape=[jax.ShapeDtypeStruct((s, RET_COLS), BF16), jax.ShapeDtypeStruct((s, RET_WIDTH), BF16),
                   jax.ShapeDtypeStruct((D_MODEL, ATT_IN_COLS), BF16)],
        scratch_shapes=[pltpu.VMEM((D_MODEL, RET_TN), BF16)],
        compiler_params=_params(("arbitrary", "arbitrary"), vmem),
        name="ret_proj",
    )(xb, w_in, cos_r, sin_r, k_decay, w_in)


ATT_W_BLOCKS = ATT_IN_COLS // MXU_DIM


def _att_proj_kernel(x_ref, *refs, tm, sub):
    w_refs = refs[:ATT_W_BLOCKS]
    c_ref, sa_ref, sb_ref, o_ref = refs[ATT_W_BLOCKS:]

    for r in range(0, tm, sub):
        rows = slice(r, r + sub)
        c = c_ref[rows, :]
        sa = sa_ref[rows, :]
        sb = sb_ref[rows, :]

        def rot(a):
            return (a * c + pltpu.roll(a, LANES - ATT_HEAD_DIM // 2, 1) * sa
                    + pltpu.roll(a, ATT_HEAD_DIM // 2, 1) * sb)

        x = x_ref[rows, :]
        for wb in range(ATT_W_BLOCKS):
            acc = jnp.dot(x, w_refs[wb][...], preferred_element_type=F32)
            for half in range(MXU_DIM // LANES):
                col = wb * MXU_DIM + half * LANES
                a = acc[:, half * LANES:(half + 1) * LANES]
                if col < ATT_WIDTH:
                    o_ref[rows, col:col + LANES] = (rot(a) * ATT_SCORE_SCALE).astype(BF16)
                elif col < ATT_WIDTH + KV_WIDTH:
                    k = rot(a)
                    o_ref[rows, col:col + LANES] = k.astype(BF16)
                    o_ref[rows, col + 2 * LANES:col + 3 * LANES] = pltpu.roll(k, HALF, 1).astype(BF16)
                else:
                    o_ref[rows, col:col + LANES] = a.astype(BF16)
                    o_ref[rows, col + 2 * LANES:col + 3 * LANES] = pltpu.roll(a, HALF, 1).astype(BF16)


def _att_proj(xb, w_att_b, cos_a, sin_a, sin_b, tm, sub):
    s = xb.shape[0]
    vmem = 2 * (tm * D_MODEL * 2 + D_MODEL * ATT_IN_COLS * 2 + tm * ATT_OUT_COLS * 2 + 3 * tm * LANES * 4) + (12 << 20)
    row = pl.BlockSpec((tm, LANES), lambda i: (i, 0))
    w_specs = [pl.BlockSpec((D_MODEL, MXU_DIM), functools.partial(lambda i, wb: (0, wb), wb=wb))
               for wb in range(ATT_W_BLOCKS)]
    return pl.pallas_call(
        functools.partial(_att_proj_kernel, tm=tm, sub=sub),
        grid=(s // tm,),
        in_specs=[pl.BlockSpec((tm, D_MODEL), lambda i: (i, 0))] + w_specs + [row, row, row],
        out_specs=pl.BlockSpec((tm, ATT_OUT_COLS), lambda i: (i, 0)),
        out_shape=jax.ShapeDtypeStruct((s, ATT_OUT_COLS), BF16),
        compiler_params=_params(("arbitrary",), vmem),
        name="att_proj",
    )(xb, *([w_att_b] * ATT_W_BLOCKS), cos_a, sin_a, sin_b)


def _mixer_kernel(cd_ref, sink_ref, r_ref, kdec_ref, a_ref, dec_ref, qd_ref, g_ref, b_ref, wo_ref, wg_ref, wp_ref,
                  o_ref, wob_ref, wgb_ref, wpb_ref, state_ref, kv_ref, *, tm, layer):
    i = pl.program_id(0)
    wob_ref[...] = wo_ref[...].astype(BF16)
    wgb_ref[...] = wg_ref[...].astype(BF16)
    wpb_ref[...] = wp_ref[...].astype(BF16)
    n_chunks = tm // RET_CHUNK
    kv_cols = 4 * KV_WIDTH

    @pl.when(i == 0)
    def _():
        state_ref[...] = jnp.zeros_like(state_ref)
        kv_ref[0:WINDOW, :] = jnp.zeros((WINDOW, kv_cols), BF16)

    @pl.when(i > 0)
    def _():
        kv_ref[0:WINDOW, :] = kv_ref[tm:tm + WINDOW, :]

    kv_ref[WINDOW:tm + WINDOW, :] = a_ref[:, ATT_WIDTH:ATT_WIDTH + kv_cols]

    contract_last = (((1,), (1,)), ((), ()))
    contract_first = (((0,), (0,)), ((), ()))
    qi = lax.broadcasted_iota(jnp.int32, (WINDOW, 2 * WINDOW), 0)
    kj = lax.broadcasted_iota(jnp.int32, (WINDOW, 2 * WINDOW), 1)
    band = (kj > qi) & (kj <= qi + WINDOW)
    low_lanes = lax.broadcasted_iota(jnp.int32, (2 * WINDOW, LANES), 1) < HALF

    def chunk(c, carry):
        r0 = pl.multiple_of(c * RET_CHUNK, RET_CHUNK)
        rows = pl.ds(r0, RET_CHUNK)

        for h in range(RET_HEADS):
            lo = h * RET_HEAD_DIM
            hi = lo + RET_HEAD_DIM
            q = r_ref[rows, lo:hi]
            k = r_ref[rows, RET_WIDTH + lo:RET_WIDTH + hi]
            v = r_ref[rows, 2 * RET_WIDTH + lo:2 * RET_WIDTH + hi]
            silu_gate = r_ref[rows, 3 * RET_WIDTH + lo:3 * RET_WIDTH + hi].astype(F32)
            st = state_ref[h]
            sc = lax.dot_general(q, k, contract_last, preferred_element_type=F32) * dec_ref[h]
            inner = jnp.dot(sc.astype(BF16), v, preferred_element_type=F32)
            cross = jnp.dot(q, st.astype(BF16), preferred_element_type=F32) * qd_ref[h]
            state_ref[h] = st * cd_ref[h] + lax.dot_general(kdec_ref[rows, lo:hi], v, contract_first,
                                                            preferred_element_type=F32)
            y = inner + cross
            mu = jnp.mean(y, axis=-1, keepdims=True)
            d = y - mu
            var = jnp.mean(d * d, axis=-1, keepdims=True)
            yn = d * lax.rsqrt(var + GN_EPS) * g_ref[:, lo:hi] + b_ref[:, lo:hi]
            o_ref[rows, lo:hi] = (silu_gate * yn).astype(BF16)

        kk = kv_ref[pl.ds(r0, 2 * WINDOW), :]
        k_nat = kk[:, 0:LANES]
        v_nat = kk[:, LANES:2 * LANES]
        k_swp = kk[:, 2 * LANES:3 * LANES]
        v_swp = kk[:, 3 * LANES:4 * LANES]
        zero = jnp.zeros_like(k_nat)
        k_ext = ((jnp.where(low_lanes, k_nat, zero), jnp.where(low_lanes, zero, k_swp)),
                 (jnp.where(low_lanes, k_swp, zero), jnp.where(low_lanes, zero, k_nat)))
        v_ext = ((jnp.where(low_lanes, v_nat, zero), jnp.where(low_lanes, zero, v_swp)),
                 (jnp.where(low_lanes, v_swp, zero), jnp.where(low_lanes, zero, v_nat)))
        first_block = (i * n_chunks + c) == 0
        mask = band & ((kj >= WINDOW) | jnp.logical_not(first_block))
        pairs_per_kv = ATT_GROUP // 2
        for kvh in range(ATT_KV_HEADS):
            for pr in range(pairs_per_kv):
                pair = kvh * pairs_per_kv + pr
                q_pair = a_ref[rows, pair * LANES:(pair + 1) * LANES]
                o_pair = None
                for par in range(2):
                    sink = sink_ref[layer, 2 * pair + par] * LOG2_E
                    sc = lax.dot_general(q_pair, k_ext[kvh][par], contract_last, preferred_element_type=F32)
                    sc = jnp.where(mask, sc, -jnp.inf)
                    m = jnp.maximum(jnp.max(sc, axis=-1, keepdims=True), sink)
                    p = jnp.exp2(sc - m)
                    den = jnp.sum(p, axis=-1, keepdims=True) + jnp.exp2(sink - m)
                    o = jnp.dot(p.astype(BF16), v_ext[kvh][par], preferred_element_type=F32) * (1.0 / den)
                    o_pair = o if o_pair is None else o_pair + o
                o_ref[rows, RET_WIDTH + pair * LANES:RET_WIDTH + (pair + 1) * LANES] = o_pair.astype(BF16)
        return carry

    lax.fori_loop(0, n_chunks, chunk, 0, unroll=2)


def _mixer(rp, kdec, ap, ret_consts, gn_g, gn_b, sinks, w_out, w_gate, w_proj, layer, tm):
    s = rp.shape[0]
    decay, q_decay, chunk_decay = ret_consts
    smem = pl.BlockSpec(memory_space=pltpu.SMEM)
    const3 = lambda shape: pl.BlockSpec(shape, lambda i: (0, 0, 0))
    steps = s // tm
    slab_d, slab_p = _slab_rows(D_MODEL, steps), _slab_rows(PLE_DIM, steps)
    w_in_spec = lambda rows: pl.BlockSpec((None, rows, D_MODEL), lambda i: (layer, i, 0))
    w_out_spec = lambda rows: pl.BlockSpec((rows, D_MODEL), lambda i: (i, 0))
    w_shape = lambda rows: jax.ShapeDtypeStruct((rows, D_MODEL), BF16)
    vmem = 2 * (tm * (RET_COLS + RET_WIDTH) * 2 + tm * ATT_OUT_COLS * 2 + tm * D_MODEL * 2) \
        + 2 * (2 * slab_d + slab_p) * D_MODEL * 6 \
        + 2 * 4 * (decay.size + q_decay.size) \
        + RET_HEADS * RET_HEAD_DIM * RET_HEAD_DIM * 4 + (tm + WINDOW) * 4 * KV_WIDTH * 2 + (8 << 20)
    return pl.pallas_call(
        functools.partial(_mixer_kernel, tm=tm, layer=layer),
        grid=(s // tm,),
        in_specs=[smem, smem,
                  pl.BlockSpec((tm, RET_COLS), lambda i: (i, 0)),
                  pl.BlockSpec((tm, RET_WIDTH), lambda i: (i, 0)),
                  pl.BlockSpec((tm, ATT_OUT_COLS), lambda i: (i, 0)),
                  const3(decay.shape), const3(q_decay.shape),
                  _layer_vec(layer, RET_WIDTH), _layer_vec(layer, RET_WIDTH),
                  w_in_spec(slab_d), w_in_spec(slab_d), w_in_spec(slab_p)],
        out_specs=[pl.BlockSpec((tm, D_MODEL), lambda i: (i, 0)),
                   w_out_spec(slab_d), w_out_spec(slab_d), w_out_spec(slab_p)],
        out_shape=[jax.ShapeDtypeStruct((s, D_MODEL), BF16), w_shape(D_MODEL), w_shape(D_MODEL), w_shape(PLE_DIM)],
        scratch_shapes=[pltpu.VMEM((RET_HEADS, RET_HEAD_DIM, RET_HEAD_DIM), F32),
                        pltpu.VMEM((tm + WINDOW, 4 * KV_WIDTH), BF16)],
        compiler_params=_params(("arbitrary",), vmem),
        name="mixer",
    )(chunk_decay, sinks, rp, kdec, ap, decay, q_decay, gn_g, gn_b, w_out, w_gate, w_proj)


def _retention_constants():
    c = RET_CHUNK
    log_g = jnp.log1p(-jnp.exp2(-5.0 - jnp.arange(RET_HEADS, dtype=F32)))
    idx = jnp.arange(c, dtype=F32)
    diff = idx[:, None] - idx[None, :]
    decay = jnp.where(diff[None] >= 0, jnp.exp(log_g[:, None, None] * jnp.maximum(diff, 0.0)[None]), 0.0)
    q_decay = jnp.exp(log_g[:, None] * (idx[None, :] + 1.0))
    k_decay = jnp.exp(log_g[:, None] * (c - 1.0 - idx[None, :]))
    chunk_decay = jnp.exp(log_g * c)
    wide = lambda t, lanes: jnp.broadcast_to(t[:, :, None], (RET_HEADS, c, lanes))
    return (decay, wide(q_decay, RET_HEAD_DIM), chunk_decay), wide(k_decay, LANES)


def _mix_out_kernel(cat_ref, x_ref, p_ref, wo_ref, wg_ref, wp_ref, g_ref, b_ref, bg_ref, wd_ref,
                    r_ref, xb_ref, wdb_ref, *, tm, sub):
    wdb_ref[...] = wd_ref[...].astype(BF16)
    for r in range(0, tm, sub):
        rows = slice(r, r + sub)
        z = DEEPNORM_ALPHA * x_ref[rows, :] + jnp.dot(cat_ref[rows, :], wo_ref[...], preferred_element_type=F32)
        x1 = _layer_norm(z, g_ref[...], b_ref[...])
        x1b = x1.astype(BF16)
        gate = jnp.dot(x1b, wg_ref[...], preferred_element_type=F32) + bg_ref[...]
        ple = jnp.dot(p_ref[rows, :].astype(BF16), wp_ref[...], preferred_element_type=F32) * _sigmoid(gate)
        r_ref[rows, :] = DEEPNORM_ALPHA * x1 + ple
        xb_ref[rows, :] = x1b


def _mix_out(cat, x, p, layer, w_out_b, w_gate_b, w_proj_b, ln_g, ln_b, b_gate, w_down, tm, sub):
    s = x.shape[0]
    row = lambda cols: pl.BlockSpec((tm, cols), lambda i: (i, 0))
    resident = lambda rows_: pl.BlockSpec((rows_, D_MODEL), lambda i: (0, 0), pipeline_mode=pl.Buffered(1))
    slab = _slab_rows(D_FF, s // tm)
    vmem = (2 * D_MODEL * D_MODEL + PLE_DIM * D_MODEL) * 2 + 2 * slab * D_MODEL * 6 \
        + 2 * tm * (D_MODEL * 2 + D_MODEL * 4 + PLE_DIM * 4 + D_MODEL * 4 + D_MODEL * 2) + (12 << 20)
    return pl.pallas_call(
        functools.partial(_mix_out_kernel, tm=tm, sub=sub),
        grid=(s // tm,),
        in_specs=[row(D_MODEL), row(D_MODEL),
                  pl.BlockSpec((None, tm, PLE_DIM), lambda i: (layer, i, 0)),
                  resident(D_MODEL), resident(D_MODEL), resident(PLE_DIM),
                  _layer_vec(layer, D_MODEL), _layer_vec(layer, D_MODEL), _layer_vec(layer, D_MODEL),
                  pl.BlockSpec((None, slab, D_MODEL), lambda i: (layer, i, 0))],
        out_specs=[row(D_MODEL), row(D_MODEL), pl.BlockSpec((slab, D_MODEL), lambda i: (i, 0))],
        out_shape=[jax.ShapeDtypeStruct((s, D_MODEL), F32), jax.ShapeDtypeStruct((s, D_MODEL), BF16),
                   jax.ShapeDtypeStruct((D_FF, D_MODEL), BF16)],
        compiler_params=_params(("arbitrary",), vmem),
        name="mix_out",
    )(cat, x, p, w_out_b, w_gate_b, w_proj_b, ln_g, ln_b, b_gate, w_down)


FFN_TN = 512
FFN_BLOCKS = D_FF // FFN_TN


def _ffn_up_kernel(x_ref, wg_ref, wu_ref, cwg_ref, cwu_ref, cbg_ref, cbu_ref, o_ref, carry_ref, wgb_ref, wub_ref,
                   hg_ref, *, tm, nb):
    i = pl.program_id(0)
    jj = pl.program_id(1)
    gj = jnp.minimum(jj, nb - 1)
    uj = jnp.maximum(jj - 1, 0)

    @pl.when(i == 0)
    def _():
        carry_ref[gj, 0] = jnp.zeros((SUBLANES, FFN_TN), F32)
        carry_ref[uj, 1] = jnp.zeros((SUBLANES, FFN_TN), F32)

    def conv(prev, h, cw_ref, cb_ref):
        ext = jnp.concatenate([prev, h], axis=0)
        y = (ext * cw_ref[2:3, :] + pltpu.roll(ext, 1, 0) * cw_ref[1:2, :]
             + pltpu.roll(ext, 2, 0) * cw_ref[0:1, :] + cb_ref[...])
        return y[SUBLANES:]

    def up_part():
        wub_ref[...] = wu_ref[...].astype(BF16)
        hg = hg_ref[...]
        yg = conv(carry_ref[uj, 0], hg, cwg_ref, cbg_ref)
        hu = jnp.dot(x_ref[...], wub_ref[...], preferred_element_type=F32)
        yu = conv(carry_ref[uj, 1], hu, cwu_ref, cbu_ref)
        o_ref[...] = (yg * _sigmoid(yg) * yu).astype(BF16)
        carry_ref[uj, 0] = hg[tm - SUBLANES:tm]
        carry_ref[uj, 1] = hu[tm - SUBLANES:tm]

    def gate_part():
        wgb_ref[...] = wg_ref[...].astype(BF16)
        hg_ref[...] = jnp.dot(x_ref[...], wgb_ref[...], preferred_element_type=F32)

    @pl.when(jj == 0)
    def _():
        gate_part()

    @pl.when((jj > 0) & (jj < nb))
    def _():
        up_part()
        gate_part()

    @pl.when(jj == nb)
    def _():
        up_part()


def _ffn_up(xb, w_up, conv_w, conv_b, layer, tm, sub):
    assert sub == tm
    s = xb.shape[0]
    nb = FFN_BLOCKS
    vmem = 2 * (tm * D_MODEL * 2 + 2 * D_MODEL * FFN_TN * 4 + tm * FFN_TN * 2) + 2 * D_MODEL * FFN_TN * 2 \
        + tm * FFN_TN * 4 + (12 << 20)
    gcol = lambda jj: jnp.minimum(jj, nb - 1)
    ucol = lambda jj: jnp.maximum(jj - 1, 0)
    spec = lambda rows, col: pl.BlockSpec((None, rows, FFN_TN), lambda i, jj: (layer, 0, col(jj)))
    return pl.pallas_call(
        functools.partial(_ffn_up_kernel, tm=tm, nb=nb),
        grid=(s // tm, nb + 1),
        in_specs=[pl.BlockSpec((tm, D_MODEL), lambda i, jj: (i, 0)),
                  spec(D_MODEL, gcol), spec(D_MODEL, lambda jj: ucol(jj) + nb),
                  spec(CONV_WIDTH, ucol), spec(CONV_WIDTH, lambda jj: ucol(jj) + nb),
                  spec(1, ucol), spec(1, lambda jj: ucol(jj) + nb)],
        out_specs=pl.BlockSpec((tm, FFN_TN), lambda i, jj: (i, ucol(jj))),
        out_shape=jax.ShapeDtypeStruct((s, D_FF), BF16),
        scratch_shapes=[pltpu.VMEM((nb, 2, SUBLANES, FFN_TN), F32),
                        pltpu.VMEM((D_MODEL, FFN_TN), BF16), pltpu.VMEM((D_MODEL, FFN_TN), BF16),
                        pltpu.VMEM((tm, FFN_TN), F32)],
        compiler_params=_params(("arbitrary", "arbitrary"), vmem),
        name="ffn_up",
    )(xb, w_up, w_up, conv_w, conv_w, conv_b, conv_b)


def _ffn_down_kernel(a_ref, r_ref, w_ref, g_ref, b_ref, x_ref, xb_ref, *, tm, sub):
    for r in range(0, tm, sub):
        rows = slice(r, r + sub)
        z = r_ref[rows, :] + jnp.dot(a_ref[rows, :], w_ref[...], preferred_element_type=F32)
        x2 = _layer_norm(z, g_ref[...], b_ref[...])
        x_ref[rows, :] = x2
        xb_ref[rows, :] = x2.astype(BF16)


def _ffn_down(act, r, w_down_b, ln_g, ln_b, layer, tm, sub):
    s = r.shape[0]
    row = lambda cols: pl.BlockSpec((tm, cols), lambda i: (i, 0))
    vmem = D_FF * D_MODEL * 2 + 2 * tm * (D_FF * 2 + D_MODEL * 4 + D_MODEL * 4 + D_MODEL * 2) + (8 << 20)
    return pl.pallas_call(
        functools.partial(_ffn_down_kernel, tm=tm, sub=sub),
        grid=(s // tm,),
        in_specs=[row(D_FF), row(D_MODEL),
                  pl.BlockSpec((D_FF, D_MODEL), lambda i: (0, 0), pipeline_mode=pl.Buffered(1)),
                  _layer_vec(layer, D_MODEL), _layer_vec(layer, D_MODEL)],
        out_specs=[row(D_MODEL), row(D_MODEL)],
        out_shape=[jax.ShapeDtypeStruct((s, D_MODEL), F32), jax.ShapeDtypeStruct((s, D_MODEL), BF16)],
        compiler_params=_params(("arbitrary",), vmem),
        name="ffn_down",
    )(act, r, w_down_b, ln_g, ln_b)


def _tiles(s):
    pick = lambda tm, sub: (min(tm, s), min(sub, s))
    return dict(tables=pick(1024, 1024), ret_proj=pick(2048, 256), att_proj=pick(1024, 256),
                mixer=pick(1024, 128), mix_out=pick(512, 256), ffn_up=pick(1024, 1024), ffn_down=pick(512, 256))


def kernel(x, p, positions, w_in, w_out, ret_norm_g, ret_norm_b, attn_sinks, ln1_g, ln1_b, w_ffn_up,
           ffn_conv_w, ffn_conv_b, w_ffn_down, ln2_g, ln2_b, w_ple_gate, b_ple_gate, w_ple_proj):
    batch, s, d = x.shape
    assert batch == 1 and d == D_MODEL and s % RET_CHUNK == 0
    t = _tiles(s)
    assert all(s % tm == 0 and tm % sub == 0 for tm, sub in t.values())

    xf = x.reshape(s, d)
    pf = p.reshape(DEPTH, s, PLE_DIM)
    cos_r, sin_r, cos_a, sin_a, sin_b, xb = _rot_tables(positions.reshape(s), xf, t["tables"][0])
    ret_consts, k_decay = _retention_constants()
    vecs = lambda v: v.reshape(DEPTH, 1, -1)
    gn_g, gn_b, g1, b1, g2, b2 = (vecs(v) for v in (ret_norm_g, ret_norm_b, ln1_g, ln1_b, ln2_g, ln2_b))
    bg, conv_b = vecs(b_ple_gate), vecs(ffn_conv_b)

    for l in range(DEPTH):
        rp, kdec, w_att_b = _ret_proj(xb, w_in, l, cos_r, sin_r, k_decay, *t["ret_proj"])
        ap = _att_proj(xb, w_att_b, cos_a, sin_a, sin_b, *t["att_proj"])
        cat, w_out_b, w_gate_b, w_proj_b = _mixer(rp, kdec, ap, ret_consts, gn_g, gn_b, attn_sinks,
                                                  w_out, w_ple_gate, w_ple_proj, l, t["mixer"][0])
        r, xb, w_down_b = _mix_out(cat, xf, pf, l, w_out_b, w_gate_b, w_proj_b, g1, b1, bg, w_ffn_down,
                                   *t["mix_out"])
        act = _ffn_up(xb, w_ffn_up, ffn_conv_w, conv_b, l, *t["ffn_up"])
        xf, xb = _ffn_down(act, r, w_down_b, g2, b2, l, *t["ffn_down"])
    return xf.reshape(batch, s, d)
```

```python
import functools

import jax
import jax.numpy as jnp
from jax import lax
from jax.experimental import pallas as pl
from jax.experimental.pallas import tpu as pltpu

F32 = jnp.float32
BF16 = jnp.bfloat16

D_MODEL = 2048
DEPTH = 4
PLE_DIM = 256
RET_WIDTH = 1024
RET_HEADS = 4
RET_HEAD_DIM = 256
RET_CHUNK = 128
RET_ROT_BASE = 10000.0
ATT_HEAD_DIM = 64
ATT_WIDTH = 1024
ATT_HEADS = 16
ATT_KV_HEADS = 2
ATT_GROUP = ATT_HEADS // ATT_KV_HEADS
KV_WIDTH = 128
WINDOW = 128
ROPE_THETA = 10000.0
D_FF = 5632
CONV_WIDTH = 3
LN_EPS = 1e-5
GN_EPS = 1e-6
DEEPNORM_ALPHA = (2 * DEPTH) ** 0.25

LANES = 128
SUBLANES = 8
BF16_SUBLANE_ROWS = 16
MXU_DIM = 256
VMEM_LIMIT_CAP = 58 * 1024 * 1024

RET_COLS = 4 * RET_WIDTH
ATT_IN_COLS = ATT_WIDTH + 2 * KV_WIDTH
ATT_OUT_COLS = ATT_WIDTH + 4 * KV_WIDTH
HALF = LANES // 2
LOG2_E = 1.4426950408889634
ATT_SCORE_SCALE = ATT_HEAD_DIM ** -0.5 * LOG2_E


def _params(semantics, vmem_bytes):
    return pltpu.CompilerParams(dimension_semantics=semantics,
                                vmem_limit_bytes=int(min(vmem_bytes, VMEM_LIMIT_CAP)))


def _layer_norm(z, g, b):
    mu = jnp.mean(z, axis=-1, keepdims=True)
    d = z - mu
    var = jnp.mean(d * d, axis=-1, keepdims=True)
    return d * lax.rsqrt(var + LN_EPS) * g + b


def _sigmoid(v):
    return 1.0 / (1.0 + jnp.exp2(v * -LOG2_E))


def _slab_rows(rows, steps):
    assert rows % steps == 0 and (rows // steps) % BF16_SUBLANE_ROWS == 0, (rows, steps)
    return rows // steps


def _layer_vec(layer, cols):
    return pl.BlockSpec((None, 1, cols), lambda *_: (layer, 0, 0))


def _rot_tables_kernel(pos_ref, fr_ref, fa_ref, x_ref, cr_ref, sr_ref, ca_ref, sa_ref, sb_ref, xb_ref):
    xb_ref[...] = x_ref[...].astype(BF16)
    pos = pos_ref[...].astype(F32)
    ang_r = pos * fr_ref[...]
    cr_ref[...] = jnp.cos(ang_r)
    sr_ref[...] = jnp.sin(ang_r)
    ang_a = pos * fa_ref[...]
    c = jnp.cos(ang_a)
    s = jnp.sin(ang_a)
    lane = lax.broadcasted_iota(jnp.int32, c.shape, 1)
    first_half = (lane & (ATT_HEAD_DIM // 2)) == 0
    ca_ref[...] = c
    sa_ref[...] = jnp.where(first_half, -s, 0.0)
    sb_ref[...] = jnp.where(first_half, 0.0, s)


def _rot_tables(positions, xf, tm):
    s = positions.shape[0]
    ret_inv_freq = 1.0 / (RET_ROT_BASE ** jnp.linspace(0.0, 1.0, RET_HEAD_DIM // 2, dtype=F32))
    att_inv_freq = ROPE_THETA ** (-jnp.arange(0, ATT_HEAD_DIM, 2, dtype=F32) / ATT_HEAD_DIM)
    fr = ret_inv_freq.reshape(1, LANES)
    fa = jnp.tile(att_inv_freq, LANES // (ATT_HEAD_DIM // 2)).reshape(1, LANES)
    tab = jax.ShapeDtypeStruct((s, LANES), F32)
    row = pl.BlockSpec((tm, LANES), lambda i: (i, 0))
    const = pl.BlockSpec((1, LANES), lambda i: (0, 0))
    wide = pl.BlockSpec((tm, D_MODEL), lambda i: (i, 0))
    return pl.pallas_call(
        _rot_tables_kernel,
        grid=(s // tm,),
        in_specs=[pl.BlockSpec((tm, 1), lambda i: (i, 0)), const, const, wide],
        out_specs=[row] * 5 + [wide],
        out_shape=[tab] * 5 + [jax.ShapeDtypeStruct((s, D_MODEL), BF16)],
        compiler_params=_params(("arbitrary",), 2 * tm * D_MODEL * 6 + (16 << 20)),
        name="rot_tables",
    )(positions.reshape(s, 1), fr, fa, xf)


RET_TN = 512
RET_BLOCKS_PER_PART = RET_WIDTH // RET_TN
RET_HEADS_PER_BLOCK = RET_TN // RET_HEAD_DIM


def _ret_proj_kernel(x_ref, w_ref, c_ref, s_ref, kd_ref, wrow_ref, o_ref, kdec_ref, watt_ref, wb_ref, *, tm, sub):
    j = pl.program_id(1)
    part = j // RET_BLOCKS_PER_PART
    watt_ref[...] = wrow_ref[:, RET_COLS:].astype(BF16)

    def sub_blocks():
        wb_ref[...] = w_ref[...].astype(BF16)
        for r in range(0, tm, sub):
            rows = slice(r, r + sub)
            yield rows, jnp.dot(x_ref[rows, :], wb_ref[...], preferred_element_type=F32)

    def rotary(rows, acc, scale):
        c = c_ref[rows, :] * scale
        s = s_ref[rows, :] * scale
        for h in range(RET_HEADS_PER_BLOCK):
            lo = h * RET_HEAD_DIM
            mid = lo + RET_HEAD_DIM // 2
            a1 = acc[:, lo:mid]
            a2 = acc[:, mid:lo + RET_HEAD_DIM]
            yield h, lo, a1 * c - a2 * s
            yield h, mid, a1 * s + a2 * c

    @pl.when(part == 0)
    def _():
        for rows, acc in sub_blocks():
            for _, col, val in rotary(rows, acc, 1.0):
                o_ref[rows, col:col + LANES] = val.astype(BF16)

    @pl.when(part == 1)
    def _():
        for rows, acc in sub_blocks():
            for h, col, val in rotary(rows, acc, RET_HEAD_DIM ** -0.5):
                o_ref[rows, col:col + LANES] = val.astype(BF16)
                chunks = val.reshape(sub // RET_CHUNK, RET_CHUNK, LANES) * kd_ref[h][None]
                kdec_ref[rows, col:col + LANES] = chunks.reshape(sub, LANES).astype(BF16)

    @pl.when(part == 2)
    def _():
        for rows, acc in sub_blocks():
            o_ref[rows, :] = acc.astype(BF16)

    @pl.when(part == 3)
    def _():
        for rows, acc in sub_blocks():
            o_ref[rows, :] = (acc * _sigmoid(acc)).astype(BF16)


def _ret_proj(xb, w_in, layer, cos_r, sin_r, k_decay, tm, sub):
    s = xb.shape[0]
    vmem = 2 * (tm * D_MODEL * 2 + D_MODEL * RET_TN * 4 + 2 * tm * RET_TN * 2 + 2 * tm * LANES * 4) \
        + D_MODEL * RET_TN * 2 + (12 << 20)
    k_block = lambda j: jnp.clip(j - RET_BLOCKS_PER_PART, 0, RET_BLOCKS_PER_PART - 1)
    nj = RET_COLS // RET_TN
    slab = _slab_rows(D_MODEL, (s // tm) * nj)
    return pl.pallas_call(
        functools.partial(_ret_proj_kernel, tm=tm, sub=sub),
        grid=(s // tm, nj),
        in_specs=[pl.BlockSpec((tm, D_MODEL), lambda i, j: (i, 0)),
                  pl.BlockSpec((None, D_MODEL, RET_TN), lambda i, j: (layer, 0, j)),
                  pl.BlockSpec((tm, LANES), lambda i, j: (i, 0)),
                  pl.BlockSpec((tm, LANES), lambda i, j: (i, 0)),
                  pl.BlockSpec((RET_HEADS_PER_BLOCK, RET_CHUNK, LANES), lambda i, j: (k_block(j), 0, 0)),
                  pl.BlockSpec((None, slab, RET_COLS + ATT_IN_COLS), lambda i, j: (layer, i * nj + j, 0))],
        out_specs=[pl.BlockSpec((tm, RET_TN), lambda i, j: (i, j)),
                   pl.BlockSpec((tm, RET_TN), lambda i, j: (i, k_block(j))),
                   pl.BlockSpec((slab, ATT_IN_COLS), lambda i, j: (i * nj + j, 0))],
        out_shape=[jax.ShapeDtypeStruct((s, RET_COLS), BF16), jax.ShapeDtypeStruct((s, RET_WIDTH), BF16),
                   jax.ShapeDtypeStruct((D_MODEL, ATT_IN_COLS), BF16)],
        scratch_shapes=[pltpu.VMEM((D_MODEL, RET_TN), BF16)],
        compiler_params=_params(("arbitrary", "arbitrary"), vmem),
        name="ret_proj",
    )(xb, w_in, cos_r, sin_r, k_decay, w_in)


ATT_W_BLOCKS = ATT_IN_COLS // MXU_DIM


def _att_proj_kernel(x_ref, *refs, tm, sub):
    w_refs = refs[:ATT_W_BLOCKS]
    c_ref, sa_ref, sb_ref, o_ref = refs[ATT_W_BLOCKS:]

    for r in range(0, tm, sub):
        rows = slice(r, r + sub)
        c = c_ref[rows, :]
        sa = sa_ref[rows, :]
        sb = sb_ref[rows, :]

        def rot(a):
            return (a * c + pltpu.roll(a, LANES - ATT_HEAD_DIM // 2, 1) * sa
                    + pltpu.roll(a, ATT_HEAD_DIM // 2, 1) * sb)

        x = x_ref[rows, :]
        for wb in range(ATT_W_BLOCKS):
            acc = jnp.dot(x, w_refs[wb][...], preferred_element_type=F32)
            for half in range(MXU_DIM // LANES):
                col = wb * MXU_DIM + half * LANES
                a = acc[:, half * LANES:(half + 1) * LANES]
                if col < ATT_WIDTH:
                    o_ref[rows, col:col + LANES] = (rot(a) * ATT_SCORE_SCALE).astype(BF16)
                elif col < ATT_WIDTH + KV_WIDTH:
                    k = rot(a)
                    o_ref[rows, col:col + LANES] = k.astype(BF16)
                    o_ref[rows, col + 2 * LANES:col + 3 * LANES] = pltpu.roll(k, HALF, 1).astype(BF16)
                else:
                    o_ref[rows, col:col + LANES] = a.astype(BF16)
                    o_ref[rows, col + 2 * LANES:col + 3 * LANES] = pltpu.roll(a, HALF, 1).astype(BF16)


def _att_proj(xb, w_att_b, cos_a, sin_a, sin_b, tm, sub):
    s = xb.shape[0]
    vmem = 2 * (tm * D_MODEL * 2 + D_MODEL * ATT_IN_COLS * 2 + tm * ATT_OUT_COLS * 2 + 3 * tm * LANES * 4) + (12 << 20)
    row = pl.BlockSpec((tm, LANES), lambda i: (i, 0))
    w_specs = [pl.BlockSpec((D_MODEL, MXU_DIM), functools.partial(lambda i, wb: (0, wb), wb=wb))
               for wb in range(ATT_W_BLOCKS)]
    return pl.pallas_call(
        functools.partial(_att_proj_kernel, tm=tm, sub=sub),
        grid=(s // tm,),
        in_specs=[pl.BlockSpec((tm, D_MODEL), lambda i: (i, 0))] + w_specs + [row, row, row],
        out_specs=pl.BlockSpec((tm, ATT_OUT_COLS), lambda i: (i, 0)),
        out_shape=jax.ShapeDtypeStruct((s, ATT_OUT_COLS), BF16),
        compiler_params=_params(("arbitrary",), vmem),
        name="att_proj",
    )(xb, *([w_att_b] * ATT_W_BLOCKS), cos_a, sin_a, sin_b)


def _mixer_kernel(cd_ref, sink_ref, r_ref, kdec_ref, a_ref, dec_ref, qd_ref, g_ref, b_ref, wo_ref, wg_ref, wp_ref,
                  o_ref, wob_ref, wgb_ref, wpb_ref, state_ref, kv_ref, *, tm, layer):
    i = pl.program_id(0)
    wob_ref[...] = wo_ref[...].astype(BF16)
    wgb_ref[...] = wg_ref[...].astype(BF16)
    wpb_ref[...] = wp_ref[...].astype(BF16)
    n_chunks = tm // RET_CHUNK
    kv_cols = 4 * KV_WIDTH

    @pl.when(i == 0)
    def _():
        state_ref[...] = jnp.zeros_like(state_ref)
        kv_ref[0:WINDOW, :] = jnp.zeros((WINDOW, kv_cols), BF16)

    @pl.when(i > 0)
    def _():
        kv_ref[0:WINDOW, :] = kv_ref[tm:tm + WINDOW, :]

    kv_ref[WINDOW:tm + WINDOW, :] = a_ref[:, ATT_WIDTH:ATT_WIDTH + kv_cols]

    contract_last = (((1,), (1,)), ((), ()))
    contract_first = (((0,), (0,)), ((), ()))
    qi = lax.broadcasted_iota(jnp.int32, (WINDOW, 2 * WINDOW), 0)
    kj = lax.broadcasted_iota(jnp.int32, (WINDOW, 2 * WINDOW), 1)
    band = (kj > qi) & (kj <= qi + WINDOW)
    low_lanes = lax.broadcasted_iota(jnp.int32, (2 * WINDOW, LANES), 1) < HALF

    def chunk(c, carry):
        r0 = pl.multiple_of(c * RET_CHUNK, RET_CHUNK)
        rows = pl.ds(r0, RET_CHUNK)

        for h in range(RET_HEADS):
            lo = h * RET_HEAD_DIM
            hi = lo + RET_HEAD_DIM
            q = r_ref[rows, lo:hi]
            k = r_ref[rows, RET_WIDTH + lo:RET_WIDTH + hi]
            v = r_ref[rows, 2 * RET_WIDTH + lo:2 * RET_WIDTH + hi]
            silu_gate = r_ref[rows, 3 * RET_WIDTH + lo:3 * RET_WIDTH + hi].astype(F32)
            st = state_ref[h]
            sc = lax.dot_general(q, k, contract_last, preferred_element_type=F32) * dec_ref[h]
            inner = jnp.dot(sc.astype(BF16), v, preferred_element_type=F32)
            cross = jnp.dot(q, st.astype(BF16), preferred_element_type=F32) * qd_ref[h]
            state_ref[h] = st * cd_ref[h] + lax.dot_general(kdec_ref[rows, lo:hi], v, contract_first,
                                                            preferred_element_type=F32)
            y = inner + cross
            mu = jnp.mean(y, axis=-1, keepdims=True)
            d = y - mu
            var = jnp.mean(d * d, axis=-1, keepdims=True)
            yn = d * lax.rsqrt(var + GN_EPS) * g_ref[:, lo:hi] + b_ref[:, lo:hi]
            o_ref[rows, lo:hi] = (silu_gate * yn).astype(BF16)

        kk = kv_ref[pl.ds(r0, 2 * WINDOW), :]
        k_nat = kk[:, 0:LANES]
        v_nat = kk[:, LANES:2 * LANES]
        k_swp = kk[:, 2 * LANES:3 * LANES]
        v_swp = kk[:, 3 * LANES:4 * LANES]
        zero = jnp.zeros_like(k_nat)
        k_ext = ((jnp.where(low_lanes, k_nat, zero), jnp.where(low_lanes, zero, k_swp)),
                 (jnp.where(low_lanes, k_swp, zero), jnp.where(low_lanes, zero, k_nat)))
        v_ext = ((jnp.where(low_lanes, v_nat, zero), jnp.where(low_lanes, zero, v_swp)),
                 (jnp.where(low_lanes, v_swp, zero), jnp.where(low_lanes, zero, v_nat)))
        first_block = (i * n_chunks + c) == 0
        mask = band & ((kj >= WINDOW) | jnp.logical_not(first_block))
        pairs_per_kv = ATT_GROUP // 2
        for kvh in range(ATT_KV_HEADS):
            for pr in range(pairs_per_kv):
                pair = kvh * pairs_per_kv + pr
                q_pair = a_ref[rows, pair * LANES:(pair + 1) * LANES]
                o_pair = None
                for par in range(2):
                    sink = sink_ref[layer, 2 * pair + par] * LOG2_E
                    sc = lax.dot_general(q_pair, k_ext[kvh][par], contract_last, preferred_element_type=F32)
                    sc = jnp.where(mask, sc, -jnp.inf)
                    m = jnp.maximum(jnp.max(sc, axis=-1, keepdims=True), sink)
                    p = jnp.exp2(sc - m)
                    den = jnp.sum(p, axis=-1, keepdims=True) + jnp.exp2(sink - m)
                    o = jnp.dot(p.astype(BF16), v_ext[kvh][par], preferred_element_type=F32) * (1.0 / den)
                    o_pair = o if o_pair is None else o_pair + o
                o_ref[rows, RET_WIDTH + pair * LANES:RET_WIDTH + (pair + 1) * LANES] = o_pair.astype(BF16)
        return carry

    lax.fori_loop(0, n_chunks, chunk, 0, unroll=2)


def _mixer(rp, kdec, ap, ret_consts, gn_g, gn_b, sinks, w_out, w_gate, w_proj, layer, tm):
    s = rp.shape[0]
    decay, q_decay, chunk_decay = ret_consts
    smem = pl.BlockSpec(memory_space=pltpu.SMEM)
    const3 = lambda shape: pl.BlockSpec(shape, lambda i: (0, 0, 0))
    steps = s // tm
    slab_d, slab_p = _slab_rows(D_MODEL, steps), _slab_rows(PLE_DIM, steps)
    w_in_spec = lambda rows: pl.BlockSpec((None, rows, D_MODEL), lambda i: (layer, i, 0))
    w_out_spec = lambda rows: pl.BlockSpec((rows, D_MODEL), lambda i: (i, 0))
    w_shape = lambda rows: jax.ShapeDtypeStruct((rows, D_MODEL), BF16)
    vmem = 2 * (tm * (RET_COLS + RET_WIDTH) * 2 + tm * ATT_OUT_COLS * 2 + tm * D_MODEL * 2) \
        + 2 * (2 * slab_d + slab_p) * D_MODEL * 6 \
        + 2 * 4 * (decay.size + q_decay.size) \
        + RET_HEADS * RET_HEAD_DIM * RET_HEAD_DIM * 4 + (tm + WINDOW) * 4 * KV_WIDTH * 2 + (8 << 20)
    return pl.pallas_call(
        functools.partial(_mixer_kernel, tm=tm, layer=layer),
        grid=(s // tm,),
        in_specs=[smem, smem,
                  pl.BlockSpec((tm, RET_COLS), lambda i: (i, 0)),
                  pl.BlockSpec((tm, RET_WIDTH), lambda i: (i, 0)),
                  pl.BlockSpec((tm, ATT_OUT_COLS), lambda i: (i, 0)),
                  const3(decay.shape), const3(q_decay.shape),
                  _layer_vec(layer, RET_WIDTH), _layer_vec(layer, RET_WIDTH),
                  w_in_spec(slab_d), w_in_spec(slab_d), w_in_spec(slab_p)],
        out_specs=[pl.BlockSpec((tm, D_MODEL), lambda i: (i, 0)),
                   w_out_spec(slab_d), w_out_spec(slab_d), w_out_spec(slab_p)],
        out_shape=[jax.ShapeDtypeStruct((s, D_MODEL), BF16), w_shape(D_MODEL), w_shape(D_MODEL), w_shape(PLE_DIM)],
        scratch_shapes=[pltpu.VMEM((RET_HEADS, RET_HEAD_DIM, RET_HEAD_DIM), F32),
                        pltpu.VMEM((tm + WINDOW, 4 * KV_WIDTH), BF16)],
        compiler_params=_params(("arbitrary",), vmem),
        name="mixer",
    )(chunk_decay, sinks, rp, kdec, ap, decay, q_decay, gn_g, gn_b, w_out, w_gate, w_proj)


def _retention_constants():
    c = RET_CHUNK
    log_g = jnp.log1p(-jnp.exp2(-5.0 - jnp.arange(RET_HEADS, dtype=F32)))
    idx = jnp.arange(c, dtype=F32)
    diff = idx[:, None] - idx[None, :]
    decay = jnp.where(diff[None] >= 0, jnp.exp(log_g[:, None, None] * jnp.maximum(diff, 0.0)[None]), 0.0)
    q_decay = jnp.exp(log_g[:, None] * (idx[None, :] + 1.0))
    k_decay = jnp.exp(log_g[:, None] * (c - 1.0 - idx[None, :]))
    chunk_decay = jnp.exp(log_g * c)
    wide = lambda t, lanes: jnp.broadcast_to(t[:, :, None], (RET_HEADS, c, lanes))
    return (decay, wide(q_decay, RET_HEAD_DIM), chunk_decay), wide(k_decay, LANES)


def _mix_out_kernel(cat_ref, x_ref, p_ref, wo_ref, wg_ref, wp_ref, g_ref, b_ref, bg_ref, wd_ref,
                    r_ref, xb_ref, wdb_ref, *, tm, sub):
    wdb_ref[...] = wd_ref[...].astype(BF16)
    for r in range(0, tm, sub):
        rows = slice(r, r + sub)
        z = DEEPNORM_ALPHA * x_ref[rows, :] + jnp.dot(cat_ref[rows, :], wo_ref[...], preferred_element_type=F32)
        x1 = _layer_norm(z, g_ref[...], b_ref[...])
        x1b = x1.astype(BF16)
        gate = jnp.dot(x1b, wg_ref[...], preferred_element_type=F32) + bg_ref[...]
        ple = jnp.dot(p_ref[rows, :].astype(BF16), wp_ref[...], preferred_element_type=F32) * _sigmoid(gate)
        r_ref[rows, :] = DEEPNORM_ALPHA * x1 + ple
        xb_ref[rows, :] = x1b


def _mix_out(cat, x, p, layer, w_out_b, w_gate_b, w_proj_b, ln_g, ln_b, b_gate, w_down, tm, sub):
    s = x.shape[0]
    row = lambda cols: pl.BlockSpec((tm, cols), lambda i: (i, 0))
    resident = lambda rows_: pl.BlockSpec((rows_, D_MODEL), lambda i: (0, 0), pipeline_mode=pl.Buffered(1))
    slab = _slab_rows(D_FF, s // tm)
    vmem = (2 * D_MODEL * D_MODEL + PLE_DIM * D_MODEL) * 2 + 2 * slab * D_MODEL * 6 \
        + 2 * tm * (D_MODEL * 2 + D_MODEL * 4 + PLE_DIM * 4 + D_MODEL * 4 + D_MODEL * 2) + (12 << 20)
    return pl.pallas_call(
        functools.partial(_mix_out_kernel, tm=tm, sub=sub),
        grid=(s // tm,),
        in_specs=[row(D_MODEL), row(D_MODEL),
                  pl.BlockSpec((None, tm, PLE_DIM), lambda i: (layer, i, 0)),
                  resident(D_MODEL), resident(D_MODEL), resident(PLE_DIM),
                  _layer_vec(layer, D_MODEL), _layer_vec(layer, D_MODEL), _layer_vec(layer, D_MODEL),
                  pl.BlockSpec((None, slab, D_MODEL), lambda i: (layer, i, 0))],
        out_specs=[row(D_MODEL), row(D_MODEL), pl.BlockSpec((slab, D_MODEL), lambda i: (i, 0))],
        out_shape=[jax.ShapeDtypeStruct((s, D_MODEL), F32), jax.ShapeDtypeStruct((s, D_MODEL), BF16),
                   jax.ShapeDtypeStruct((D_FF, D_MODEL), BF16)],
        compiler_params=_params(("arbitrary",), vmem),
        name="mix_out",
    )(cat, x, p, w_out_b, w_gate_b, w_proj_b, ln_g, ln_b, b_gate, w_down)


FFN_TN = 512
FFN_BLOCKS = D_FF // FFN_TN


def _ffn_up_kernel(x_ref, wg_ref, wu_ref, cwg_ref, cwu_ref, cbg_ref, cbu_ref, o_ref, carry_ref, wgb_ref, wub_ref,
                   *, tm, sub):
    @pl.when(pl.program_id(1) == 0)
    def _():
        wgb_ref[...] = wg_ref[...].astype(BF16)
        wub_ref[...] = wu_ref[...].astype(BF16)
        carry_ref[...] = jnp.zeros_like(carry_ref)

    def conv(prev, h, cw_ref, cb_ref):
        ext = jnp.concatenate([prev, h], axis=0)
        y = (ext * cw_ref[2:3, :] + pltpu.roll(ext, 1, 0) * cw_ref[1:2, :]
             + pltpu.roll(ext, 2, 0) * cw_ref[0:1, :] + cb_ref[...])
        return y[SUBLANES:]

    prev_g = carry_ref[0]
    prev_u = carry_ref[1]
    for r in range(0, tm, sub):
        x = x_ref[r:r + sub, :]
        hg = jnp.dot(x, wgb_ref[...], preferred_element_type=F32)
        hu = jnp.dot(x, wub_ref[...], preferred_element_type=F32)
        yg = conv(prev_g, hg, cwg_ref, cbg_ref)
        yu = conv(prev_u, hu, cwu_ref, cbu_ref)
        prev_g = hg[sub - SUBLANES:sub]
        prev_u = hu[sub - SUBLANES:sub]
        o_ref[r:r + sub, :] = (yg * _sigmoid(yg) * yu).astype(BF16)
    carry_ref[0] = prev_g
    carry_ref[1] = prev_u


def _ffn_up(xb, w_up, conv_w, conv_b, layer, tm, sub):
    s = xb.shape[0]
    nb = FFN_BLOCKS
    vmem = 2 * (tm * D_MODEL * 2 + 2 * D_MODEL * FFN_TN * 4 + tm * FFN_TN * 2) + 2 * D_MODEL * FFN_TN * 2 + (12 << 20)
    wspec = lambda off: pl.BlockSpec((None, D_MODEL, FFN_TN), lambda j, i: (layer, 0, j + off))
    cwspec = lambda off: pl.BlockSpec((None, CONV_WIDTH, FFN_TN), lambda j, i: (layer, 0, j + off))
    cbspec = lambda off: pl.BlockSpec((None, 1, FFN_TN), lambda j, i: (layer, 0, j + off))
    return pl.pallas_call(
        functools.partial(_ffn_up_kernel, tm=tm, sub=sub),
        grid=(nb, s // tm),
        in_specs=[pl.BlockSpec((tm, D_MODEL), lambda j, i: (i, 0)),
                  wspec(0), wspec(nb), cwspec(0), cwspec(nb), cbspec(0), cbspec(nb)],
        out_specs=pl.BlockSpec((tm, FFN_TN), lambda j, i: (i, j)),
        out_shape=jax.ShapeDtypeStruct((s, D_FF), BF16),
        scratch_shapes=[pltpu.VMEM((2, SUBLANES, FFN_TN), F32),
                        pltpu.VMEM((D_MODEL, FFN_TN), BF16), pltpu.VMEM((D_MODEL, FFN_TN), BF16)],
        compiler_params=_params(("arbitrary", "arbitrary"), vmem),
        name="ffn_up",
    )(xb, w_up, w_up, conv_w, conv_w, conv_b, conv_b)


def _ffn_down_kernel(a_ref, r_ref, w_ref, g_ref, b_ref, x_ref, xb_ref, *, tm, sub):
    for r in range(0, tm, sub):
        rows = slice(r, r + sub)
        z = r_ref[rows, :] + jnp.dot(a_ref[rows, :], w_ref[...], preferred_element_type=F32)
        x2 = _layer_norm(z, g_ref[...], b_ref[...])
        x_ref[rows, :] = x2
        xb_ref[rows, :] = x2.astype(BF16)


def _ffn_down(act, r, w_down_b, ln_g, ln_b, layer, tm, sub):
    s = r.shape[0]
    row = lambda cols: pl.BlockSpec((tm, cols), lambda i: (i, 0))
    vmem = D_FF * D_MODEL * 2 + 2 * tm * (D_FF * 2 + D_MODEL * 4 + D_MODEL * 4 + D_MODEL * 2) + (8 << 20)
    return pl.pallas_call(
        functools.partial(_ffn_down_kernel, tm=tm, sub=sub),
        grid=(s // tm,),
        in_specs=[row(D_FF), row(D_MODEL),
                  pl.BlockSpec((D_FF, D_MODEL), lambda i: (0, 0), pipeline_mode=pl.Buffered(1)),
                  _layer_vec(layer, D_MODEL), _layer_vec(layer, D_MODEL)],
        out_specs=[row(D_MODEL), row(D_MODEL)],
        out_shape=[jax.ShapeDtypeStruct((s, D_MODEL), F32), jax.ShapeDtypeStruct((s, D_MODEL), BF16)],
        compiler_params=_params(("arbitrary",), vmem),
        name="ffn_down",
    )(act, r, w_down_b, ln_g, ln_b)


def _tiles(s):
    pick = lambda tm, sub: (min(tm, s), min(sub, s))
    return dict(tables=pick(1024, 1024), ret_proj=pick(2048, 256), att_proj=pick(1024, 256),
                mixer=pick(1024, 128), mix_out=pick(512, 256), ffn_up=pick(1024, 1024), ffn_down=pick(512, 256))


def kernel(x, p, positions, w_in, w_out, ret_norm_g, ret_norm_b, attn_sinks, ln1_g, ln1_b, w_ffn_up,
           ffn_conv_w, ffn_conv_b, w_ffn_down, ln2_g, ln2_b, w_ple_gate, b_ple_gate, w_ple_proj):
    batch, s, d = x.shape
    assert batch == 1 and d == D_MODEL and s % RET_CHUNK == 0
    t = _tiles(s)
    assert all(s % tm == 0 and tm % sub == 0 for tm, sub in t.values())

    xf = x.reshape(s, d)
    pf = p.reshape(DEPTH, s, PLE_DIM)
    cos_r, sin_r, cos_a, sin_a, sin_b, xb = _rot_tables(positions.reshape(s), xf, t["tables"][0])
    ret_consts, k_decay = _retention_constants()
    vecs = lambda v: v.reshape(DEPTH, 1, -1)
    gn_g, gn_b, g1, b1, g2, b2 = (vecs(v) for v in (ret_norm_g, ret_norm_b, ln1_g, ln1_b, ln2_g, ln2_b))
    bg, conv_b = vecs(b_ple_gate), vecs(ffn_conv_b)

    for l in range(DEPTH):
        rp, kdec, w_att_b = _ret_proj(xb, w_in, l, cos_r, sin_r, k_decay, *t["ret_proj"])
        ap = _att_proj(xb, w_att_b, cos_a, sin_a, sin_b, *t["att_proj"])
        cat, w_out_b, w_gate_b, w_proj_b = _mixer(rp, kdec, ap, ret_consts, gn_g, gn_b, attn_sinks,
                                                  w_out, w_ple_gate, w_ple_proj, l, t["mixer"][0])
        r, xb, w_down_b = _mix_out(cat, xf, pf, l, w_out_b, w_gate_b, w_proj_b, g1, b1, bg, w_ffn_down,
                                   *t["mix_out"])
        act = _ffn_up(xb, w_ffn_up, ffn_conv_w, conv_b, l, *t["ffn_up"])
        xf, xb = _ffn_down(act, r, w_down_b, g2, b2, l, *t["ffn_down"])
    return xf.reshape(batch, s, d)
```

```python
import functools

import jax
import jax.numpy as jnp
from jax import lax
from jax.experimental import pallas as pl
from jax.experimental.pallas import tpu as pltpu

F32 = jnp.float32
BF16 = jnp.bfloat16

D_MODEL = 2048
DEPTH = 4
PLE_DIM = 256
RET_WIDTH = 1024
RET_HEADS = 4
RET_HEAD_DIM = 256
RET_CHUNK = 128
RET_ROT_BASE = 10000.0
ATT_HEAD_DIM = 64
ATT_WIDTH = 1024
ATT_HEADS = 16
ATT_KV_HEADS = 2
ATT_GROUP = ATT_HEADS // ATT_KV_HEADS
KV_WIDTH = 128
WINDOW = 128
ROPE_THETA = 10000.0
D_FF = 5632
CONV_WIDTH = 3
LN_EPS = 1e-5
GN_EPS = 1e-6
DEEPNORM_ALPHA = (2 * DEPTH) ** 0.25

LANES = 128
SUBLANES = 8
BF16_SUBLANE_ROWS = 16
MXU_DIM = 256
VMEM_LIMIT_CAP = 58 * 1024 * 1024

RET_COLS = 4 * RET_WIDTH
ATT_IN_COLS = ATT_WIDTH + 2 * KV_WIDTH
ATT_OUT_COLS = ATT_WIDTH + 4 * KV_WIDTH
HALF = LANES // 2
LOG2_E = 1.4426950408889634
ATT_SCORE_SCALE = ATT_HEAD_DIM ** -0.5 * LOG2_E


def _params(semantics, vmem_bytes):
    return pltpu.CompilerParams(dimension_semantics=semantics,
                                vmem_limit_bytes=int(min(vmem_bytes, VMEM_LIMIT_CAP)))


def _layer_norm(z, g, b):
    mu = jnp.mean(z, axis=-1, keepdims=True)
    d = z - mu
    var = jnp.mean(d * d, axis=-1, keepdims=True)
    return d * lax.rsqrt(var + LN_EPS) * g + b


def _sigmoid(v):
    return 1.0 / (1.0 + jnp.exp2(v * -LOG2_E))


def _slab_rows(rows, steps):
    assert rows % steps == 0 and (rows // steps) % BF16_SUBLANE_ROWS == 0, (rows, steps)
    return rows // steps


def _layer_vec(layer, cols):
    return pl.BlockSpec((None, 1, cols), lambda *_: (layer, 0, 0))


def _rot_tables_kernel(pos_ref, fr_ref, fa_ref, x_ref, cr_ref, sr_ref, ca_ref, sa_ref, sb_ref, xb_ref):
    xb_ref[...] = x_ref[...].astype(BF16)
    pos = pos_ref[...].astype(F32)
    ang_r = pos * fr_ref[...]
    cr_ref[...] = jnp.cos(ang_r)
    sr_ref[...] = jnp.sin(ang_r)
    ang_a = pos * fa_ref[...]
    c = jnp.cos(ang_a)
    s = jnp.sin(ang_a)
    lane = lax.broadcasted_iota(jnp.int32, c.shape, 1)
    first_half = (lane & (ATT_HEAD_DIM // 2)) == 0
    ca_ref[...] = c
    sa_ref[...] = jnp.where(first_half, -s, 0.0)
    sb_ref[...] = jnp.where(first_half, 0.0, s)


def _rot_tables(positions, xf, tm):
    s = positions.shape[0]
    ret_inv_freq = 1.0 / (RET_ROT_BASE ** jnp.linspace(0.0, 1.0, RET_HEAD_DIM // 2, dtype=F32))
    att_inv_freq = ROPE_THETA ** (-jnp.arange(0, ATT_HEAD_DIM, 2, dtype=F32) / ATT_HEAD_DIM)
    fr = ret_inv_freq.reshape(1, LANES)
    fa = jnp.tile(att_inv_freq, LANES // (ATT_HEAD_DIM // 2)).reshape(1, LANES)
    tab = jax.ShapeDtypeStruct((s, LANES), F32)
    row = pl.BlockSpec((tm, LANES), lambda i: (i, 0))
    const = pl.BlockSpec((1, LANES), lambda i: (0, 0))
    wide = pl.BlockSpec((tm, D_MODEL), lambda i: (i, 0))
    return pl.pallas_call(
        _rot_tables_kernel,
        grid=(s // tm,),
        in_specs=[pl.BlockSpec((tm, 1), lambda i: (i, 0)), const, const, wide],
        out_specs=[row] * 5 + [wide],
        out_shape=[tab] * 5 + [jax.ShapeDtypeStruct((s, D_MODEL), BF16)],
        compiler_params=_params(("arbitrary",), 2 * tm * D_MODEL * 6 + (16 << 20)),
        name="rot_tables",
    )(positions.reshape(s, 1), fr, fa, xf)


RET_TN = 512
RET_BLOCKS_PER_PART = RET_WIDTH // RET_TN
RET_HEADS_PER_BLOCK = RET_TN // RET_HEAD_DIM


def _ret_proj_kernel(x_ref, w_ref, c_ref, s_ref, kd_ref, wrow_ref, o_ref, kdec_ref, watt_ref, wb_ref, *, tm, sub):
    j = pl.program_id(1)
    part = j // RET_BLOCKS_PER_PART
    watt_ref[...] = wrow_ref[:, RET_COLS:].astype(BF16)

    def sub_blocks():
        wb_ref[...] = w_ref[...].astype(BF16)
        for r in range(0, tm, sub):
            rows = slice(r, r + sub)
            yield rows, jnp.dot(x_ref[rows, :], wb_ref[...], preferred_element_type=F32)

    def rotary(rows, acc, scale):
        c = c_ref[rows, :] * scale
        s = s_ref[rows, :] * scale
        for h in range(RET_HEADS_PER_BLOCK):
            lo = h * RET_HEAD_DIM
            mid = lo + RET_HEAD_DIM // 2
            a1 = acc[:, lo:mid]
            a2 = acc[:, mid:lo + RET_HEAD_DIM]
            yield h, lo, a1 * c - a2 * s
            yield h, mid, a1 * s + a2 * c

    @pl.when(part == 0)
    def _():
        for rows, acc in sub_blocks():
            for _, col, val in rotary(rows, acc, 1.0):
                o_ref[rows, col:col + LANES] = val.astype(BF16)

    @pl.when(part == 1)
    def _():
        for rows, acc in sub_blocks():
            for h, col, val in rotary(rows, acc, RET_HEAD_DIM ** -0.5):
                o_ref[rows, col:col + LANES] = val.astype(BF16)
                chunks = val.reshape(sub // RET_CHUNK, RET_CHUNK, LANES) * kd_ref[h][None]
                kdec_ref[rows, col:col + LANES] = chunks.reshape(sub, LANES).astype(BF16)

    @pl.when(part == 2)
    def _():
        for rows, acc in sub_blocks():
            o_ref[rows, :] = acc.astype(BF16)

    @pl.when(part == 3)
    def _():
        for rows, acc in sub_blocks():
            o_ref[rows, :] = (acc * _sigmoid(acc)).astype(BF16)


def _ret_proj(xb, w_in, layer, cos_r, sin_r, k_decay, tm, sub):
    s = xb.shape[0]
    vmem = 2 * (tm * D_MODEL * 2 + D_MODEL * RET_TN * 4 + 2 * tm * RET_TN * 2 + 2 * tm * LANES * 4) \
        + D_MODEL * RET_TN * 2 + (12 << 20)
    k_block = lambda j: jnp.clip(j - RET_BLOCKS_PER_PART, 0, RET_BLOCKS_PER_PART - 1)
    nj = RET_COLS // RET_TN
    slab = _slab_rows(D_MODEL, (s // tm) * nj)
    return pl.pallas_call(
        functools.partial(_ret_proj_kernel, tm=tm, sub=sub),
        grid=(s // tm, nj),
        in_specs=[pl.BlockSpec((tm, D_MODEL), lambda i, j: (i, 0)),
                  pl.BlockSpec((None, D_MODEL, RET_TN), lambda i, j: (layer, 0, j)),
                  pl.BlockSpec((tm, LANES), lambda i, j: (i, 0)),
                  pl.BlockSpec((tm, LANES), lambda i, j: (i, 0)),
                  pl.BlockSpec((RET_HEADS_PER_BLOCK, RET_CHUNK, LANES), lambda i, j: (k_block(j), 0, 0)),
                  pl.BlockSpec((None, slab, RET_COLS + ATT_IN_COLS), lambda i, j: (layer, i * nj + j, 0))],
        out_specs=[pl.BlockSpec((tm, RET_TN), lambda i, j: (i, j)),
                   pl.BlockSpec((tm, RET_TN), lambda i, j: (i, k_block(j))),
                   pl.BlockSpec((slab, ATT_IN_COLS), lambda i, j: (i * nj + j, 0))],
        out_shape=[jax.ShapeDtypeStruct((s, RET_COLS), BF16), jax.ShapeDtypeStruct((s, RET_WIDTH), BF16),
                   jax.ShapeDtypeStruct((D_MODEL, ATT_IN_COLS), BF16)],
        scratch_shapes=[pltpu.VMEM((D_MODEL, RET_TN), BF16)],
        compiler_params=_params(("arbitrary", "arbitrary"), vmem),
        name="ret_proj",
    )(xb, w_in, cos_r, sin_r, k_decay, w_in)


ATT_W_BLOCKS = ATT_IN_COLS // MXU_DIM


def _att_proj_kernel(x_ref, *refs, tm, sub):
    w_refs = refs[:ATT_W_BLOCKS]
    c_ref, sa_ref, sb_ref, o_ref = refs[ATT_W_BLOCKS:]

    for r in range(0, tm, sub):
        rows = slice(r, r + sub)
        c = c_ref[rows, :]
        sa = sa_ref[rows, :]
        sb = sb_ref[rows, :]

        def rot(a):
            return (a * c + pltpu.roll(a, LANES - ATT_HEAD_DIM // 2, 1) * sa
                    + pltpu.roll(a, ATT_HEAD_DIM // 2, 1) * sb)

        x = x_ref[rows, :]
        for wb in range(ATT_W_BLOCKS):
            acc = jnp.dot(x, w_refs[wb][...], preferred_element_type=F32)
            for half in range(MXU_DIM // LANES):
                col = wb * MXU_DIM + half * LANES
                a = acc[:, half * LANES:(half + 1) * LANES]
                if col < ATT_WIDTH:
                    o_ref[rows, col:col + LANES] = (rot(a) * ATT_SCORE_SCALE).astype(BF16)
                elif col < ATT_WIDTH + KV_WIDTH:
                    k = rot(a)
                    o_ref[rows, col:col + LANES] = k.astype(BF16)
                    o_ref[rows, col + 2 * LANES:col + 3 * LANES] = pltpu.roll(k, HALF, 1).astype(BF16)
                else:
                    o_ref[rows, col:col + LANES] = a.astype(BF16)
                    o_ref[rows, col + 2 * LANES:col + 3 * LANES] = pltpu.roll(a, HALF, 1).astype(BF16)


def _att_proj(xb, w_att_b, cos_a, sin_a, sin_b, tm, sub):
    s = xb.shape[0]
    vmem = 2 * (tm * D_MODEL * 2 + D_MODEL * ATT_IN_COLS * 2 + tm * ATT_OUT_COLS * 2 + 3 * tm * LANES * 4) + (12 << 20)
    row = pl.BlockSpec((tm, LANES), lambda i: (i, 0))
    w_specs = [pl.BlockSpec((D_MODEL, MXU_DIM), functools.partial(lambda i, wb: (0, wb), wb=wb))
               for wb in range(ATT_W_BLOCKS)]
    return pl.pallas_call(
        functools.partial(_att_proj_kernel, tm=tm, sub=sub),
        grid=(s // tm,),
        in_specs=[pl.BlockSpec((tm, D_MODEL), lambda i: (i, 0))] + w_specs + [row, row, row],
        out_specs=pl.BlockSpec((tm, ATT_OUT_COLS), lambda i: (i, 0)),
        out_shape=jax.ShapeDtypeStruct((s, ATT_OUT_COLS), BF16),
        compiler_params=_params(("arbitrary",), vmem),
        name="att_proj",
    )(xb, *([w_att_b] * ATT_W_BLOCKS), cos_a, sin_a, sin_b)


def _mixer_kernel(cd_ref, sink_ref, r_ref, kdec_ref, a_ref, dec_ref, qd_ref, g_ref, b_ref, wo_ref, wg_ref, wp_ref,
                  o_ref, wob_ref, wgb_ref, wpb_ref, state_ref, kv_ref, *, tm, layer):
    i = pl.program_id(0)
    wob_ref[...] = wo_ref[...].astype(BF16)
    wgb_ref[...] = wg_ref[...].astype(BF16)
    wpb_ref[...] = wp_ref[...].astype(BF16)
    n_chunks = tm // RET_CHUNK
    kv_cols = 4 * KV_WIDTH

    @pl.when(i == 0)
    def _():
        state_ref[...] = jnp.zeros_like(state_ref)
        kv_ref[0:WINDOW, :] = jnp.zeros((WINDOW, kv_cols), BF16)

    @pl.when(i > 0)
    def _():
        kv_ref[0:WINDOW, :] = kv_ref[tm:tm + WINDOW, :]

    kv_ref[WINDOW:tm + WINDOW, :] = a_ref[:, ATT_WIDTH:ATT_WIDTH + kv_cols]

    contract_last = (((1,), (1,)), ((), ()))
    contract_first = (((0,), (0,)), ((), ()))
    qi = lax.broadcasted_iota(jnp.int32, (WINDOW, 2 * WINDOW), 0)
    kj = lax.broadcasted_iota(jnp.int32, (WINDOW, 2 * WINDOW), 1)
    band = (kj > qi) & (kj <= qi + WINDOW)
    low_lanes = lax.broadcasted_iota(jnp.int32, (2 * WINDOW, LANES), 1) < HALF

    def chunk(c, carry):
        r0 = pl.multiple_of(c * RET_CHUNK, RET_CHUNK)
        rows = pl.ds(r0, RET_CHUNK)

        for h in range(RET_HEADS):
            lo = h * RET_HEAD_DIM
            hi = lo + RET_HEAD_DIM
            q = r_ref[rows, lo:hi]
            k = r_ref[rows, RET_WIDTH + lo:RET_WIDTH + hi]
            v = r_ref[rows, 2 * RET_WIDTH + lo:2 * RET_WIDTH + hi]
            silu_gate = r_ref[rows, 3 * RET_WIDTH + lo:3 * RET_WIDTH + hi].astype(F32)
            st = state_ref[h]
            sc = lax.dot_general(q, k, contract_last, preferred_element_type=F32) * dec_ref[h]
            inner = jnp.dot(sc.astype(BF16), v, preferred_element_type=F32)
            cross = jnp.dot(q, st.astype(BF16), preferred_element_type=F32) * qd_ref[h]
            state_ref[h] = st * cd_ref[h] + lax.dot_general(kdec_ref[rows, lo:hi], v, contract_first,
                                                            preferred_element_type=F32)
            y = inner + cross
            mu = jnp.mean(y, axis=-1, keepdims=True)
            d = y - mu
            var = jnp.mean(d * d, axis=-1, keepdims=True)
            yn = d * lax.rsqrt(var + GN_EPS) * g_ref[:, lo:hi] + b_ref[:, lo:hi]
            o_ref[rows, lo:hi] = (silu_gate * yn).astype(BF16)

        kk = kv_ref[pl.ds(r0, 2 * WINDOW), :]
        k_nat = kk[:, 0:LANES]
        v_nat = kk[:, LANES:2 * LANES]
        k_swp = kk[:, 2 * LANES:3 * LANES]
        v_swp = kk[:, 3 * LANES:4 * LANES]
        zero = jnp.zeros_like(k_nat)
        k_ext = ((jnp.where(low_lanes, k_nat, zero), jnp.where(low_lanes, zero, k_swp)),
                 (jnp.where(low_lanes, k_swp, zero), jnp.where(low_lanes, zero, k_nat)))
        v_ext = ((jnp.where(low_lanes, v_nat, zero), jnp.where(low_lanes, zero, v_swp)),
                 (jnp.where(low_lanes, v_swp, zero), jnp.where(low_lanes, zero, v_nat)))
        first_block = (i * n_chunks + c) == 0
        mask = band & ((kj >= WINDOW) | jnp.logical_not(first_block))
        pairs_per_kv = ATT_GROUP // 2
        for kvh in range(ATT_KV_HEADS):
            for pr in range(pairs_per_kv):
                pair = kvh * pairs_per_kv + pr
                q_pair = a_ref[rows, pair * LANES:(pair + 1) * LANES]
                o_pair = None
                for par in range(2):
                    sink = sink_ref[layer, 2 * pair + par] * LOG2_E
                    sc = lax.dot_general(q_pair, k_ext[kvh][par], contract_last, preferred_element_type=F32)
                    sc = jnp.where(mask, sc, -jnp.inf)
                    m = jnp.maximum(jnp.max(sc, axis=-1, keepdims=True), sink)
                    p = jnp.exp2(sc - m)
                    den = jnp.sum(p, axis=-1, keepdims=True) + jnp.exp2(sink - m)
                    o = jnp.dot(p.astype(BF16), v_ext[kvh][par], preferred_element_type=F32) * (1.0 / den)
                    o_pair = o if o_pair is None else o_pair + o
                o_ref[rows, RET_WIDTH + pair * LANES:RET_WIDTH + (pair + 1) * LANES] = o_pair.astype(BF16)
        return carry

    lax.fori_loop(0, n_chunks, chunk, 0, unroll=4)


def _mixer(rp, kdec, ap, ret_consts, gn_g, gn_b, sinks, w_out, w_gate, w_proj, layer, tm):
    s = rp.shape[0]
    decay, q_decay, chunk_decay = ret_consts
    smem = pl.BlockSpec(memory_space=pltpu.SMEM)
    const3 = lambda shape: pl.BlockSpec(shape, lambda i: (0, 0, 0))
    steps = s // tm
    slab_d, slab_p = _slab_rows(D_MODEL, steps), _slab_rows(PLE_DIM, steps)
    w_in_spec = lambda rows: pl.BlockSpec((None, rows, D_MODEL), lambda i: (layer, i, 0))
    w_out_spec = lambda rows: pl.BlockSpec((rows, D_MODEL), lambda i: (i, 0))
    w_shape = lambda rows: jax.ShapeDtypeStruct((rows, D_MODEL), BF16)
    vmem = 2 * (tm * (RET_COLS + RET_WIDTH) * 2 + tm * ATT_OUT_COLS * 2 + tm * D_MODEL * 2) \
        + 2 * (2 * slab_d + slab_p) * D_MODEL * 6 \
        + 2 * 4 * (decay.size + q_decay.size) \
        + RET_HEADS * RET_HEAD_DIM * RET_HEAD_DIM * 4 + (tm + WINDOW) * 4 * KV_WIDTH * 2 + (14 << 20)
    return pl.pallas_call(
        functools.partial(_mixer_kernel, tm=tm, layer=layer),
        grid=(s // tm,),
        in_specs=[smem, smem,
                  pl.BlockSpec((tm, RET_COLS), lambda i: (i, 0)),
                  pl.BlockSpec((tm, RET_WIDTH), lambda i: (i, 0)),
                  pl.BlockSpec((tm, ATT_OUT_COLS), lambda i: (i, 0)),
                  const3(decay.shape), const3(q_decay.shape),
                  _layer_vec(layer, RET_WIDTH), _layer_vec(layer, RET_WIDTH),
                  w_in_spec(slab_d), w_in_spec(slab_d), w_in_spec(slab_p)],
        out_specs=[pl.BlockSpec((tm, D_MODEL), lambda i: (i, 0)),
                   w_out_spec(slab_d), w_out_spec(slab_d), w_out_spec(slab_p)],
        out_shape=[jax.ShapeDtypeStruct((s, D_MODEL), BF16), w_shape(D_MODEL), w_shape(D_MODEL), w_shape(PLE_DIM)],
        scratch_shapes=[pltpu.VMEM((RET_HEADS, RET_HEAD_DIM, RET_HEAD_DIM), F32),
                        pltpu.VMEM((tm + WINDOW, 4 * KV_WIDTH), BF16)],
        compiler_params=_params(("arbitrary",), vmem),
        name="mixer",
    )(chunk_decay, sinks, rp, kdec, ap, decay, q_decay, gn_g, gn_b, w_out, w_gate, w_proj)


def _retention_constants():
    c = RET_CHUNK
    log_g = jnp.log1p(-jnp.exp2(-5.0 - jnp.arange(RET_HEADS, dtype=F32)))
    idx = jnp.arange(c, dtype=F32)
    diff = idx[:, None] - idx[None, :]
    decay = jnp.where(diff[None] >= 0, jnp.exp(log_g[:, None, None] * jnp.maximum(diff, 0.0)[None]), 0.0)
    q_decay = jnp.exp(log_g[:, None] * (idx[None, :] + 1.0))
    k_decay = jnp.exp(log_g[:, None] * (c - 1.0 - idx[None, :]))
    chunk_decay = jnp.exp(log_g * c)
    wide = lambda t, lanes: jnp.broadcast_to(t[:, :, None], (RET_HEADS, c, lanes))
    return (decay, wide(q_decay, RET_HEAD_DIM), chunk_decay), wide(k_decay, LANES)


def _mix_out_kernel(cat_ref, x_ref, p_ref, wo_ref, wg_ref, wp_ref, g_ref, b_ref, bg_ref, wd_ref,
                    r_ref, xb_ref, wdb_ref, *, tm, sub):
    wdb_ref[...] = wd_ref[...].astype(BF16)
    for r in range(0, tm, sub):
        rows = slice(r, r + sub)
        z = DEEPNORM_ALPHA * x_ref[rows, :] + jnp.dot(cat_ref[rows, :], wo_ref[...], preferred_element_type=F32)
        x1 = _layer_norm(z, g_ref[...], b_ref[...])
        x1b = x1.astype(BF16)
        gate = jnp.dot(x1b, wg_ref[...], preferred_element_type=F32) + bg_ref[...]
        ple = jnp.dot(p_ref[rows, :].astype(BF16), wp_ref[...], preferred_element_type=F32) * _sigmoid(gate)
        r_ref[rows, :] = DEEPNORM_ALPHA * x1 + ple
        xb_ref[rows, :] = x1b


def _mix_out(cat, x, p, layer, w_out_b, w_gate_b, w_proj_b, ln_g, ln_b, b_gate, w_down, tm, sub):
    s = x.shape[0]
    row = lambda cols: pl.BlockSpec((tm, cols), lambda i: (i, 0))
    resident = lambda rows_: pl.BlockSpec((rows_, D_MODEL), lambda i: (0, 0), pipeline_mode=pl.Buffered(1))
    slab = _slab_rows(D_FF, s // tm)
    vmem = (2 * D_MODEL * D_MODEL + PLE_DIM * D_MODEL) * 2 + 2 * slab * D_MODEL * 6 \
        + 2 * tm * (D_MODEL * 2 + D_MODEL * 4 + PLE_DIM * 4 + D_MODEL * 4 + D_MODEL * 2) + (12 << 20)
    return pl.pallas_call(
        functools.partial(_mix_out_kernel, tm=tm, sub=sub),
        grid=(s // tm,),
        in_specs=[row(D_MODEL), row(D_MODEL),
                  pl.BlockSpec((None, tm, PLE_DIM), lambda i: (layer, i, 0)),
                  resident(D_MODEL), resident(D_MODEL), resident(PLE_DIM),
                  _layer_vec(layer, D_MODEL), _layer_vec(layer, D_MODEL), _layer_vec(layer, D_MODEL),
                  pl.BlockSpec((None, slab, D_MODEL), lambda i: (layer, i, 0))],
        out_specs=[row(D_MODEL), row(D_MODEL), pl.BlockSpec((slab, D_MODEL), lambda i: (i, 0))],
        out_shape=[jax.ShapeDtypeStruct((s, D_MODEL), F32), jax.ShapeDtypeStruct((s, D_MODEL), BF16),
                   jax.ShapeDtypeStruct((D_FF, D_MODEL), BF16)],
        compiler_params=_params(("arbitrary",), vmem),
        name="mix_out",
    )(cat, x, p, w_out_b, w_gate_b, w_proj_b, ln_g, ln_b, b_gate, w_down)


FFN_TN = 512
FFN_BLOCKS = D_FF // FFN_TN


def _ffn_up_kernel(x_ref, wg_ref, wu_ref, cwg_ref, cwu_ref, cbg_ref, cbu_ref, o_ref, carry_ref, wgb_ref, wub_ref,
                   *, tm, sub):
    @pl.when(pl.program_id(1) == 0)
    def _():
        wgb_ref[...] = wg_ref[...].astype(BF16)
        wub_ref[...] = wu_ref[...].astype(BF16)
        carry_ref[...] = jnp.zeros_like(carry_ref)

    def conv(prev, h, cw_ref, cb_ref):
        ext = jnp.concatenate([prev, h], axis=0)
        y = (ext * cw_ref[2:3, :] + pltpu.roll(ext, 1, 0) * cw_ref[1:2, :]
             + pltpu.roll(ext, 2, 0) * cw_ref[0:1, :] + cb_ref[...])
        return y[SUBLANES:]

    prev_g = carry_ref[0]
    prev_u = carry_ref[1]
    for r in range(0, tm, sub):
        x = x_ref[r:r + sub, :]
        hg = jnp.dot(x, wgb_ref[...], preferred_element_type=F32)
        hu = jnp.dot(x, wub_ref[...], preferred_element_type=F32)
        yg = conv(prev_g, hg, cwg_ref, cbg_ref)
        yu = conv(prev_u, hu, cwu_ref, cbu_ref)
        prev_g = hg[sub - SUBLANES:sub]
        prev_u = hu[sub - SUBLANES:sub]
        o_ref[r:r + sub, :] = (yg * _sigmoid(yg) * yu).astype(BF16)
    carry_ref[0] = prev_g
    carry_ref[1] = prev_u


def _ffn_up(xb, w_up, conv_w, conv_b, layer, tm, sub):
    s = xb.shape[0]
    nb = FFN_BLOCKS
    vmem = 2 * (tm * D_MODEL * 2 + 2 * D_MODEL * FFN_TN * 4 + tm * FFN_TN * 2) + 2 * D_MODEL * FFN_TN * 2 + (12 << 20)
    wspec = lambda off: pl.BlockSpec((None, D_MODEL, FFN_TN), lambda j, i: (layer, 0, j + off))
    cwspec = lambda off: pl.BlockSpec((None, CONV_WIDTH, FFN_TN), lambda j, i: (layer, 0, j + off))
    cbspec = lambda off: pl.BlockSpec((None, 1, FFN_TN), lambda j, i: (layer, 0, j + off))
    return pl.pallas_call(
        functools.partial(_ffn_up_kernel, tm=tm, sub=sub),
        grid=(nb, s // tm),
        in_specs=[pl.BlockSpec((tm, D_MODEL), lambda j, i: (i, 0)),
                  wspec(0), wspec(nb), cwspec(0), cwspec(nb), cbspec(0), cbspec(nb)],
        out_specs=pl.BlockSpec((tm, FFN_TN), lambda j, i: (i, j)),
        out_shape=jax.ShapeDtypeStruct((s, D_FF), BF16),
        scratch_shapes=[pltpu.VMEM((2, SUBLANES, FFN_TN), F32),
                        pltpu.VMEM((D_MODEL, FFN_TN), BF16), pltpu.VMEM((D_MODEL, FFN_TN), BF16)],
        compiler_params=_params(("arbitrary", "arbitrary"), vmem),
        name="ffn_up",
    )(xb, w_up, w_up, conv_w, conv_w, conv_b, conv_b)


def _ffn_down_kernel(a_ref, r_ref, w_ref, g_ref, b_ref, x_ref, xb_ref, *, tm, sub):
    for r in range(0, tm, sub):
        rows = slice(r, r + sub)
        z = r_ref[rows, :] + jnp.dot(a_ref[rows, :], w_ref[...], preferred_element_type=F32)
        x2 = _layer_norm(z, g_ref[...], b_ref[...])
        x_ref[rows, :] = x2
        xb_ref[rows, :] = x2.astype(BF16)


def _ffn_down(act, r, w_down_b, ln_g, ln_b, layer, tm, sub):
    s = r.shape[0]
    row = lambda cols: pl.BlockSpec((tm, cols), lambda i: (i, 0))
    vmem = D_FF * D_MODEL * 2 + 2 * tm * (D_FF * 2 + D_MODEL * 4 + D_MODEL * 4 + D_MODEL * 2) + (8 << 20)
    return pl.pallas_call(
        functools.partial(_ffn_down_kernel, tm=tm, sub=sub),
        grid=(s // tm,),
        in_specs=[row(D_FF), row(D_MODEL),
                  pl.BlockSpec((D_FF, D_MODEL), lambda i: (0, 0), pipeline_mode=pl.Buffered(1)),
                  _layer_vec(layer, D_MODEL), _layer_vec(layer, D_MODEL)],
        out_specs=[row(D_MODEL), row(D_MODEL)],
        out_shape=[jax.ShapeDtypeStruct((s, D_MODEL), F32), jax.ShapeDtypeStruct((s, D_MODEL), BF16)],
        compiler_params=_params(("arbitrary",), vmem),
        name="ffn_down",
    )(act, r, w_down_b, ln_g, ln_b)


def _tiles(s):
    pick = lambda tm, sub: (min(tm, s), min(sub, s))
    return dict(tables=pick(1024, 1024), ret_proj=pick(2048, 1024), att_proj=pick(1024, 256),
                mixer=pick(1024, 128), mix_out=pick(512, 512), ffn_up=pick(1024, 1024), ffn_down=pick(512, 256))


def kernel(x, p, positions, w_in, w_out, ret_norm_g, ret_norm_b, attn_sinks, ln1_g, ln1_b, w_ffn_up,
           ffn_conv_w, ffn_conv_b, w_ffn_down, ln2_g, ln2_b, w_ple_gate, b_ple_gate, w_ple_proj):
    batch, s, d = x.shape
    assert batch == 1 and d == D_MODEL and s % RET_CHUNK == 0
    t = _tiles(s)
    assert all(s % tm == 0 and tm % sub == 0 for tm, sub in t.values())

    xf = x.reshape(s, d)
    pf = p.reshape(DEPTH, s, PLE_DIM)
    cos_r, sin_r, cos_a, sin_a, sin_b, xb = _rot_tables(positions.reshape(s), xf, t["tables"][0])
    ret_consts, k_decay = _retention_constants()
    vecs = lambda v: v.reshape(DEPTH, 1, -1)
    gn_g, gn_b, g1, b1, g2, b2 = (vecs(v) for v in (ret_norm_g, ret_norm_b, ln1_g, ln1_b, ln2_g, ln2_b))
    bg, conv_b = vecs(b_ple_gate), vecs(ffn_conv_b)

    for l in range(DEPTH):
        rp, kdec, w_att_b = _ret_proj(xb, w_in, l, cos_r, sin_r, k_decay, *t["ret_proj"])
        ap = _att_proj(xb, w_att_b, cos_a, sin_a, sin_b, *t["att_proj"])
        cat, w_out_b, w_gate_b, w_proj_b = _mixer(rp, kdec, ap, ret_consts, gn_g, gn_b, attn_sinks,
                                                  w_out, w_ple_gate, w_ple_proj, l, t["mixer"][0])
        r, xb, w_down_b = _mix_out(cat, xf, pf, l, w_out_b, w_gate_b, w_proj_b, g1, b1, bg, w_ffn_down,
                                   *t["mix_out"])
        act = _ffn_up(xb, w_ffn_up, ffn_conv_w, conv_b, l, *t["ffn_up"])
        xf, xb = _ffn_down(act, r, w_down_b, g2, b2, l, *t["ffn_down"])
    return xf.reshape(batch, s, d)
```

```python
import functools

import jax
import jax.numpy as jnp
from jax import lax
from jax.experimental import pallas as pl
from jax.experimental.pallas import tpu as pltpu

F32 = jnp.float32
BF16 = jnp.bfloat16

D_MODEL = 2048
DEPTH = 4
PLE_DIM = 256
RET_WIDTH = 1024
RET_HEADS = 4
RET_HEAD_DIM = 256
RET_CHUNK = 128
RET_ROT_BASE = 10000.0
ATT_HEAD_DIM = 64
ATT_WIDTH = 1024
ATT_HEADS = 16
ATT_KV_HEADS = 2
ATT_GROUP = ATT_HEADS // ATT_KV_HEADS
KV_WIDTH = 128
WINDOW = 128
ROPE_THETA = 10000.0
D_FF = 5632
CONV_WIDTH = 3
LN_EPS = 1e-5
GN_EPS = 1e-6
DEEPNORM_ALPHA = (2 * DEPTH) ** 0.25

LANES = 128
SUBLANES = 8
BF16_SUBLANE_ROWS = 16
MXU_DIM = 256
VMEM_LIMIT_CAP = 58 * 1024 * 1024

RET_COLS = 4 * RET_WIDTH
ATT_IN_COLS = ATT_WIDTH + 2 * KV_WIDTH
ATT_OUT_COLS = ATT_WIDTH + 4 * KV_WIDTH
HALF = LANES // 2
LOG2_E = 1.4426950408889634
ATT_SCORE_SCALE = ATT_HEAD_DIM ** -0.5 * LOG2_E


def _params(semantics, vmem_bytes):
    return pltpu.CompilerParams(dimension_semantics=semantics,
                                vmem_limit_bytes=int(min(vmem_bytes, VMEM_LIMIT_CAP)))


def _layer_norm(z, g, b):
    mu = jnp.mean(z, axis=-1, keepdims=True)
    d = z - mu
    var = jnp.mean(d * d, axis=-1, keepdims=True)
    return d * lax.rsqrt(var + LN_EPS) * g + b


def _sigmoid(v):
    return 0.5 + 0.5 * jnp.tanh(0.5 * v)


def _slab_rows(rows, steps):
    assert rows % steps == 0 and (rows // steps) % BF16_SUBLANE_ROWS == 0, (rows, steps)
    return rows // steps


def _layer_vec(layer, cols):
    return pl.BlockSpec((None, 1, cols), lambda *_: (layer, 0, 0))


def _rot_tables_kernel(pos_ref, fr_ref, fa_ref, x_ref, cr_ref, sr_ref, ca_ref, sa_ref, sb_ref, xb_ref):
    xb_ref[...] = x_ref[...].astype(BF16)
    pos = pos_ref[...].astype(F32)
    ang_r = pos * fr_ref[...]
    cr_ref[...] = jnp.cos(ang_r)
    sr_ref[...] = jnp.sin(ang_r)
    ang_a = pos * fa_ref[...]
    c = jnp.cos(ang_a)
    s = jnp.sin(ang_a)
    lane = lax.broadcasted_iota(jnp.int32, c.shape, 1)
    first_half = (lane & (ATT_HEAD_DIM // 2)) == 0
    ca_ref[...] = c
    sa_ref[...] = jnp.where(first_half, -s, 0.0)
    sb_ref[...] = jnp.where(first_half, 0.0, s)


def _rot_tables(positions, xf, tm):
    s = positions.shape[0]
    ret_inv_freq = 1.0 / (RET_ROT_BASE ** jnp.linspace(0.0, 1.0, RET_HEAD_DIM // 2, dtype=F32))
    att_inv_freq = ROPE_THETA ** (-jnp.arange(0, ATT_HEAD_DIM, 2, dtype=F32) / ATT_HEAD_DIM)
    fr = ret_inv_freq.reshape(1, LANES)
    fa = jnp.tile(att_inv_freq, LANES // (ATT_HEAD_DIM // 2)).reshape(1, LANES)
    tab = jax.ShapeDtypeStruct((s, LANES), F32)
    row = pl.BlockSpec((tm, LANES), lambda i: (i, 0))
    const = pl.BlockSpec((1, LANES), lambda i: (0, 0))
    wide = pl.BlockSpec((tm, D_MODEL), lambda i: (i, 0))
    return pl.pallas_call(
        _rot_tables_kernel,
        grid=(s // tm,),
        in_specs=[pl.BlockSpec((tm, 1), lambda i: (i, 0)), const, const, wide],
        out_specs=[row] * 5 + [wide],
        out_shape=[tab] * 5 + [jax.ShapeDtypeStruct((s, D_MODEL), BF16)],
        compiler_params=_params(("arbitrary",), 2 * tm * D_MODEL * 6 + (16 << 20)),
        name="rot_tables",
    )(positions.reshape(s, 1), fr, fa, xf)


RET_TN = 512
RET_BLOCKS_PER_PART = RET_WIDTH // RET_TN
RET_HEADS_PER_BLOCK = RET_TN // RET_HEAD_DIM


def _ret_proj_kernel(x_ref, w_ref, c_ref, s_ref, kd_ref, wrow_ref, o_ref, kdec_ref, watt_ref, wb_ref, *, tm, sub):
    j = pl.program_id(1)
    part = j // RET_BLOCKS_PER_PART
    watt_ref[...] = wrow_ref[:, RET_COLS:].astype(BF16)

    def sub_blocks():
        wb_ref[...] = w_ref[...].astype(BF16)
        for r in range(0, tm, sub):
            rows = slice(r, r + sub)
            yield rows, jnp.dot(x_ref[rows, :], wb_ref[...], preferred_element_type=F32)

    def rotary(rows, acc, scale):
        c = c_ref[rows, :] * scale
        s = s_ref[rows, :] * scale
        for h in range(RET_HEADS_PER_BLOCK):
            lo = h * RET_HEAD_DIM
            mid = lo + RET_HEAD_DIM // 2
            a1 = acc[:, lo:mid]
            a2 = acc[:, mid:lo + RET_HEAD_DIM]
            yield h, lo, a1 * c - a2 * s
            yield h, mid, a1 * s + a2 * c

    @pl.when(part == 0)
    def _():
        for rows, acc in sub_blocks():
            for _, col, val in rotary(rows, acc, 1.0):
                o_ref[rows, col:col + LANES] = val.astype(BF16)

    @pl.when(part == 1)
    def _():
        for rows, acc in sub_blocks():
            for h, col, val in rotary(rows, acc, RET_HEAD_DIM ** -0.5):
                o_ref[rows, col:col + LANES] = val.astype(BF16)
                chunks = val.reshape(sub // RET_CHUNK, RET_CHUNK, LANES) * kd_ref[h][None]
                kdec_ref[rows, col:col + LANES] = chunks.reshape(sub, LANES).astype(BF16)

    @pl.when(part == 2)
    def _():
        for rows, acc in sub_blocks():
            o_ref[rows, :] = acc.astype(BF16)

    @pl.when(part == 3)
    def _():
        for rows, acc in sub_blocks():
            o_ref[rows, :] = (acc * _sigmoid(acc)).astype(BF16)


def _ret_proj(xb, w_in, layer, cos_r, sin_r, k_decay, tm, sub):
    s = xb.shape[0]
    vmem = 2 * (tm * D_MODEL * 2 + D_MODEL * RET_TN * 4 + 2 * tm * RET_TN * 2 + 2 * tm * LANES * 4) \
        + D_MODEL * RET_TN * 2 + (12 << 20)
    k_block = lambda j: jnp.clip(j - RET_BLOCKS_PER_PART, 0, RET_BLOCKS_PER_PART - 1)
    nj = RET_COLS // RET_TN
    slab = _slab_rows(D_MODEL, (s // tm) * nj)
    return pl.pallas_call(
        functools.partial(_ret_proj_kernel, tm=tm, sub=sub),
        grid=(s // tm, nj),
        in_specs=[pl.BlockSpec((tm, D_MODEL), lambda i, j: (i, 0)),
                  pl.BlockSpec((None, D_MODEL, RET_TN), lambda i, j: (layer, 0, j)),
                  pl.BlockSpec((tm, LANES), lambda i, j: (i, 0)),
                  pl.BlockSpec((tm, LANES), lambda i, j: (i, 0)),
                  pl.BlockSpec((RET_HEADS_PER_BLOCK, RET_CHUNK, LANES), lambda i, j: (k_block(j), 0, 0)),
                  pl.BlockSpec((None, slab, RET_COLS + ATT_IN_COLS), lambda i, j: (layer, i * nj + j, 0))],
        out_specs=[pl.BlockSpec((tm, RET_TN), lambda i, j: (i, j)),
                   pl.BlockSpec((tm, RET_TN), lambda i, j: (i, k_block(j))),
                   pl.BlockSpec((slab, ATT_IN_COLS), lambda i, j: (i * nj + j, 0))],
        out_shape=[jax.ShapeDtypeStruct((s, RET_COLS), BF16), jax.ShapeDtypeStruct((s, RET_WIDTH), BF16),
                   jax.ShapeDtypeStruct((D_MODEL, ATT_IN_COLS), BF16)],
        scratch_shapes=[pltpu.VMEM((D_MODEL, RET_TN), BF16)],
        compiler_params=_params(("arbitrary", "arbitrary"), vmem),
        name="ret_proj",
    )(xb, w_in, cos_r, sin_r, k_decay, w_in)


ATT_W_BLOCKS = ATT_IN_COLS // MXU_DIM


def _att_proj_kernel(x_ref, *refs, tm, sub):
    w_refs = refs[:ATT_W_BLOCKS]
    c_ref, sa_ref, sb_ref, o_ref = refs[ATT_W_BLOCKS:]

    for r in range(0, tm, sub):
        rows = slice(r, r + sub)
        c = c_ref[rows, :]
        sa = sa_ref[rows, :]
        sb = sb_ref[rows, :]

        def rot(a):
            return (a * c + pltpu.roll(a, LANES - ATT_HEAD_DIM // 2, 1) * sa
                    + pltpu.roll(a, ATT_HEAD_DIM // 2, 1) * sb)

        x = x_ref[rows, :]
        for wb in range(ATT_W_BLOCKS):
            acc = jnp.dot(x, w_refs[wb][...], preferred_element_type=F32)
            for half in range(MXU_DIM // LANES):
                col = wb * MXU_DIM + half * LANES
                a = acc[:, half * LANES:(half + 1) * LANES]
                if col < ATT_WIDTH:
                    o_ref[rows, col:col + LANES] = (rot(a) * ATT_SCORE_SCALE).astype(BF16)
                elif col < ATT_WIDTH + KV_WIDTH:
                    k = rot(a)
                    o_ref[rows, col:col + LANES] = k.astype(BF16)
                    o_ref[rows, col + 2 * LANES:col + 3 * LANES] = pltpu.roll(k, HALF, 1).astype(BF16)
                else:
                    o_ref[rows, col:col + LANES] = a.astype(BF16)
                    o_ref[rows, col + 2 * LANES:col + 3 * LANES] = pltpu.roll(a, HALF, 1).astype(BF16)


def _att_proj(xb, w_att_b, cos_a, sin_a, sin_b, tm, sub):
    s = xb.shape[0]
    vmem = 2 * (tm * D_MODEL * 2 + D_MODEL * ATT_IN_COLS * 2 + tm * ATT_OUT_COLS * 2 + 3 * tm * LANES * 4) + (12 << 20)
    row = pl.BlockSpec((tm, LANES), lambda i: (i, 0))
    w_specs = [pl.BlockSpec((D_MODEL, MXU_DIM), functools.partial(lambda i, wb: (0, wb), wb=wb))
               for wb in range(ATT_W_BLOCKS)]
    return pl.pallas_call(
        functools.partial(_att_proj_kernel, tm=tm, sub=sub),
        grid=(s // tm,),
        in_specs=[pl.BlockSpec((tm, D_MODEL), lambda i: (i, 0))] + w_specs + [row, row, row],
        out_specs=pl.BlockSpec((tm, ATT_OUT_COLS), lambda i: (i, 0)),
        out_shape=jax.ShapeDtypeStruct((s, ATT_OUT_COLS), BF16),
        compiler_params=_params(("arbitrary",), vmem),
        name="att_proj",
    )(xb, *([w_att_b] * ATT_W_BLOCKS), cos_a, sin_a, sin_b)


def _mixer_kernel(cd_ref, sink_ref, r_ref, kdec_ref, a_ref, dec_ref, qd_ref, g_ref, b_ref, wo_ref, wg_ref, wp_ref,
                  o_ref, wob_ref, wgb_ref, wpb_ref, state_ref, kv_ref, *, tm, layer):
    i = pl.program_id(0)
    wob_ref[...] = wo_ref[...].astype(BF16)
    wgb_ref[...] = wg_ref[...].astype(BF16)
    wpb_ref[...] = wp_ref[...].astype(BF16)
    n_chunks = tm // RET_CHUNK
    kv_cols = 4 * KV_WIDTH

    @pl.when(i == 0)
    def _():
        state_ref[...] = jnp.zeros_like(state_ref)
        kv_ref[0:WINDOW, :] = jnp.zeros((WINDOW, kv_cols), BF16)

    @pl.when(i > 0)
    def _():
        kv_ref[0:WINDOW, :] = kv_ref[tm:tm + WINDOW, :]

    kv_ref[WINDOW:tm + WINDOW, :] = a_ref[:, ATT_WIDTH:ATT_WIDTH + kv_cols]

    contract_last = (((1,), (1,)), ((), ()))
    contract_first = (((0,), (0,)), ((), ()))
    qi = lax.broadcasted_iota(jnp.int32, (WINDOW, 2 * WINDOW), 0)
    kj = lax.broadcasted_iota(jnp.int32, (WINDOW, 2 * WINDOW), 1)
    band = (kj > qi) & (kj <= qi + WINDOW)
    low_lanes = lax.broadcasted_iota(jnp.int32, (2 * WINDOW, LANES), 1) < HALF

    def chunk(c, carry):
        r0 = pl.multiple_of(c * RET_CHUNK, RET_CHUNK)
        rows = pl.ds(r0, RET_CHUNK)

        for h in range(RET_HEADS):
            lo = h * RET_HEAD_DIM
            hi = lo + RET_HEAD_DIM
            q = r_ref[rows, lo:hi]
            k = r_ref[rows, RET_WIDTH + lo:RET_WIDTH + hi]
            v = r_ref[rows, 2 * RET_WIDTH + lo:2 * RET_WIDTH + hi]
            silu_gate = r_ref[rows, 3 * RET_WIDTH + lo:3 * RET_WIDTH + hi].astype(F32)
            st = state_ref[h]
            sc = lax.dot_general(q, k, contract_last, preferred_element_type=F32) * dec_ref[h]
            inner = jnp.dot(sc.astype(BF16), v, preferred_element_type=F32)
            cross = jnp.dot(q, st.astype(BF16), preferred_element_type=F32) * qd_ref[h]
            state_ref[h] = st * cd_ref[h] + lax.dot_general(kdec_ref[rows, lo:hi], v, contract_first,
                                                            preferred_element_type=F32)
            y = inner + cross
            mu = jnp.mean(y, axis=-1, keepdims=True)
            d = y - mu
            var = jnp.mean(d * d, axis=-1, keepdims=True)
            yn = d * lax.rsqrt(var + GN_EPS) * g_ref[:, lo:hi] + b_ref[:, lo:hi]
            o_ref[rows, lo:hi] = (silu_gate * yn).astype(BF16)

        kk = kv_ref[pl.ds(r0, 2 * WINDOW), :]
        k_nat = kk[:, 0:LANES]
        v_nat = kk[:, LANES:2 * LANES]
        k_swp = kk[:, 2 * LANES:3 * LANES]
        v_swp = kk[:, 3 * LANES:4 * LANES]
        zero = jnp.zeros_like(k_nat)
        k_ext = ((jnp.where(low_lanes, k_nat, zero), jnp.where(low_lanes, zero, k_swp)),
                 (jnp.where(low_lanes, k_swp, zero), jnp.where(low_lanes, zero, k_nat)))
        v_ext = ((jnp.where(low_lanes, v_nat, zero), jnp.where(low_lanes, zero, v_swp)),
                 (jnp.where(low_lanes, v_swp, zero), jnp.where(low_lanes, zero, v_nat)))
        first_block = (i * n_chunks + c) == 0
        mask = band & ((kj >= WINDOW) | jnp.logical_not(first_block))
        pairs_per_kv = ATT_GROUP // 2
        for kvh in range(ATT_KV_HEADS):
            for pr in range(pairs_per_kv):
                pair = kvh * pairs_per_kv + pr
                q_pair = a_ref[rows, pair * LANES:(pair + 1) * LANES]
                o_pair = None
                for par in range(2):
                    sink = sink_ref[layer, 2 * pair + par] * LOG2_E
                    sc = lax.dot_general(q_pair, k_ext[kvh][par], contract_last, preferred_element_type=F32)
                    sc = jnp.where(mask, sc, -jnp.inf)
                    m = jnp.maximum(jnp.max(sc, axis=-1, keepdims=True), sink)
                    p = jnp.exp2(sc - m)
                    den = jnp.sum(p, axis=-1, keepdims=True) + jnp.exp2(sink - m)
                    o = jnp.dot(p.astype(BF16), v_ext[kvh][par], preferred_element_type=F32) * (1.0 / den)
                    o_pair = o if o_pair is None else o_pair + o
                o_ref[rows, RET_WIDTH + pair * LANES:RET_WIDTH + (pair + 1) * LANES] = o_pair.astype(BF16)
        return carry

    lax.fori_loop(0, n_chunks, chunk, 0, unroll=4)


def _mixer(rp, kdec, ap, ret_consts, gn_g, gn_b, sinks, w_out, w_gate, w_proj, layer, tm):
    s = rp.shape[0]
    decay, q_decay, chunk_decay = ret_consts
    smem = pl.BlockSpec(memory_space=pltpu.SMEM)
    const3 = lambda shape: pl.BlockSpec(shape, lambda i: (0, 0, 0))
    steps = s // tm
    slab_d, slab_p = _slab_rows(D_MODEL, steps), _slab_rows(PLE_DIM, steps)
    w_in_spec = lambda rows: pl.BlockSpec((None, rows, D_MODEL), lambda i: (layer, i, 0))
    w_out_spec = lambda rows: pl.BlockSpec((rows, D_MODEL), lambda i: (i, 0))
    w_shape = lambda rows: jax.ShapeDtypeStruct((rows, D_MODEL), BF16)
    vmem = 2 * (tm * (RET_COLS + RET_WIDTH) * 2 + tm * ATT_OUT_COLS * 2 + tm * D_MODEL * 2) \
        + 2 * (2 * slab_d + slab_p) * D_MODEL * 6 \
        + 2 * 4 * (decay.size + q_decay.size) \
        + RET_HEADS * RET_HEAD_DIM * RET_HEAD_DIM * 4 + (tm + WINDOW) * 4 * KV_WIDTH * 2 + (14 << 20)
    return pl.pallas_call(
        functools.partial(_mixer_kernel, tm=tm, layer=layer),
        grid=(s // tm,),
        in_specs=[smem, smem,
                  pl.BlockSpec((tm, RET_COLS), lambda i: (i, 0)),
                  pl.BlockSpec((tm, RET_WIDTH), lambda i: (i, 0)),
                  pl.BlockSpec((tm, ATT_OUT_COLS), lambda i: (i, 0)),
                  const3(decay.shape), const3(q_decay.shape),
                  _layer_vec(layer, RET_WIDTH), _layer_vec(layer, RET_WIDTH),
                  w_in_spec(slab_d), w_in_spec(slab_d), w_in_spec(slab_p)],
        out_specs=[pl.BlockSpec((tm, D_MODEL), lambda i: (i, 0)),
                   w_out_spec(slab_d), w_out_spec(slab_d), w_out_spec(slab_p)],
        out_shape=[jax.ShapeDtypeStruct((s, D_MODEL), BF16), w_shape(D_MODEL), w_shape(D_MODEL), w_shape(PLE_DIM)],
        scratch_shapes=[pltpu.VMEM((RET_HEADS, RET_HEAD_DIM, RET_HEAD_DIM), F32),
                        pltpu.VMEM((tm + WINDOW, 4 * KV_WIDTH), BF16)],
        compiler_params=_params(("arbitrary",), vmem),
        name="mixer",
    )(chunk_decay, sinks, rp, kdec, ap, decay, q_decay, gn_g, gn_b, w_out, w_gate, w_proj)


def _retention_constants():
    c = RET_CHUNK
    log_g = jnp.log1p(-jnp.exp2(-5.0 - jnp.arange(RET_HEADS, dtype=F32)))
    idx = jnp.arange(c, dtype=F32)
    diff = idx[:, None] - idx[None, :]
    decay = jnp.where(diff[None] >= 0, jnp.exp(log_g[:, None, None] * jnp.maximum(diff, 0.0)[None]), 0.0)
    q_decay = jnp.exp(log_g[:, None] * (idx[None, :] + 1.0))
    k_decay = jnp.exp(log_g[:, None] * (c - 1.0 - idx[None, :]))
    chunk_decay = jnp.exp(log_g * c)
    wide = lambda t, lanes: jnp.broadcast_to(t[:, :, None], (RET_HEADS, c, lanes))
    return (decay, wide(q_decay, RET_HEAD_DIM), chunk_decay), wide(k_decay, LANES)


def _mix_out_kernel(cat_ref, x_ref, p_ref, wo_ref, wg_ref, wp_ref, g_ref, b_ref, bg_ref, wd_ref,
                    r_ref, xb_ref, wdb_ref, *, tm, sub):
    wdb_ref[...] = wd_ref[...].astype(BF16)
    for r in range(0, tm, sub):
        rows = slice(r, r + sub)
        z = DEEPNORM_ALPHA * x_ref[rows, :] + jnp.dot(cat_ref[rows, :], wo_ref[...], preferred_element_type=F32)
        x1 = _layer_norm(z, g_ref[...], b_ref[...])
        x1b = x1.astype(BF16)
        gate = jnp.dot(x1b, wg_ref[...], preferred_element_type=F32) + bg_ref[...]
        ple = jnp.dot(p_ref[rows, :].astype(BF16), wp_ref[...], preferred_element_type=F32) * _sigmoid(gate)
        r_ref[rows, :] = DEEPNORM_ALPHA * x1 + ple
        xb_ref[rows, :] = x1b


def _mix_out(cat, x, p, layer, w_out_b, w_gate_b, w_proj_b, ln_g, ln_b, b_gate, w_down, tm, sub):
    s = x.shape[0]
    row = lambda cols: pl.BlockSpec((tm, cols), lambda i: (i, 0))
    resident = lambda rows_: pl.BlockSpec((rows_, D_MODEL), lambda i: (0, 0), pipeline_mode=pl.Buffered(1))
    slab = _slab_rows(D_FF, s // tm)
    vmem = (2 * D_MODEL * D_MODEL + PLE_DIM * D_MODEL) * 2 + 2 * slab * D_MODEL * 6 \
        + 2 * tm * (D_MODEL * 2 + D_MODEL * 4 + PLE_DIM * 4 + D_MODEL * 4 + D_MODEL * 2) + (12 << 20)
    return pl.pallas_call(
        functools.partial(_mix_out_kernel, tm=tm, sub=sub),
        grid=(s // tm,),
        in_specs=[row(D_MODEL), row(D_MODEL),
                  pl.BlockSpec((None, tm, PLE_DIM), lambda i: (layer, i, 0)),
                  resident(D_MODEL), resident(D_MODEL), resident(PLE_DIM),
                  _layer_vec(layer, D_MODEL), _layer_vec(layer, D_MODEL), _layer_vec(layer, D_MODEL),
                  pl.BlockSpec((None, slab, D_MODEL), lambda i: (layer, i, 0))],
        out_specs=[row(D_MODEL), row(D_MODEL), pl.BlockSpec((slab, D_MODEL), lambda i: (i, 0))],
        out_shape=[jax.ShapeDtypeStruct((s, D_MODEL), F32), jax.ShapeDtypeStruct((s, D_MODEL), BF16),
                   jax.ShapeDtypeStruct((D_FF, D_MODEL), BF16)],
        compiler_params=_params(("arbitrary",), vmem),
        name="mix_out",
    )(cat, x, p, w_out_b, w_gate_b, w_proj_b, ln_g, ln_b, b_gate, w_down)


FFN_TN = 512
FFN_BLOCKS = D_FF // FFN_TN


def _ffn_up_kernel(x_ref, wg_ref, wu_ref, cwg_ref, cwu_ref, cbg_ref, cbu_ref, o_ref, carry_ref, wgb_ref, wub_ref,
                   *, tm, sub):
    @pl.when(pl.program_id(1) == 0)
    def _():
        wgb_ref[...] = wg_ref[...].astype(BF16)
        wub_ref[...] = wu_ref[...].astype(BF16)
        carry_ref[...] = jnp.zeros_like(carry_ref)

    def conv(prev, h, cw_ref, cb_ref):
        ext = jnp.concatenate([prev, h], axis=0)
        y = (ext * cw_ref[2:3, :] + pltpu.roll(ext, 1, 0) * cw_ref[1:2, :]
             + pltpu.roll(ext, 2, 0) * cw_ref[0:1, :] + cb_ref[...])
        return y[SUBLANES:]

    prev_g = carry_ref[0]
    prev_u = carry_ref[1]
    for r in range(0, tm, sub):
        x = x_ref[r:r + sub, :]
        hg = jnp.dot(x, wgb_ref[...], preferred_element_type=F32)
        hu = jnp.dot(x, wub_ref[...], preferred_element_type=F32)
        yg = conv(prev_g, hg, cwg_ref, cbg_ref)
        yu = conv(prev_u, hu, cwu_ref, cbu_ref)
        prev_g = hg[sub - SUBLANES:sub]
        prev_u = hu[sub - SUBLANES:sub]
        o_ref[r:r + sub, :] = (yg * _sigmoid(yg) * yu).astype(BF16)
    carry_ref[0] = prev_g
    carry_ref[1] = prev_u


def _ffn_up(xb, w_up, conv_w, conv_b, layer, tm, sub):
    s = xb.shape[0]
    nb = FFN_BLOCKS
    vmem = 2 * (tm * D_MODEL * 2 + 2 * D_MODEL * FFN_TN * 4 + tm * FFN_TN * 2) + 2 * D_MODEL * FFN_TN * 2 + (12 << 20)
    wspec = lambda off: pl.BlockSpec((None, D_MODEL, FFN_TN), lambda j, i: (layer, 0, j + off))
    cwspec = lambda off: pl.BlockSpec((None, CONV_WIDTH, FFN_TN), lambda j, i: (layer, 0, j + off))
    cbspec = lambda off: pl.BlockSpec((None, 1, FFN_TN), lambda j, i: (layer, 0, j + off))
    return pl.pallas_call(
        functools.partial(_ffn_up_kernel, tm=tm, sub=sub),
        grid=(nb, s // tm),
        in_specs=[pl.BlockSpec((tm, D_MODEL), lambda j, i: (i, 0)),
                  wspec(0), wspec(nb), cwspec(0), cwspec(nb), cbspec(0), cbspec(nb)],
        out_specs=pl.BlockSpec((tm, FFN_TN), lambda j, i: (i, j)),
        out_shape=jax.ShapeDtypeStruct((s, D_FF), BF16),
        scratch_shapes=[pltpu.VMEM((2, SUBLANES, FFN_TN), F32),
                        pltpu.VMEM((D_MODEL, FFN_TN), BF16), pltpu.VMEM((D_MODEL, FFN_TN), BF16)],
        compiler_params=_params(("arbitrary", "arbitrary"), vmem),
        name="ffn_up",
    )(xb, w_up, w_up, conv_w, conv_w, conv_b, conv_b)


def _ffn_down_kernel(a_ref, r_ref, w_ref, g_ref, b_ref, x_ref, xb_ref, *, tm, sub):
    for r in range(0, tm, sub):
        rows = slice(r, r + sub)
        z = r_ref[rows, :] + jnp.dot(a_ref[rows, :], w_ref[...], preferred_element_type=F32)
        x2 = _layer_norm(z, g_ref[...], b_ref[...])
        x_ref[rows, :] = x2
        xb_ref[rows, :] = x2.astype(BF16)


def _ffn_down(act, r, w_down_b, ln_g, ln_b, layer, tm, sub):
    s = r.shape[0]
    row = lambda cols: pl.BlockSpec((tm, cols), lambda i: (i, 0))
    vmem = D_FF * D_MODEL * 2 + 2 * tm * (D_FF * 2 + D_MODEL * 4 + D_MODEL * 4 + D_MODEL * 2) + (8 << 20)
    return pl.pallas_call(
        functools.partial(_ffn_down_kernel, tm=tm, sub=sub),
        grid=(s // tm,),
        in_specs=[row(D_FF), row(D_MODEL),
                  pl.BlockSpec((D_FF, D_MODEL), lambda i: (0, 0), pipeline_mode=pl.Buffered(1)),
                  _layer_vec(layer, D_MODEL), _layer_vec(layer, D_MODEL)],
        out_specs=[row(D_MODEL), row(D_MODEL)],
        out_shape=[jax.ShapeDtypeStruct((s, D_MODEL), F32), jax.ShapeDtypeStruct((s, D_MODEL), BF16)],
        compiler_params=_params(("arbitrary",), vmem),
        name="ffn_down",
    )(act, r, w_down_b, ln_g, ln_b)


def _tiles(s):
    pick = lambda tm, sub: (min(tm, s), min(sub, s))
    return dict(tables=pick(1024, 1024), ret_proj=pick(2048, 1024), att_proj=pick(1024, 256),
                mixer=pick(1024, 128), mix_out=pick(512, 512), ffn_up=pick(1024, 1024), ffn_down=pick(512, 256))


def kernel(x, p, positions, w_in, w_out, ret_norm_g, ret_norm_b, attn_sinks, ln1_g, ln1_b, w_ffn_up,
           ffn_conv_w, ffn_conv_b, w_ffn_down, ln2_g, ln2_b, w_ple_gate, b_ple_gate, w_ple_proj):
    batch, s, d = x.shape
    assert batch == 1 and d == D_MODEL and s % RET_CHUNK == 0
    t = _tiles(s)
    assert all(s % tm == 0 and tm % sub == 0 for tm, sub in t.values())

    xf = x.reshape(s, d)
    pf = p.reshape(DEPTH, s, PLE_DIM)
    cos_r, sin_r, cos_a, sin_a, sin_b, xb = _rot_tables(positions.reshape(s), xf, t["tables"][0])
    ret_consts, k_decay = _retention_constants()
    vecs = lambda v: v.reshape(DEPTH, 1, -1)
    gn_g, gn_b, g1, b1, g2, b2 = (vecs(v) for v in (ret_norm_g, ret_norm_b, ln1_g, ln1_b, ln2_g, ln2_b))
    bg, conv_b = vecs(b_ple_gate), vecs(ffn_conv_b)

    for l in range(DEPTH):
        rp, kdec, w_att_b = _ret_proj(xb, w_in, l, cos_r, sin_r, k_decay, *t["ret_proj"])
        ap = _att_proj(xb, w_att_b, cos_a, sin_a, sin_b, *t["att_proj"])
        cat, w_out_b, w_gate_b, w_proj_b = _mixer(rp, kdec, ap, ret_consts, gn_g, gn_b, attn_sinks,
                                                  w_out, w_ple_gate, w_ple_proj, l, t["mixer"][0])
        r, xb, w_down_b = _mix_out(cat, xf, pf, l, w_out_b, w_gate_b, w_proj_b, g1, b1, bg, w_ffn_down,
                                   *t["mix_out"])
        act = _ffn_up(xb, w_ffn_up, ffn_conv_w, conv_b, l, *t["ffn_up"])
        xf, xb = _ffn_down(act, r, w_down_b, g2, b2, l, *t["ffn_down"])
    return xf.reshape(batch, s, d)
```

```python
import functools

import jax
import jax.numpy as jnp
from jax import lax
from jax.experimental import pallas as pl
from jax.experimental.pallas import tpu as pltpu

F32 = jnp.float32
BF16 = jnp.bfloat16

D_MODEL = 2048
DEPTH = 4
PLE_DIM = 256
RET_WIDTH = 1024
RET_HEADS = 4
RET_HEAD_DIM = 256
RET_CHUNK = 128
RET_ROT_BASE = 10000.0
ATT_HEAD_DIM = 64
ATT_WIDTH = 1024
ATT_HEADS = 16
ATT_KV_HEADS = 2
ATT_GROUP = ATT_HEADS // ATT_KV_HEADS
KV_WIDTH = 128
WINDOW = 128
ROPE_THETA = 10000.0
D_FF = 5632
CONV_WIDTH = 3
LN_EPS = 1e-5
GN_EPS = 1e-6
DEEPNORM_ALPHA = (2 * DEPTH) ** 0.25

LANES = 128
SUBLANES = 8
BF16_SUBLANE_ROWS = 16
MXU_DIM = 256
VMEM_LIMIT_CAP = 58 * 1024 * 1024

RET_COLS = 4 * RET_WIDTH
ATT_IN_COLS = ATT_WIDTH + 2 * KV_WIDTH
ATT_OUT_COLS = ATT_WIDTH + 4 * KV_WIDTH
HALF = LANES // 2
LOG2_E = 1.4426950408889634
ATT_SCORE_SCALE = ATT_HEAD_DIM ** -0.5 * LOG2_E


def _params(semantics, vmem_bytes):
    return pltpu.CompilerParams(dimension_semantics=semantics,
                                vmem_limit_bytes=int(min(vmem_bytes, VMEM_LIMIT_CAP)))


def _layer_norm(z, g, b):
    mu = jnp.mean(z, axis=-1, keepdims=True)
    d = z - mu
    var = jnp.mean(d * d, axis=-1, keepdims=True)
    return d * lax.rsqrt(var + LN_EPS) * g + b


def _sigmoid(v):
    return 0.5 + 0.5 * jnp.tanh(0.5 * v)


def _silu(v):
    t = 0.5 * v
    return t + t * jnp.tanh(t)


def _slab_rows(rows, steps):
    assert rows % steps == 0 and (rows // steps) % BF16_SUBLANE_ROWS == 0, (rows, steps)
    return rows // steps


def _layer_vec(layer, cols):
    return pl.BlockSpec((None, 1, cols), lambda *_: (layer, 0, 0))


def _rot_tables_kernel(pos_ref, fr_ref, fa_ref, x_ref, cr_ref, sr_ref, ca_ref, sa_ref, sb_ref, xb_ref):
    xb_ref[...] = x_ref[...].astype(BF16)
    pos = pos_ref[...].astype(F32)
    ang_r = pos * fr_ref[...]
    cr_ref[...] = jnp.cos(ang_r)
    sr_ref[...] = jnp.sin(ang_r)
    ang_a = pos * fa_ref[...]
    c = jnp.cos(ang_a)
    s = jnp.sin(ang_a)
    lane = lax.broadcasted_iota(jnp.int32, c.shape, 1)
    first_half = (lane & (ATT_HEAD_DIM // 2)) == 0
    ca_ref[...] = c
    sa_ref[...] = jnp.where(first_half, -s, 0.0)
    sb_ref[...] = jnp.where(first_half, 0.0, s)


def _rot_tables(positions, xf, tm):
    s = positions.shape[0]
    ret_inv_freq = 1.0 / (RET_ROT_BASE ** jnp.linspace(0.0, 1.0, RET_HEAD_DIM // 2, dtype=F32))
    att_inv_freq = ROPE_THETA ** (-jnp.arange(0, ATT_HEAD_DIM, 2, dtype=F32) / ATT_HEAD_DIM)
    fr = ret_inv_freq.reshape(1, LANES)
    fa = jnp.tile(att_inv_freq, LANES // (ATT_HEAD_DIM // 2)).reshape(1, LANES)
    tab = jax.ShapeDtypeStruct((s, LANES), F32)
    row = pl.BlockSpec((tm, LANES), lambda i: (i, 0))
    const = pl.BlockSpec((1, LANES), lambda i: (0, 0))
    wide = pl.BlockSpec((tm, D_MODEL), lambda i: (i, 0))
    return pl.pallas_call(
        _rot_tables_kernel,
        grid=(s // tm,),
        in_specs=[pl.BlockSpec((tm, 1), lambda i: (i, 0)), const, const, wide],
        out_specs=[row] * 5 + [wide],
        out_shape=[tab] * 5 + [jax.ShapeDtypeStruct((s, D_MODEL), BF16)],
        compiler_params=_params(("arbitrary",), 2 * tm * D_MODEL * 6 + (16 << 20)),
        name="rot_tables",
    )(positions.reshape(s, 1), fr, fa, xf)


RET_TN = 512
RET_BLOCKS_PER_PART = RET_WIDTH // RET_TN
RET_HEADS_PER_BLOCK = RET_TN // RET_HEAD_DIM


def _ret_proj_kernel(x_ref, w_ref, c_ref, s_ref, kd_ref, wrow_ref, o_ref, kdec_ref, watt_ref, wb_ref, *, tm, sub):
    j = pl.program_id(1)
    part = j // RET_BLOCKS_PER_PART
    watt_ref[...] = wrow_ref[:, RET_COLS:].astype(BF16)

    def sub_blocks():
        wb_ref[...] = w_ref[...].astype(BF16)
        for r in range(0, tm, sub):
            rows = slice(r, r + sub)
            yield rows, jnp.dot(x_ref[rows, :], wb_ref[...], preferred_element_type=F32)

    def rotary(rows, acc, scale):
        c = c_ref[rows, :] * scale
        s = s_ref[rows, :] * scale
        for h in range(RET_HEADS_PER_BLOCK):
            lo = h * RET_HEAD_DIM
            mid = lo + RET_HEAD_DIM // 2
            a1 = acc[:, lo:mid]
            a2 = acc[:, mid:lo + RET_HEAD_DIM]
            yield h, lo, a1 * c - a2 * s
            yield h, mid, a1 * s + a2 * c

    @pl.when(part == 0)
    def _():
        for rows, acc in sub_blocks():
            for _, col, val in rotary(rows, acc, 1.0):
                o_ref[rows, col:col + LANES] = val.astype(BF16)

    @pl.when(part == 1)
    def _():
        for rows, acc in sub_blocks():
            for h, col, val in rotary(rows, acc, RET_HEAD_DIM ** -0.5):
                o_ref[rows, col:col + LANES] = val.astype(BF16)
                chunks = val.reshape(sub // RET_CHUNK, RET_CHUNK, LANES) * kd_ref[h][None]
                kdec_ref[rows, col:col + LANES] = chunks.reshape(sub, LANES).astype(BF16)

    @pl.when(part == 2)
    def _():
        for rows, acc in sub_blocks():
            o_ref[rows, :] = acc.astype(BF16)

    @pl.when(part == 3)
    def _():
        for rows, acc in sub_blocks():
            o_ref[rows, :] = _silu(acc).astype(BF16)


def _ret_proj(xb, w_in, layer, cos_r, sin_r, k_decay, tm, sub):
    s = xb.shape[0]
    vmem = 2 * (tm * D_MODEL * 2 + D_MODEL * RET_TN * 4 + 2 * tm * RET_TN * 2 + 2 * tm * LANES * 4) \
        + D_MODEL * RET_TN * 2 + (12 << 20)
    k_block = lambda j: jnp.clip(j - RET_BLOCKS_PER_PART, 0, RET_BLOCKS_PER_PART - 1)
    nj = RET_COLS // RET_TN
    slab = _slab_rows(D_MODEL, (s // tm) * nj)
    return pl.pallas_call(
        functools.partial(_ret_proj_kernel, tm=tm, sub=sub),
        grid=(s // tm, nj),
        in_specs=[pl.BlockSpec((tm, D_MODEL), lambda i, j: (i, 0)),
                  pl.BlockSpec((None, D_MODEL, RET_TN), lambda i, j: (layer, 0, j)),
                  pl.BlockSpec((tm, LANES), lambda i, j: (i, 0)),
                  pl.BlockSpec((tm, LANES), lambda i, j: (i, 0)),
                  pl.BlockSpec((RET_HEADS_PER_BLOCK, RET_CHUNK, LANES), lambda i, j: (k_block(j), 0, 0)),
                  pl.BlockSpec((None, slab, RET_COLS + ATT_IN_COLS), lambda i, j: (layer, i * nj + j, 0))],
        out_specs=[pl.BlockSpec((tm, RET_TN), lambda i, j: (i, j)),
                   pl.BlockSpec((tm, RET_TN), lambda i, j: (i, k_block(j))),
                   pl.BlockSpec((slab, ATT_IN_COLS), lambda i, j: (i * nj + j, 0))],
        out_shape=[jax.ShapeDtypeStruct((s, RET_COLS), BF16), jax.ShapeDtypeStruct((s, RET_WIDTH), BF16),
                   jax.ShapeDtypeStruct((D_MODEL, ATT_IN_COLS), BF16)],
        scratch_shapes=[pltpu.VMEM((D_MODEL, RET_TN), BF16)],
        compiler_params=_params(("arbitrary", "arbitrary"), vmem),
        name="ret_proj",
    )(xb, w_in, cos_r, sin_r, k_decay, w_in)


ATT_W_BLOCKS = ATT_IN_COLS // MXU_DIM


def _att_proj_kernel(x_ref, *refs, tm, sub):
    w_refs = refs[:ATT_W_BLOCKS]
    c_ref, sa_ref, sb_ref, o_ref = refs[ATT_W_BLOCKS:]

    for r in range(0, tm, sub):
        rows = slice(r, r + sub)
        c = c_ref[rows, :]
        sa = sa_ref[rows, :]
        sb = sb_ref[rows, :]

        def rot(a):
            return (a * c + pltpu.roll(a, LANES - ATT_HEAD_DIM // 2, 1) * sa
                    + pltpu.roll(a, ATT_HEAD_DIM // 2, 1) * sb)

        x = x_ref[rows, :]
        for wb in range(ATT_W_BLOCKS):
            acc = jnp.dot(x, w_refs[wb][...], preferred_element_type=F32)
            for half in range(MXU_DIM // LANES):
                col = wb * MXU_DIM + half * LANES
                a = acc[:, half * LANES:(half + 1) * LANES]
                if col < ATT_WIDTH:
                    o_ref[rows, col:col + LANES] = (rot(a) * ATT_SCORE_SCALE).astype(BF16)
                elif col < ATT_WIDTH + KV_WIDTH:
                    k = rot(a)
                    o_ref[rows, col:col + LANES] = k.astype(BF16)
                    o_ref[rows, col + 2 * LANES:col + 3 * LANES] = pltpu.roll(k, HALF, 1).astype(BF16)
                else:
                    o_ref[rows, col:col + LANES] = a.astype(BF16)
                    o_ref[rows, col + 2 * LANES:col + 3 * LANES] = pltpu.roll(a, HALF, 1).astype(BF16)


def _att_proj(xb, w_att_b, cos_a, sin_a, sin_b, tm, sub):
    s = xb.shape[0]
    vmem = 2 * (tm * D_MODEL * 2 + D_MODEL * ATT_IN_COLS * 2 + tm * ATT_OUT_COLS * 2 + 3 * tm * LANES * 4) + (12 << 20)
    row = pl.BlockSpec((tm, LANES), lambda i: (i, 0))
    w_specs = [pl.BlockSpec((D_MODEL, MXU_DIM), functools.partial(lambda i, wb: (0, wb), wb=wb))
               for wb in range(ATT_W_BLOCKS)]
    return pl.pallas_call(
        functools.partial(_att_proj_kernel, tm=tm, sub=sub),
        grid=(s // tm,),
        in_specs=[pl.BlockSpec((tm, D_MODEL), lambda i: (i, 0))] + w_specs + [row, row, row],
        out_specs=pl.BlockSpec((tm, ATT_OUT_COLS), lambda i: (i, 0)),
        out_shape=jax.ShapeDtypeStruct((s, ATT_OUT_COLS), BF16),
        compiler_params=_params(("arbitrary",), vmem),
        name="att_proj",
    )(xb, *([w_att_b] * ATT_W_BLOCKS), cos_a, sin_a, sin_b)


def _mixer_kernel(cd_ref, sink_ref, r_ref, kdec_ref, a_ref, dec_ref, qd_ref, g_ref, b_ref, wo_ref, wg_ref, wp_ref,
                  o_ref, wob_ref, wgb_ref, wpb_ref, state_ref, kv_ref, *, tm, layer):
    i = pl.program_id(0)
    wob_ref[...] = wo_ref[...].astype(BF16)
    wgb_ref[...] = wg_ref[...].astype(BF16)
    wpb_ref[...] = wp_ref[...].astype(BF16)
    n_chunks = tm // RET_CHUNK
    kv_cols = 4 * KV_WIDTH

    @pl.when(i == 0)
    def _():
        state_ref[...] = jnp.zeros_like(state_ref)
        kv_ref[0:WINDOW, :] = jnp.zeros((WINDOW, kv_cols), BF16)

    @pl.when(i > 0)
    def _():
        kv_ref[0:WINDOW, :] = kv_ref[tm:tm + WINDOW, :]

    kv_ref[WINDOW:tm + WINDOW, :] = a_ref[:, ATT_WIDTH:ATT_WIDTH + kv_cols]

    contract_last = (((1,), (1,)), ((), ()))
    contract_first = (((0,), (0,)), ((), ()))
    qi = lax.broadcasted_iota(jnp.int32, (WINDOW, 2 * WINDOW), 0)
    kj = lax.broadcasted_iota(jnp.int32, (WINDOW, 2 * WINDOW), 1)
    band = (kj > qi) & (kj <= qi + WINDOW)
    low_lanes = lax.broadcasted_iota(jnp.int32, (2 * WINDOW, LANES), 1) < HALF

    def chunk(c, carry):
        r0 = pl.multiple_of(c * RET_CHUNK, RET_CHUNK)
        rows = pl.ds(r0, RET_CHUNK)

        for h in range(RET_HEADS):
            lo = h * RET_HEAD_DIM
            hi = lo + RET_HEAD_DIM
            q = r_ref[rows, lo:hi]
            k = r_ref[rows, RET_WIDTH + lo:RET_WIDTH + hi]
            v = r_ref[rows, 2 * RET_WIDTH + lo:2 * RET_WIDTH + hi]
            silu_gate = r_ref[rows, 3 * RET_WIDTH + lo:3 * RET_WIDTH + hi].astype(F32)
            st = state_ref[h]
            sc = lax.dot_general(q, k, contract_last, preferred_element_type=F32) * dec_ref[h]
            inner = jnp.dot(sc.astype(BF16), v, preferred_element_type=F32)
            cross = jnp.dot(q, st.astype(BF16), preferred_element_type=F32) * qd_ref[h]
            state_ref[h] = st * cd_ref[h] + lax.dot_general(kdec_ref[rows, lo:hi], v, contract_first,
                                                            preferred_element_type=F32)
            y = inner + cross
            mu = jnp.mean(y, axis=-1, keepdims=True)
            d = y - mu
            var = jnp.mean(d * d, axis=-1, keepdims=True)
            yn = d * lax.rsqrt(var + GN_EPS) * g_ref[:, lo:hi] + b_ref[:, lo:hi]
            o_ref[rows, lo:hi] = (silu_gate * yn).astype(BF16)

        kk = kv_ref[pl.ds(r0, 2 * WINDOW), :]
        k_nat = kk[:, 0:LANES]
        v_nat = kk[:, LANES:2 * LANES]
        k_swp = kk[:, 2 * LANES:3 * LANES]
        v_swp = kk[:, 3 * LANES:4 * LANES]
        zero = jnp.zeros_like(k_nat)
        k_ext = ((jnp.where(low_lanes, k_nat, zero), jnp.where(low_lanes, zero, k_swp)),
                 (jnp.where(low_lanes, k_swp, zero), jnp.where(low_lanes, zero, k_nat)))
        v_ext = ((jnp.where(low_lanes, v_nat, zero), jnp.where(low_lanes, zero, v_swp)),
                 (jnp.where(low_lanes, v_swp, zero), jnp.where(low_lanes, zero, v_nat)))
        first_block = (i * n_chunks + c) == 0
        mask = band & ((kj >= WINDOW) | jnp.logical_not(first_block))
        pairs_per_kv = ATT_GROUP // 2
        for kvh in range(ATT_KV_HEADS):
            for pr in range(pairs_per_kv):
                pair = kvh * pairs_per_kv + pr
                q_pair = a_ref[rows, pair * LANES:(pair + 1) * LANES]
                o_pair = None
                for par in range(2):
                    sink = sink_ref[layer, 2 * pair + par] * LOG2_E
                    sc = lax.dot_general(q_pair, k_ext[kvh][par], contract_last, preferred_element_type=F32)
                    sc = jnp.where(mask, sc, -jnp.inf)
                    m = jnp.maximum(jnp.max(sc, axis=-1, keepdims=True), sink)
                    p = jnp.exp2(sc - m)
                    den = jnp.sum(p, axis=-1, keepdims=True) + jnp.exp2(sink - m)
                    o = jnp.dot(p.astype(BF16), v_ext[kvh][par], preferred_element_type=F32) * (1.0 / den)
                    o_pair = o if o_pair is None else o_pair + o
                o_ref[rows, RET_WIDTH + pair * LANES:RET_WIDTH + (pair + 1) * LANES] = o_pair.astype(BF16)
        return carry

    lax.fori_loop(0, n_chunks, chunk, 0, unroll=4)


def _mixer(rp, kdec, ap, ret_consts, gn_g, gn_b, sinks, w_out, w_gate, w_proj, layer, tm):
    s = rp.shape[0]
    decay, q_decay, chunk_decay = ret_consts
    smem = pl.BlockSpec(memory_space=pltpu.SMEM)
    const3 = lambda shape: pl.BlockSpec(shape, lambda i: (0, 0, 0))
    steps = s // tm
    slab_d, slab_p = _slab_rows(D_MODEL, steps), _slab_rows(PLE_DIM, steps)
    w_in_spec = lambda rows: pl.BlockSpec((None, rows, D_MODEL), lambda i: (layer, i, 0))
    w_out_spec = lambda rows: pl.BlockSpec((rows, D_MODEL), lambda i: (i, 0))
    w_shape = lambda rows: jax.ShapeDtypeStruct((rows, D_MODEL), BF16)
    vmem = 2 * (tm * (RET_COLS + RET_WIDTH) * 2 + tm * ATT_OUT_COLS * 2 + tm * D_MODEL * 2) \
        + 2 * (2 * slab_d + slab_p) * D_MODEL * 6 \
        + 2 * 4 * (decay.size + q_decay.size) \
        + RET_HEADS * RET_HEAD_DIM * RET_HEAD_DIM * 4 + (tm + WINDOW) * 4 * KV_WIDTH * 2 + (14 << 20)
    return pl.pallas_call(
        functools.partial(_mixer_kernel, tm=tm, layer=layer),
        grid=(s // tm,),
        in_specs=[smem, smem,
                  pl.BlockSpec((tm, RET_COLS), lambda i: (i, 0)),
                  pl.BlockSpec((tm, RET_WIDTH), lambda i: (i, 0)),
                  pl.BlockSpec((tm, ATT_OUT_COLS), lambda i: (i, 0)),
                  const3(decay.shape), const3(q_decay.shape),
                  _layer_vec(layer, RET_WIDTH), _layer_vec(layer, RET_WIDTH),
                  w_in_spec(slab_d), w_in_spec(slab_d), w_in_spec(slab_p)],
        out_specs=[pl.BlockSpec((tm, D_MODEL), lambda i: (i, 0)),
                   w_out_spec(slab_d), w_out_spec(slab_d), w_out_spec(slab_p)],
        out_shape=[jax.ShapeDtypeStruct((s, D_MODEL), BF16), w_shape(D_MODEL), w_shape(D_MODEL), w_shape(PLE_DIM)],
        scratch_shapes=[pltpu.VMEM((RET_HEADS, RET_HEAD_DIM, RET_HEAD_DIM), F32),
                        pltpu.VMEM((tm + WINDOW, 4 * KV_WIDTH), BF16)],
        compiler_params=_params(("arbitrary",), vmem),
        name="mixer",
    )(chunk_decay, sinks, rp, kdec, ap, decay, q_decay, gn_g, gn_b, w_out, w_gate, w_proj)


def _retention_constants():
    c = RET_CHUNK
    log_g = jnp.log1p(-jnp.exp2(-5.0 - jnp.arange(RET_HEADS, dtype=F32)))
    idx = jnp.arange(c, dtype=F32)
    diff = idx[:, None] - idx[None, :]
    decay = jnp.where(diff[None] >= 0, jnp.exp(log_g[:, None, None] * jnp.maximum(diff, 0.0)[None]), 0.0)
    q_decay = jnp.exp(log_g[:, None] * (idx[None, :] + 1.0))
    k_decay = jnp.exp(log_g[:, None] * (c - 1.0 - idx[None, :]))
    chunk_decay = jnp.exp(log_g * c)
    wide = lambda t, lanes: jnp.broadcast_to(t[:, :, None], (RET_HEADS, c, lanes))
    return (decay, wide(q_decay, RET_HEAD_DIM), chunk_decay), wide(k_decay, LANES)


def _mix_out_kernel(cat_ref, x_ref, p_ref, wo_ref, wg_ref, wp_ref, g_ref, b_ref, bg_ref, wd_ref,
                    r_ref, xb_ref, wdb_ref, *, tm, sub):
    wdb_ref[...] = wd_ref[...].astype(BF16)
    for r in range(0, tm, sub):
        rows = slice(r, r + sub)
        z = DEEPNORM_ALPHA * x_ref[rows, :] + jnp.dot(cat_ref[rows, :], wo_ref[...], preferred_element_type=F32)
        x1 = _layer_norm(z, g_ref[...], b_ref[...])
        x1b = x1.astype(BF16)
        gate = jnp.dot(x1b, wg_ref[...], preferred_element_type=F32) + bg_ref[...]
        ple = jnp.dot(p_ref[rows, :].astype(BF16), wp_ref[...], preferred_element_type=F32) * _sigmoid(gate)
        r_ref[rows, :] = DEEPNORM_ALPHA * x1 + ple
        xb_ref[rows, :] = x1b


def _mix_out(cat, x, p, layer, w_out_b, w_gate_b, w_proj_b, ln_g, ln_b, b_gate, w_down, tm, sub):
    s = x.shape[0]
    row = lambda cols: pl.BlockSpec((tm, cols), lambda i: (i, 0))
    resident = lambda rows_: pl.BlockSpec((rows_, D_MODEL), lambda i: (0, 0), pipeline_mode=pl.Buffered(1))
    slab = _slab_rows(D_FF, s // tm)
    vmem = (2 * D_MODEL * D_MODEL + PLE_DIM * D_MODEL) * 2 + 2 * slab * D_MODEL * 6 \
        + 2 * tm * (D_MODEL * 2 + D_MODEL * 4 + PLE_DIM * 4 + D_MODEL * 4 + D_MODEL * 2) + (12 << 20)
    return pl.pallas_call(
        functools.partial(_mix_out_kernel, tm=tm, sub=sub),
        grid=(s // tm,),
        in_specs=[row(D_MODEL), row(D_MODEL),
                  pl.BlockSpec((None, tm, PLE_DIM), lambda i: (layer, i, 0)),
                  resident(D_MODEL), resident(D_MODEL), resident(PLE_DIM),
                  _layer_vec(layer, D_MODEL), _layer_vec(layer, D_MODEL), _layer_vec(layer, D_MODEL),
                  pl.BlockSpec((None, slab, D_MODEL), lambda i: (layer, i, 0))],
        out_specs=[row(D_MODEL), row(D_MODEL), pl.BlockSpec((slab, D_MODEL), lambda i: (i, 0))],
        out_shape=[jax.ShapeDtypeStruct((s, D_MODEL), F32), jax.ShapeDtypeStruct((s, D_MODEL), BF16),
                   jax.ShapeDtypeStruct((D_FF, D_MODEL), BF16)],
        compiler_params=_params(("arbitrary",), vmem),
        name="mix_out",
    )(cat, x, p, w_out_b, w_gate_b, w_proj_b, ln_g, ln_b, b_gate, w_down)


FFN_TN = 512
FFN_BLOCKS = D_FF // FFN_TN


def _ffn_up_kernel(x_ref, wg_ref, wu_ref, cwg_ref, cwu_ref, cbg_ref, cbu_ref, o_ref, carry_ref, wgb_ref, wub_ref,
                   *, tm, sub):
    @pl.when(pl.program_id(1) == 0)
    def _():
        wgb_ref[...] = wg_ref[...].astype(BF16)
        wub_ref[...] = wu_ref[...].astype(BF16)
        carry_ref[...] = jnp.zeros_like(carry_ref)

    def conv(prev, h, cw_ref, cb_ref):
        ext = jnp.concatenate([prev, h], axis=0)
        y = (ext * cw_ref[2:3, :] + pltpu.roll(ext, 1, 0) * cw_ref[1:2, :]
             + pltpu.roll(ext, 2, 0) * cw_ref[0:1, :] + cb_ref[...])
        return y[SUBLANES:]

    prev_g = carry_ref[0]
    prev_u = carry_ref[1]
    for r in range(0, tm, sub):
        x = x_ref[r:r + sub, :]
        hg = jnp.dot(x, wgb_ref[...], preferred_element_type=F32)
        hu = jnp.dot(x, wub_ref[...], preferred_element_type=F32)
        yg = conv(prev_g, hg, cwg_ref, cbg_ref)
        yu = conv(prev_u, hu, cwu_ref, cbu_ref)
        prev_g = hg[sub - SUBLANES:sub]
        prev_u = hu[sub - SUBLANES:sub]
        o_ref[r:r + sub, :] = (_silu(yg) * yu).astype(BF16)
    carry_ref[0] = prev_g
    carry_ref[1] = prev_u


def _ffn_up(xb, w_up, conv_w, conv_b, layer, tm, sub):
    s = xb.shape[0]
    nb = FFN_BLOCKS
    vmem = 2 * (tm * D_MODEL * 2 + 2 * D_MODEL * FFN_TN * 4 + tm * FFN_TN * 2) + 2 * D_MODEL * FFN_TN * 2 + (12 << 20)
    wspec = lambda off: pl.BlockSpec((None, D_MODEL, FFN_TN), lambda j, i: (layer, 0, j + off))
    cwspec = lambda off: pl.BlockSpec((None, CONV_WIDTH, FFN_TN), lambda j, i: (layer, 0, j + off))
    cbspec = lambda off: pl.BlockSpec((None, 1, FFN_TN), lambda j, i: (layer, 0, j + off))
    return pl.pallas_call(
        functools.partial(_ffn_up_kernel, tm=tm, sub=sub),
        grid=(nb, s // tm),
        in_specs=[pl.BlockSpec((tm, D_MODEL), lambda j, i: (i, 0)),
                  wspec(0), wspec(nb), cwspec(0), cwspec(nb), cbspec(0), cbspec(nb)],
        out_specs=pl.BlockSpec((tm, FFN_TN), lambda j, i: (i, j)),
        out_shape=jax.ShapeDtypeStruct((s, D_FF), BF16),
        scratch_shapes=[pltpu.VMEM((2, SUBLANES, FFN_TN), F32),
                        pltpu.VMEM((D_MODEL, FFN_TN), BF16), pltpu.VMEM((D_MODEL, FFN_TN), BF16)],
        compiler_params=_params(("arbitrary", "arbitrary"), vmem),
        name="ffn_up",
    )(xb, w_up, w_up, conv_w, conv_w, conv_b, conv_b)


def _ffn_down_kernel(a_ref, r_ref, w_ref, g_ref, b_ref, x_ref, xb_ref, *, tm, sub):
    for r in range(0, tm, sub):
        rows = slice(r, r + sub)
        z = r_ref[rows, :] + jnp.dot(a_ref[rows, :], w_ref[...], preferred_element_type=F32)
        x2 = _layer_norm(z, g_ref[...], b_ref[...])
        x_ref[rows, :] = x2
        xb_ref[rows, :] = x2.astype(BF16)


def _ffn_down(act, r, w_down_b, ln_g, ln_b, layer, tm, sub):
    s = r.shape[0]
    row = lambda cols: pl.BlockSpec((tm, cols), lambda i: (i, 0))
    vmem = D_FF * D_MODEL * 2 + 2 * tm * (D_FF * 2 + D_MODEL * 4 + D_MODEL * 4 + D_MODEL * 2) + (8 << 20)
    return pl.pallas_call(
        functools.partial(_ffn_down_kernel, tm=tm, sub=sub),
        grid=(s // tm,),
        in_specs=[row(D_FF), row(D_MODEL),
                  pl.BlockSpec((D_FF, D_MODEL), lambda i: (0, 0), pipeline_mode=pl.Buffered(1)),
                  _layer_vec(layer, D_MODEL), _layer_vec(layer, D_MODEL)],
        out_specs=[row(D_MODEL), row(D_MODEL)],
        out_shape=[jax.ShapeDtypeStruct((s, D_MODEL), F32), jax.ShapeDtypeStruct((s, D_MODEL), BF16)],
        compiler_params=_params(("arbitrary",), vmem),
        name="ffn_down",
    )(act, r, w_down_b, ln_g, ln_b)


def _tiles(s):
    pick = lambda tm, sub: (min(tm, s), min(sub, s))
    return dict(tables=pick(1024, 1024), ret_proj=pick(2048, 1024), att_proj=pick(1024, 256),
                mixer=pick(1024, 128), mix_out=pick(512, 512), ffn_up=pick(1024, 1024), ffn_down=pick(512, 256))


def kernel(x, p, positions, w_in, w_out, ret_norm_g, ret_norm_b, attn_sinks, ln1_g, ln1_b, w_ffn_up,
           ffn_conv_w, ffn_conv_b, w_ffn_down, ln2_g, ln2_b, w_ple_gate, b_ple_gate, w_ple_proj):
    batch, s, d = x.shape
    assert batch == 1 and d == D_MODEL and s % RET_CHUNK == 0
    t = _tiles(s)
    assert all(s % tm == 0 and tm % sub == 0 for tm, sub in t.values())

    xf = x.reshape(s, d)
    pf = p.reshape(DEPTH, s, PLE_DIM)
    cos_r, sin_r, cos_a, sin_a, sin_b, xb = _rot_tables(positions.reshape(s), xf, t["tables"][0])
    ret_consts, k_decay = _retention_constants()
    vecs = lambda v: v.reshape(DEPTH, 1, -1)
    gn_g, gn_b, g1, b1, g2, b2 = (vecs(v) for v in (ret_norm_g, ret_norm_b, ln1_g, ln1_b, ln2_g, ln2_b))
    bg, conv_b = vecs(b_ple_gate), vecs(ffn_conv_b)

    for l in range(DEPTH):
        rp, kdec, w_att_b = _ret_proj(xb, w_in, l, cos_r, sin_r, k_decay, *t["ret_proj"])
        ap = _att_proj(xb, w_att_b, cos_a, sin_a, sin_b, *t["att_proj"])
        cat, w_out_b, w_gate_b, w_proj_b = _mixer(rp, kdec, ap, ret_consts, gn_g, gn_b, attn_sinks,
                                                  w_out, w_ple_gate, w_ple_proj, l, t["mixer"][0])
        r, xb, w_down_b = _mix_out(cat, xf, pf, l, w_out_b, w_gate_b, w_proj_b, g1, b1, bg, w_ffn_down,
                                   *t["mix_out"])
        act = _ffn_up(xb, w_ffn_up, ffn_conv_w, conv_b, l, *t["ffn_up"])
        xf, xb = _ffn_down(act, r, w_down_b, g2, b2, l, *t["ffn_down"])
    return xf.reshape(batch, s, d)
```

```python
import functools

import jax
import jax.numpy as jnp
from jax import lax
from jax.experimental import pallas as pl
from jax.experimental.pallas import tpu as pltpu

F32 = jnp.float32
BF16 = jnp.bfloat16

D_MODEL = 2048
DEPTH = 4
PLE_DIM = 256
RET_WIDTH = 1024
RET_HEADS = 4
RET_HEAD_DIM = 256
RET_CHUNK = 128
RET_ROT_BASE = 10000.0
ATT_HEAD_DIM = 64
ATT_WIDTH = 1024
ATT_HEADS = 16
ATT_KV_HEADS = 2
ATT_GROUP = ATT_HEADS // ATT_KV_HEADS
KV_WIDTH = 128
WINDOW = 128
ROPE_THETA = 10000.0
D_FF = 5632
CONV_WIDTH = 3
LN_EPS = 1e-5
GN_EPS = 1e-6
DEEPNORM_ALPHA = (2 * DEPTH) ** 0.25

LANES = 128
SUBLANES = 8
BF16_SUBLANE_ROWS = 16
MXU_DIM = 256
MIB = 1 << 20
VMEM_LIMIT_CAP = 58 * MIB
COMPILER_VMEM = dict(rot_tables=16 * MIB, ret_proj=12 * MIB, att_proj=12 * MIB, mixer=14 * MIB, mix_out=12 * MIB,
                     ffn_up=12 * MIB, ffn_down=8 * MIB)

RET_COLS = 4 * RET_WIDTH
ATT_IN_COLS = ATT_WIDTH + 2 * KV_WIDTH
ATT_OUT_COLS = ATT_WIDTH + 4 * KV_WIDTH
HALF = LANES // 2
LOG2_E = 1.4426950408889634
ATT_SCORE_SCALE = ATT_HEAD_DIM ** -0.5 * LOG2_E


def _params(semantics, vmem_bytes):
    return pltpu.CompilerParams(dimension_semantics=semantics,
                                vmem_limit_bytes=int(min(vmem_bytes, VMEM_LIMIT_CAP)))


def _layer_norm(z, g, b):
    mu = jnp.mean(z, axis=-1, keepdims=True)
    d = z - mu
    var = jnp.mean(d * d, axis=-1, keepdims=True)
    return d * lax.rsqrt(var + LN_EPS) * g + b


def _sigmoid(v):
    return 0.5 + 0.5 * jnp.tanh(0.5 * v)


def _silu(v):
    t = 0.5 * v
    return t + t * jnp.tanh(t)


def _slab_rows(rows, steps):
    assert rows % steps == 0 and (rows // steps) % BF16_SUBLANE_ROWS == 0, (rows, steps)
    return rows // steps


def _layer_vec(layer, cols):
    return pl.BlockSpec((None, 1, cols), lambda *_: (layer, 0, 0))


def _rot_tables_kernel(pos_ref, fr_ref, fa_ref, x_ref, cr_ref, sr_ref, ca_ref, sa_ref, sb_ref, xb_ref):
    xb_ref[...] = x_ref[...].astype(BF16)
    pos = pos_ref[...].astype(F32)
    ang_r = pos * fr_ref[...]
    cr_ref[...] = jnp.cos(ang_r)
    sr_ref[...] = jnp.sin(ang_r)
    ang_a = pos * fa_ref[...]
    c = jnp.cos(ang_a)
    s = jnp.sin(ang_a)
    lane = lax.broadcasted_iota(jnp.int32, c.shape, 1)
    first_half = (lane & (ATT_HEAD_DIM // 2)) == 0
    ca_ref[...] = c
    sa_ref[...] = jnp.where(first_half, -s, 0.0)
    sb_ref[...] = jnp.where(first_half, 0.0, s)


def _rot_tables(positions, xf, tm):
    s = positions.shape[0]
    ret_inv_freq = 1.0 / (RET_ROT_BASE ** jnp.linspace(0.0, 1.0, RET_HEAD_DIM // 2, dtype=F32))
    att_inv_freq = ROPE_THETA ** (-jnp.arange(0, ATT_HEAD_DIM, 2, dtype=F32) / ATT_HEAD_DIM)
    fr = ret_inv_freq.reshape(1, LANES)
    fa = jnp.tile(att_inv_freq, LANES // (ATT_HEAD_DIM // 2)).reshape(1, LANES)
    tab = jax.ShapeDtypeStruct((s, LANES), F32)
    row = pl.BlockSpec((tm, LANES), lambda i: (i, 0))
    const = pl.BlockSpec((1, LANES), lambda i: (0, 0))
    wide = pl.BlockSpec((tm, D_MODEL), lambda i: (i, 0))
    return pl.pallas_call(
        _rot_tables_kernel,
        grid=(s // tm,),
        in_specs=[pl.BlockSpec((tm, 1), lambda i: (i, 0)), const, const, wide],
        out_specs=[row] * 5 + [wide],
        out_shape=[tab] * 5 + [jax.ShapeDtypeStruct((s, D_MODEL), BF16)],
        compiler_params=_params(("arbitrary",), 2 * tm * D_MODEL * 6 + COMPILER_VMEM["rot_tables"]),
        name="rot_tables",
    )(positions.reshape(s, 1), fr, fa, xf)


RET_TN = 512
RET_BLOCKS_PER_PART = RET_WIDTH // RET_TN
RET_HEADS_PER_BLOCK = RET_TN // RET_HEAD_DIM


def _ret_proj_kernel(x_ref, w_ref, c_ref, s_ref, kd_ref, wrow_ref, o_ref, kdec_ref, watt_ref, wb_ref, *, tm, sub):
    j = pl.program_id(1)
    part = j // RET_BLOCKS_PER_PART
    watt_ref[...] = wrow_ref[:, RET_COLS:].astype(BF16)

    def sub_blocks():
        wb_ref[...] = w_ref[...].astype(BF16)
        for r in range(0, tm, sub):
            rows = slice(r, r + sub)
            yield rows, jnp.dot(x_ref[rows, :], wb_ref[...], preferred_element_type=F32)

    def rotary(rows, acc, scale):
        c = c_ref[rows, :] * scale
        s = s_ref[rows, :] * scale
        for h in range(RET_HEADS_PER_BLOCK):
            lo = h * RET_HEAD_DIM
            mid = lo + RET_HEAD_DIM // 2
            a1 = acc[:, lo:mid]
            a2 = acc[:, mid:lo + RET_HEAD_DIM]
            yield h, lo, a1 * c - a2 * s
            yield h, mid, a1 * s + a2 * c

    @pl.when(part == 0)
    def _():
        for rows, acc in sub_blocks():
            for _, col, val in rotary(rows, acc, 1.0):
                o_ref[rows, col:col + LANES] = val.astype(BF16)

    @pl.when(part == 1)
    def _():
        for rows, acc in sub_blocks():
            for h, col, val in rotary(rows, acc, RET_HEAD_DIM ** -0.5):
                o_ref[rows, col:col + LANES] = val.astype(BF16)
                chunks = val.reshape(sub // RET_CHUNK, RET_CHUNK, LANES) * kd_ref[h][None]
                kdec_ref[rows, col:col + LANES] = chunks.reshape(sub, LANES).astype(BF16)

    @pl.when(part == 2)
    def _():
        for rows, acc in sub_blocks():
            o_ref[rows, :] = acc.astype(BF16)

    @pl.when(part == 3)
    def _():
        for rows, acc in sub_blocks():
            o_ref[rows, :] = _silu(acc).astype(BF16)


def _ret_proj(xb, w_in, layer, cos_r, sin_r, k_decay, tm, sub):
    s = xb.shape[0]
    vmem = 2 * (tm * D_MODEL * 2 + D_MODEL * RET_TN * 4 + 2 * tm * RET_TN * 2 + 2 * tm * LANES * 4) \
        + D_MODEL * RET_TN * 2 + COMPILER_VMEM["ret_proj"]
    k_block = lambda j: jnp.clip(j - RET_BLOCKS_PER_PART, 0, RET_BLOCKS_PER_PART - 1)
    nj = RET_COLS // RET_TN
    slab = _slab_rows(D_MODEL, (s // tm) * nj)
    return pl.pallas_call(
        functools.partial(_ret_proj_kernel, tm=tm, sub=sub),
        grid=(s // tm, nj),
        in_specs=[pl.BlockSpec((tm, D_MODEL), lambda i, j: (i, 0)),
                  pl.BlockSpec((None, D_MODEL, RET_TN), lambda i, j: (layer, 0, j)),
                  pl.BlockSpec((tm, LANES), lambda i, j: (i, 0)),
                  pl.BlockSpec((tm, LANES), lambda i, j: (i, 0)),
                  pl.BlockSpec((RET_HEADS_PER_BLOCK, RET_CHUNK, LANES), lambda i, j: (k_block(j), 0, 0)),
                  pl.BlockSpec((None, slab, RET_COLS + ATT_IN_COLS), lambda i, j: (layer, i * nj + j, 0))],
        out_specs=[pl.BlockSpec((tm, RET_TN), lambda i, j: (i, j)),
                   pl.BlockSpec((tm, RET_TN), lambda i, j: (i, k_block(j))),
                   pl.BlockSpec((slab, ATT_IN_COLS), lambda i, j: (i * nj + j, 0))],
        out_shape=[jax.ShapeDtypeStruct((s, RET_COLS), BF16), jax.ShapeDtypeStruct((s, RET_WIDTH), BF16),
                   jax.ShapeDtypeStruct((D_MODEL, ATT_IN_COLS), BF16)],
        scratch_shapes=[pltpu.VMEM((D_MODEL, RET_TN), BF16)],
        compiler_params=_params(("arbitrary", "arbitrary"), vmem),
        name="ret_proj",
    )(xb, w_in, cos_r, sin_r, k_decay, w_in)


ATT_W_BLOCKS = ATT_IN_COLS // MXU_DIM


def _att_proj_kernel(x_ref, *refs, tm, sub):
    w_refs = refs[:ATT_W_BLOCKS]
    c_ref, sa_ref, sb_ref, o_ref = refs[ATT_W_BLOCKS:]

    for r in range(0, tm, sub):
        rows = slice(r, r + sub)
        c = c_ref[rows, :]
        sa = sa_ref[rows, :]
        sb = sb_ref[rows, :]

        def rot(a):
            return (a * c + pltpu.roll(a, LANES - ATT_HEAD_DIM // 2, 1) * sa
                    + pltpu.roll(a, ATT_HEAD_DIM // 2, 1) * sb)

        x = x_ref[rows, :]
        for wb in range(ATT_W_BLOCKS):
            acc = jnp.dot(x, w_refs[wb][...], preferred_element_type=F32)
            for half in range(MXU_DIM // LANES):
                col = wb * MXU_DIM + half * LANES
                a = acc[:, half * LANES:(half + 1) * LANES]
                if col < ATT_WIDTH:
                    o_ref[rows, col:col + LANES] = (rot(a) * ATT_SCORE_SCALE).astype(BF16)
                elif col < ATT_WIDTH + KV_WIDTH:
                    k = rot(a)
                    o_ref[rows, col:col + LANES] = k.astype(BF16)
                    o_ref[rows, col + 2 * LANES:col + 3 * LANES] = pltpu.roll(k, HALF, 1).astype(BF16)
                else:
                    o_ref[rows, col:col + LANES] = a.astype(BF16)
                    o_ref[rows, col + 2 * LANES:col + 3 * LANES] = pltpu.roll(a, HALF, 1).astype(BF16)


def _att_proj(xb, w_att_b, cos_a, sin_a, sin_b, tm, sub):
    s = xb.shape[0]
    vmem = 2 * (tm * D_MODEL * 2 + D_MODEL * ATT_IN_COLS * 2 + tm * ATT_OUT_COLS * 2 + 3 * tm * LANES * 4) \
        + COMPILER_VMEM["att_proj"]
    row = pl.BlockSpec((tm, LANES), lambda i: (i, 0))
    w_specs = [pl.BlockSpec((D_MODEL, MXU_DIM), functools.partial(lambda i, wb: (0, wb), wb=wb))
               for wb in range(ATT_W_BLOCKS)]
    return pl.pallas_call(
        functools.partial(_att_proj_kernel, tm=tm, sub=sub),
        grid=(s // tm,),
        in_specs=[pl.BlockSpec((tm, D_MODEL), lambda i: (i, 0))] + w_specs + [row, row, row],
        out_specs=pl.BlockSpec((tm, ATT_OUT_COLS), lambda i: (i, 0)),
        out_shape=jax.ShapeDtypeStruct((s, ATT_OUT_COLS), BF16),
        compiler_params=_params(("arbitrary",), vmem),
        name="att_proj",
    )(xb, *([w_att_b] * ATT_W_BLOCKS), cos_a, sin_a, sin_b)


def _mixer_kernel(cd_ref, sink_ref, r_ref, kdec_ref, a_ref, dec_ref, qd_ref, g_ref, b_ref, wo_ref, wg_ref, wp_ref,
                  o_ref, wob_ref, wgb_ref, wpb_ref, state_ref, kv_ref, *, tm, layer):
    i = pl.program_id(0)
    wob_ref[...] = wo_ref[...].astype(BF16)
    wgb_ref[...] = wg_ref[...].astype(BF16)
    wpb_ref[...] = wp_ref[...].astype(BF16)
    n_chunks = tm // RET_CHUNK
    kv_cols = 4 * KV_WIDTH

    @pl.when(i == 0)
    def _():
        state_ref[...] = jnp.zeros_like(state_ref)
        kv_ref[0:WINDOW, :] = jnp.zeros((WINDOW, kv_cols), BF16)

    @pl.when(i > 0)
    def _():
        kv_ref[0:WINDOW, :] = kv_ref[tm:tm + WINDOW, :]

    kv_ref[WINDOW:tm + WINDOW, :] = a_ref[:, ATT_WIDTH:ATT_WIDTH + kv_cols]

    contract_last = (((1,), (1,)), ((), ()))
    contract_first = (((0,), (0,)), ((), ()))
    qi = lax.broadcasted_iota(jnp.int32, (WINDOW, 2 * WINDOW), 0)
    kj = lax.broadcasted_iota(jnp.int32, (WINDOW, 2 * WINDOW), 1)
    band = (kj > qi) & (kj <= qi + WINDOW)
    low_lanes = lax.broadcasted_iota(jnp.int32, (2 * WINDOW, LANES), 1) < HALF

    def chunk(c, carry):
        r0 = pl.multiple_of(c * RET_CHUNK, RET_CHUNK)
        rows = pl.ds(r0, RET_CHUNK)

        for h in range(RET_HEADS):
            lo = h * RET_HEAD_DIM
            hi = lo + RET_HEAD_DIM
            q = r_ref[rows, lo:hi]
            k = r_ref[rows, RET_WIDTH + lo:RET_WIDTH + hi]
            v = r_ref[rows, 2 * RET_WIDTH + lo:2 * RET_WIDTH + hi]
            silu_gate = r_ref[rows, 3 * RET_WIDTH + lo:3 * RET_WIDTH + hi].astype(F32)
            st = state_ref[h]
            sc = lax.dot_general(q, k, contract_last, preferred_element_type=F32) * dec_ref[h]
            inner = jnp.dot(sc.astype(BF16), v, preferred_element_type=F32)
            cross = jnp.dot(q, st.astype(BF16), preferred_element_type=F32) * qd_ref[h]
            state_ref[h] = st * cd_ref[h] + lax.dot_general(kdec_ref[rows, lo:hi], v, contract_first,
                                                            preferred_element_type=F32)
            y = inner + cross
            mu = jnp.mean(y, axis=-1, keepdims=True)
            d = y - mu
            var = jnp.mean(d * d, axis=-1, keepdims=True)
            yn = d * lax.rsqrt(var + GN_EPS) * g_ref[:, lo:hi] + b_ref[:, lo:hi]
            o_ref[rows, lo:hi] = (silu_gate * yn).astype(BF16)

        kk = kv_ref[pl.ds(r0, 2 * WINDOW), :]
        k_nat = kk[:, 0:LANES]
        v_nat = kk[:, LANES:2 * LANES]
        k_swp = kk[:, 2 * LANES:3 * LANES]
        v_swp = kk[:, 3 * LANES:4 * LANES]
        zero = jnp.zeros_like(k_nat)
        k_ext = ((jnp.where(low_lanes, k_nat, zero), jnp.where(low_lanes, zero, k_swp)),
                 (jnp.where(low_lanes, k_swp, zero), jnp.where(low_lanes, zero, k_nat)))
        v_ext = ((jnp.where(low_lanes, v_nat, zero), jnp.where(low_lanes, zero, v_swp)),
                 (jnp.where(low_lanes, v_swp, zero), jnp.where(low_lanes, zero, v_nat)))
        first_block = (i * n_chunks + c) == 0
        mask = band & ((kj >= WINDOW) | jnp.logical_not(first_block))
        pairs_per_kv = ATT_GROUP // 2
        for kvh in range(ATT_KV_HEADS):
            for pr in range(pairs_per_kv):
                pair = kvh * pairs_per_kv + pr
                q_pair = a_ref[rows, pair * LANES:(pair + 1) * LANES]
                o_pair = None
                for par in range(2):
                    sink = sink_ref[layer, 2 * pair + par] * LOG2_E
                    sc = lax.dot_general(q_pair, k_ext[kvh][par], contract_last, preferred_element_type=F32)
                    sc = jnp.where(mask, sc, -jnp.inf)
                    m = jnp.maximum(jnp.max(sc, axis=-1, keepdims=True), sink)
                    p = jnp.exp2(sc - m)
                    den = jnp.sum(p, axis=-1, keepdims=True) + jnp.exp2(sink - m)
                    o = jnp.dot(p.astype(BF16), v_ext[kvh][par], preferred_element_type=F32) * (1.0 / den)
                    o_pair = o if o_pair is None else o_pair + o
                o_ref[rows, RET_WIDTH + pair * LANES:RET_WIDTH + (pair + 1) * LANES] = o_pair.astype(BF16)
        return carry

    lax.fori_loop(0, n_chunks, chunk, 0, unroll=4)


def _mixer(rp, kdec, ap, ret_consts, gn_g, gn_b, sinks, w_out, w_gate, w_proj, layer, tm):
    s = rp.shape[0]
    decay, q_decay, chunk_decay = ret_consts
    smem = pl.BlockSpec(memory_space=pltpu.SMEM)
    const3 = lambda shape: pl.BlockSpec(shape, lambda i: (0, 0, 0))
    steps = s // tm
    slab_d, slab_p = _slab_rows(D_MODEL, steps), _slab_rows(PLE_DIM, steps)
    w_in_spec = lambda rows: pl.BlockSpec((None, rows, D_MODEL), lambda i: (layer, i, 0))
    w_out_spec = lambda rows: pl.BlockSpec((rows, D_MODEL), lambda i: (i, 0))
    w_shape = lambda rows: jax.ShapeDtypeStruct((rows, D_MODEL), BF16)
    vmem = 2 * (tm * (RET_COLS + RET_WIDTH) * 2 + tm * ATT_OUT_COLS * 2 + tm * D_MODEL * 2) \
        + 2 * (2 * slab_d + slab_p) * D_MODEL * 6 \
        + 2 * 4 * (decay.size + q_decay.size) \
        + RET_HEADS * RET_HEAD_DIM * RET_HEAD_DIM * 4 + (tm + WINDOW) * 4 * KV_WIDTH * 2 + COMPILER_VMEM["mixer"]
    return pl.pallas_call(
        functools.partial(_mixer_kernel, tm=tm, layer=layer),
        grid=(s // tm,),
        in_specs=[smem, smem,
                  pl.BlockSpec((tm, RET_COLS), lambda i: (i, 0)),
                  pl.BlockSpec((tm, RET_WIDTH), lambda i: (i, 0)),
                  pl.BlockSpec((tm, ATT_OUT_COLS), lambda i: (i, 0)),
                  const3(decay.shape), const3(q_decay.shape),
                  _layer_vec(layer, RET_WIDTH), _layer_vec(layer, RET_WIDTH),
                  w_in_spec(slab_d), w_in_spec(slab_d), w_in_spec(slab_p)],
        out_specs=[pl.BlockSpec((tm, D_MODEL), lambda i: (i, 0)),
                   w_out_spec(slab_d), w_out_spec(slab_d), w_out_spec(slab_p)],
        out_shape=[jax.ShapeDtypeStruct((s, D_MODEL), BF16), w_shape(D_MODEL), w_shape(D_MODEL), w_shape(PLE_DIM)],
        scratch_shapes=[pltpu.VMEM((RET_HEADS, RET_HEAD_DIM, RET_HEAD_DIM), F32),
                        pltpu.VMEM((tm + WINDOW, 4 * KV_WIDTH), BF16)],
        compiler_params=_params(("arbitrary",), vmem),
        name="mixer",
    )(chunk_decay, sinks, rp, kdec, ap, decay, q_decay, gn_g, gn_b, w_out, w_gate, w_proj)


def _retention_constants():
    c = RET_CHUNK
    log_g = jnp.log1p(-jnp.exp2(-5.0 - jnp.arange(RET_HEADS, dtype=F32)))
    idx = jnp.arange(c, dtype=F32)
    diff = idx[:, None] - idx[None, :]
    decay = jnp.where(diff[None] >= 0, jnp.exp(log_g[:, None, None] * jnp.maximum(diff, 0.0)[None]), 0.0)
    q_decay = jnp.exp(log_g[:, None] * (idx[None, :] + 1.0))
    k_decay = jnp.exp(log_g[:, None] * (c - 1.0 - idx[None, :]))
    chunk_decay = jnp.exp(log_g * c)
    wide = lambda t, lanes: jnp.broadcast_to(t[:, :, None], (RET_HEADS, c, lanes))
    return (decay, wide(q_decay, RET_HEAD_DIM), chunk_decay), wide(k_decay, LANES)


def _mix_out_kernel(cat_ref, x_ref, p_ref, wo_ref, wg_ref, wp_ref, g_ref, b_ref, bg_ref, wd_ref,
                    r_ref, xb_ref, wdb_ref, *, tm, sub):
    wdb_ref[...] = wd_ref[...].astype(BF16)
    for r in range(0, tm, sub):
        rows = slice(r, r + sub)
        z = DEEPNORM_ALPHA * x_ref[rows, :] + jnp.dot(cat_ref[rows, :], wo_ref[...], preferred_element_type=F32)
        x1 = _layer_norm(z, g_ref[...], b_ref[...])
        x1b = x1.astype(BF16)
        gate = jnp.dot(x1b, wg_ref[...], preferred_element_type=F32) + bg_ref[...]
        ple = jnp.dot(p_ref[rows, :].astype(BF16), wp_ref[...], preferred_element_type=F32) * _sigmoid(gate)
        r_ref[rows, :] = DEEPNORM_ALPHA * x1 + ple
        xb_ref[rows, :] = x1b


def _mix_out(cat, x, p, layer, w_out_b, w_gate_b, w_proj_b, ln_g, ln_b, b_gate, w_down, tm, sub):
    s = x.shape[0]
    row = lambda cols: pl.BlockSpec((tm, cols), lambda i: (i, 0))
    resident = lambda rows_: pl.BlockSpec((rows_, D_MODEL), lambda i: (0, 0), pipeline_mode=pl.Buffered(1))
    slab = _slab_rows(D_FF, s // tm)
    vmem = (2 * D_MODEL * D_MODEL + PLE_DIM * D_MODEL) * 2 + 2 * slab * D_MODEL * 6 \
        + 2 * tm * (D_MODEL * 2 + D_MODEL * 4 + PLE_DIM * 4 + D_MODEL * 4 + D_MODEL * 2) + COMPILER_VMEM["mix_out"]
    return pl.pallas_call(
        functools.partial(_mix_out_kernel, tm=tm, sub=sub),
        grid=(s // tm,),
        in_specs=[row(D_MODEL), row(D_MODEL),
                  pl.BlockSpec((None, tm, PLE_DIM), lambda i: (layer, i, 0)),
                  resident(D_MODEL), resident(D_MODEL), resident(PLE_DIM),
                  _layer_vec(layer, D_MODEL), _layer_vec(layer, D_MODEL), _layer_vec(layer, D_MODEL),
                  pl.BlockSpec((None, slab, D_MODEL), lambda i: (layer, i, 0))],
        out_specs=[row(D_MODEL), row(D_MODEL), pl.BlockSpec((slab, D_MODEL), lambda i: (i, 0))],
        out_shape=[jax.ShapeDtypeStruct((s, D_MODEL), F32), jax.ShapeDtypeStruct((s, D_MODEL), BF16),
                   jax.ShapeDtypeStruct((D_FF, D_MODEL), BF16)],
        compiler_params=_params(("arbitrary",), vmem),
        name="mix_out",
    )(cat, x, p, w_out_b, w_gate_b, w_proj_b, ln_g, ln_b, b_gate, w_down)


FFN_TN = 512
FFN_BLOCKS = D_FF // FFN_TN


def _ffn_up_kernel(x_ref, wg_ref, wu_ref, cwg_ref, cwu_ref, cbg_ref, cbu_ref, o_ref, carry_ref, wgb_ref, wub_ref,
                   *, tm, sub):
    @pl.when(pl.program_id(1) == 0)
    def _():
        wgb_ref[...] = wg_ref[...].astype(BF16)
        wub_ref[...] = wu_ref[...].astype(BF16)
        carry_ref[...] = jnp.zeros_like(carry_ref)

    def conv(prev, h, cw_ref, cb_ref):
        ext = jnp.concatenate([prev, h], axis=0)
        y = (ext * cw_ref[2:3, :] + pltpu.roll(ext, 1, 0) * cw_ref[1:2, :]
             + pltpu.roll(ext, 2, 0) * cw_ref[0:1, :] + cb_ref[...])
        return y[SUBLANES:]

    prev_g = carry_ref[0]
    prev_u = carry_ref[1]
    for r in range(0, tm, sub):
        x = x_ref[r:r + sub, :]
        hg = jnp.dot(x, wgb_ref[...], preferred_element_type=F32)
        hu = jnp.dot(x, wub_ref[...], preferred_element_type=F32)
        yg = conv(prev_g, hg, cwg_ref, cbg_ref)
        yu = conv(prev_u, hu, cwu_ref, cbu_ref)
        prev_g = hg[sub - SUBLANES:sub]
        prev_u = hu[sub - SUBLANES:sub]
        o_ref[r:r + sub, :] = (_silu(yg) * yu).astype(BF16)
    carry_ref[0] = prev_g
    carry_ref[1] = prev_u


def _ffn_up(xb, w_up, conv_w, conv_b, layer, tm, sub):
    s = xb.shape[0]
    nb = FFN_BLOCKS
    vmem = 2 * (tm * D_MODEL * 2 + 2 * D_MODEL * FFN_TN * 4 + tm * FFN_TN * 2) + 2 * D_MODEL * FFN_TN * 2 \
        + COMPILER_VMEM["ffn_up"]
    wspec = lambda off: pl.BlockSpec((None, D_MODEL, FFN_TN), lambda j, i: (layer, 0, j + off))
    cwspec = lambda off: pl.BlockSpec((None, CONV_WIDTH, FFN_TN), lambda j, i: (layer, 0, j + off))
    cbspec = lambda off: pl.BlockSpec((None, 1, FFN_TN), lambda j, i: (layer, 0, j + off))
    return pl.pallas_call(
        functools.partial(_ffn_up_kernel, tm=tm, sub=sub),
        grid=(nb, s // tm),
        in_specs=[pl.BlockSpec((tm, D_MODEL), lambda j, i: (i, 0)),
                  wspec(0), wspec(nb), cwspec(0), cwspec(nb), cbspec(0), cbspec(nb)],
        out_specs=pl.BlockSpec((tm, FFN_TN), lambda j, i: (i, j)),
        out_shape=jax.ShapeDtypeStruct((s, D_FF), BF16),
        scratch_shapes=[pltpu.VMEM((2, SUBLANES, FFN_TN), F32),
                        pltpu.VMEM((D_MODEL, FFN_TN), BF16), pltpu.VMEM((D_MODEL, FFN_TN), BF16)],
        compiler_params=_params(("arbitrary", "arbitrary"), vmem),
        name="ffn_up",
    )(xb, w_up, w_up, conv_w, conv_w, conv_b, conv_b)


def _ffn_down_kernel(a_ref, r_ref, w_ref, g_ref, b_ref, x_ref, xb_ref, *, tm, sub):
    for r in range(0, tm, sub):
        rows = slice(r, r + sub)
        z = r_ref[rows, :] + jnp.dot(a_ref[rows, :], w_ref[...], preferred_element_type=F32)
        x2 = _layer_norm(z, g_ref[...], b_ref[...])
        x_ref[rows, :] = x2
        xb_ref[rows, :] = x2.astype(BF16)


def _ffn_down(act, r, w_down_b, ln_g, ln_b, layer, tm, sub):
    s = r.shape[0]
    row = lambda cols: pl.BlockSpec((tm, cols), lambda i: (i, 0))
    vmem = D_FF * D_MODEL * 2 + 2 * tm * (D_FF * 2 + D_MODEL * 4 + D_MODEL * 4 + D_MODEL * 2) + COMPILER_VMEM["ffn_down"]
    return pl.pallas_call(
        functools.partial(_ffn_down_kernel, tm=tm, sub=sub),
        grid=(s // tm,),
        in_specs=[row(D_FF), row(D_MODEL),
                  pl.BlockSpec((D_FF, D_MODEL), lambda i: (0, 0), pipeline_mode=pl.Buffered(1)),
                  _layer_vec(layer, D_MODEL), _layer_vec(layer, D_MODEL)],
        out_specs=[row(D_MODEL), row(D_MODEL)],
        out_shape=[jax.ShapeDtypeStruct((s, D_MODEL), F32), jax.ShapeDtypeStruct((s, D_MODEL), BF16)],
        compiler_params=_params(("arbitrary",), vmem),
        name="ffn_down",
    )(act, r, w_down_b, ln_g, ln_b)


def _tiles(s):
    pick = lambda tm, sub: (min(tm, s), min(sub, s))
    return dict(tables=pick(1024, 1024), ret_proj=pick(2048, 1024), att_proj=pick(1024, 256),
                mixer=pick(1024, 128), mix_out=pick(512, 512), ffn_up=pick(1024, 1024), ffn_down=pick(512, 256))


def kernel(x, p, positions, w_in, w_out, ret_norm_g, ret_norm_b, attn_sinks, ln1_g, ln1_b, w_ffn_up,
           ffn_conv_w, ffn_conv_b, w_ffn_down, ln2_g, ln2_b, w_ple_gate, b_ple_gate, w_ple_proj):
    batch, s, d = x.shape
    assert batch == 1 and d == D_MODEL and s % RET_CHUNK == 0
    t = _tiles(s)
    assert all(s % tm == 0 and tm % sub == 0 for tm, sub in t.values())

    xf = x.reshape(s, d)
    pf = p.reshape(DEPTH, s, PLE_DIM)
    cos_r, sin_r, cos_a, sin_a, sin_b, xb = _rot_tables(positions.reshape(s), xf, t["tables"][0])
    ret_consts, k_decay = _retention_constants()
    vecs = lambda v: v.reshape(DEPTH, 1, -1)
    gn_g, gn_b, g1, b1, g2, b2 = (vecs(v) for v in (ret_norm_g, ret_norm_b, ln1_g, ln1_b, ln2_g, ln2_b))
    bg, conv_b = vecs(b_ple_gate), vecs(ffn_conv_b)

    for l in range(DEPTH):
        rp, kdec, w_att_b = _ret_proj(xb, w_in, l, cos_r, sin_r, k_decay, *t["ret_proj"])
        ap = _att_proj(xb, w_att_b, cos_a, sin_a, sin_b, *t["att_proj"])
        cat, w_out_b, w_gate_b, w_proj_b = _mixer(rp, kdec, ap, ret_consts, gn_g, gn_b, attn_sinks,
                                                  w_out, w_ple_gate, w_ple_proj, l, t["mixer"][0])
        r, xb, w_down_b = _mix_out(cat, xf, pf, l, w_out_b, w_gate_b, w_proj_b, g1, b1, bg, w_ffn_down,
                                   *t["mix_out"])
        act = _ffn_up(xb, w_ffn_up, ffn_conv_w, conv_b, l, *t["ffn_up"])
        xf, xb = _ffn_down(act, r, w_down_b, g2, b2, l, *t["ffn_down"])
    return xf.reshape(batch, s, d)
```

```python
import functools

import jax
import jax.numpy as jnp
from jax import lax
from jax.experimental import pallas as pl
from jax.experimental.pallas import tpu as pltpu

F32 = jnp.float32
BF16 = jnp.bfloat16

D_MODEL = 2048
DEPTH = 4
PLE_DIM = 256
RET_WIDTH = 1024
RET_HEADS = 4
RET_HEAD_DIM = 256
RET_CHUNK = 128
RET_ROT_BASE = 10000.0
ATT_HEAD_DIM = 64
ATT_WIDTH = 1024
ATT_HEADS = 16
ATT_KV_HEADS = 2
ATT_GROUP = ATT_HEADS // ATT_KV_HEADS
KV_WIDTH = 128
WINDOW = 128
ROPE_THETA = 10000.0
D_FF = 5632
CONV_WIDTH = 3
LN_EPS = 1e-5
GN_EPS = 1e-6
DEEPNORM_ALPHA = (2 * DEPTH) ** 0.25

LANES = 128
SUBLANES = 8
BF16_SUBLANE_ROWS = 16
MXU_DIM = 256
MIB = 1 << 20
VMEM_LIMIT_CAP = 58 * MIB
COMPILER_VMEM = dict(rot_tables=16 * MIB, ret_proj=12 * MIB, att_proj=12 * MIB, mixer=14 * MIB, mix_out=12 * MIB,
                     ffn_up=12 * MIB, ffn_down=8 * MIB)

RET_COLS = 4 * RET_WIDTH
ATT_IN_COLS = ATT_WIDTH + 2 * KV_WIDTH
ATT_OUT_COLS = ATT_WIDTH + 4 * KV_WIDTH
HALF = LANES // 2
LOG2_E = 1.4426950408889634
ATT_SCORE_SCALE = ATT_HEAD_DIM ** -0.5 * LOG2_E


def _params(semantics, vmem_bytes):
    return pltpu.CompilerParams(dimension_semantics=semantics,
                                vmem_limit_bytes=int(min(vmem_bytes, VMEM_LIMIT_CAP)))


def _layer_norm(z, g, b):
    t = z - jnp.mean(z[:, :LANES], axis=-1, keepdims=True)
    m1 = jnp.mean(t, axis=-1, keepdims=True)
    m2 = jnp.mean(t * t, axis=-1, keepdims=True)
    return (t - m1) * lax.rsqrt(m2 - m1 * m1 + LN_EPS) * g + b


def _sigmoid(v):
    return 0.5 + 0.5 * jnp.tanh(0.5 * v)


def _silu(v):
    t = 0.5 * v
    return t + t * jnp.tanh(t)


def _slab_rows(rows, steps):
    assert rows % steps == 0 and (rows // steps) % BF16_SUBLANE_ROWS == 0, (rows, steps)
    return rows // steps


def _layer_vec(layer, cols):
    return pl.BlockSpec((None, 1, cols), lambda *_: (layer, 0, 0))


def _rot_tables_kernel(pos_ref, fr_ref, fa_ref, x_ref, cr_ref, sr_ref, ca_ref, sa_ref, sb_ref, xb_ref):
    xb_ref[...] = x_ref[...].astype(BF16)
    pos = pos_ref[...].astype(F32)
    ang_r = pos * fr_ref[...]
    cr_ref[...] = jnp.cos(ang_r)
    sr_ref[...] = jnp.sin(ang_r)
    ang_a = pos * fa_ref[...]
    c = jnp.cos(ang_a)
    s = jnp.sin(ang_a)
    lane = lax.broadcasted_iota(jnp.int32, c.shape, 1)
    first_half = (lane & (ATT_HEAD_DIM // 2)) == 0
    ca_ref[...] = c
    sa_ref[...] = jnp.where(first_half, -s, 0.0)
    sb_ref[...] = jnp.where(first_half, 0.0, s)


def _rot_tables(positions, xf, tm):
    s = positions.shape[0]
    ret_inv_freq = 1.0 / (RET_ROT_BASE ** jnp.linspace(0.0, 1.0, RET_HEAD_DIM // 2, dtype=F32))
    att_inv_freq = ROPE_THETA ** (-jnp.arange(0, ATT_HEAD_DIM, 2, dtype=F32) / ATT_HEAD_DIM)
    fr = ret_inv_freq.reshape(1, LANES)
    fa = jnp.tile(att_inv_freq, LANES // (ATT_HEAD_DIM // 2)).reshape(1, LANES)
    tab = jax.ShapeDtypeStruct((s, LANES), F32)
    row = pl.BlockSpec((tm, LANES), lambda i: (i, 0))
    const = pl.BlockSpec((1, LANES), lambda i: (0, 0))
    wide = pl.BlockSpec((tm, D_MODEL), lambda i: (i, 0))
    return pl.pallas_call(
        _rot_tables_kernel,
        grid=(s // tm,),
        in_specs=[pl.BlockSpec((tm, 1), lambda i: (i, 0)), const, const, wide],
        out_specs=[row] * 5 + [wide],
        out_shape=[tab] * 5 + [jax.ShapeDtypeStruct((s, D_MODEL), BF16)],
        compiler_params=_params(("arbitrary",), 2 * tm * D_MODEL * 6 + COMPILER_VMEM["rot_tables"]),
        name="rot_tables",
    )(positions.reshape(s, 1), fr, fa, xf)


RET_TN = 512
RET_BLOCKS_PER_PART = RET_WIDTH // RET_TN
RET_HEADS_PER_BLOCK = RET_TN // RET_HEAD_DIM


def _ret_proj_kernel(x_ref, w_ref, c_ref, s_ref, kd_ref, wrow_ref, o_ref, kdec_ref, watt_ref, wb_ref, *, tm, sub):
    j = pl.program_id(1)
    part = j // RET_BLOCKS_PER_PART
    watt_ref[...] = wrow_ref[:, RET_COLS:].astype(BF16)

    def sub_blocks():
        wb_ref[...] = w_ref[...].astype(BF16)
        for r in range(0, tm, sub):
            rows = slice(r, r + sub)
            yield rows, jnp.dot(x_ref[rows, :], wb_ref[...], preferred_element_type=F32)

    def rotary(rows, acc, scale):
        c = c_ref[rows, :] * scale
        s = s_ref[rows, :] * scale
        for h in range(RET_HEADS_PER_BLOCK):
            lo = h * RET_HEAD_DIM
            mid = lo + RET_HEAD_DIM // 2
            a1 = acc[:, lo:mid]
            a2 = acc[:, mid:lo + RET_HEAD_DIM]
            yield h, lo, a1 * c - a2 * s
            yield h, mid, a1 * s + a2 * c

    @pl.when(part == 0)
    def _():
        for rows, acc in sub_blocks():
            for _, col, val in rotary(rows, acc, 1.0):
                o_ref[rows, col:col + LANES] = val.astype(BF16)

    @pl.when(part == 1)
    def _():
        for rows, acc in sub_blocks():
            for h, col, val in rotary(rows, acc, RET_HEAD_DIM ** -0.5):
                o_ref[rows, col:col + LANES] = val.astype(BF16)
                chunks = val.reshape(sub // RET_CHUNK, RET_CHUNK, LANES) * kd_ref[h][None]
                kdec_ref[rows, col:col + LANES] = chunks.reshape(sub, LANES).astype(BF16)

    @pl.when(part == 2)
    def _():
        for rows, acc in sub_blocks():
            o_ref[rows, :] = acc.astype(BF16)

    @pl.when(part == 3)
    def _():
        for rows, acc in sub_blocks():
            o_ref[rows, :] = _silu(acc).astype(BF16)


def _ret_proj(xb, w_in, layer, cos_r, sin_r, k_decay, tm, sub):
    s = xb.shape[0]
    vmem = 2 * (tm * D_MODEL * 2 + D_MODEL * RET_TN * 4 + 2 * tm * RET_TN * 2 + 2 * tm * LANES * 4) \
        + D_MODEL * RET_TN * 2 + COMPILER_VMEM["ret_proj"]
    k_block = lambda j: jnp.clip(j - RET_BLOCKS_PER_PART, 0, RET_BLOCKS_PER_PART - 1)
    nj = RET_COLS // RET_TN
    slab = _slab_rows(D_MODEL, (s // tm) * nj)
    return pl.pallas_call(
        functools.partial(_ret_proj_kernel, tm=tm, sub=sub),
        grid=(s // tm, nj),
        in_specs=[pl.BlockSpec((tm, D_MODEL), lambda i, j: (i, 0)),
                  pl.BlockSpec((None, D_MODEL, RET_TN), lambda i, j: (layer, 0, j)),
                  pl.BlockSpec((tm, LANES), lambda i, j: (i, 0)),
                  pl.BlockSpec((tm, LANES), lambda i, j: (i, 0)),
                  pl.BlockSpec((RET_HEADS_PER_BLOCK, RET_CHUNK, LANES), lambda i, j: (k_block(j), 0, 0)),
                  pl.BlockSpec((None, slab, RET_COLS + ATT_IN_COLS), lambda i, j: (layer, i * nj + j, 0))],
        out_specs=[pl.BlockSpec((tm, RET_TN), lambda i, j: (i, j)),
                   pl.BlockSpec((tm, RET_TN), lambda i, j: (i, k_block(j))),
                   pl.BlockSpec((slab, ATT_IN_COLS), lambda i, j: (i * nj + j, 0))],
        out_shape=[jax.ShapeDtypeStruct((s, RET_COLS), BF16), jax.ShapeDtypeStruct((s, RET_WIDTH), BF16),
                   jax.ShapeDtypeStruct((D_MODEL, ATT_IN_COLS), BF16)],
        scratch_shapes=[pltpu.VMEM((D_MODEL, RET_TN), BF16)],
        compiler_params=_params(("arbitrary", "arbitrary"), vmem),
        name="ret_proj",
    )(xb, w_in, cos_r, sin_r, k_decay, w_in)


ATT_W_BLOCKS = ATT_IN_COLS // MXU_DIM


def _att_proj_kernel(x_ref, *refs, tm, sub):
    w_refs = refs[:ATT_W_BLOCKS]
    c_ref, sa_ref, sb_ref, o_ref = refs[ATT_W_BLOCKS:]

    for r in range(0, tm, sub):
        rows = slice(r, r + sub)
        c = c_ref[rows, :]
        sa = sa_ref[rows, :]
        sb = sb_ref[rows, :]

        def rot(a):
            return (a * c + pltpu.roll(a, LANES - ATT_HEAD_DIM // 2, 1) * sa
                    + pltpu.roll(a, ATT_HEAD_DIM // 2, 1) * sb)

        x = x_ref[rows, :]
        for wb in range(ATT_W_BLOCKS):
            acc = jnp.dot(x, w_refs[wb][...], preferred_element_type=F32)
            for half in range(MXU_DIM // LANES):
                col = wb * MXU_DIM + half * LANES
                a = acc[:, half * LANES:(half + 1) * LANES]
                if col < ATT_WIDTH:
                    o_ref[rows, col:col + LANES] = (rot(a) * ATT_SCORE_SCALE).astype(BF16)
                elif col < ATT_WIDTH + KV_WIDTH:
                    k = rot(a)
                    o_ref[rows, col:col + LANES] = k.astype(BF16)
                    o_ref[rows, col + 2 * LANES:col + 3 * LANES] = pltpu.roll(k, HALF, 1).astype(BF16)
                else:
                    o_ref[rows, col:col + LANES] = a.astype(BF16)
                    o_ref[rows, col + 2 * LANES:col + 3 * LANES] = pltpu.roll(a, HALF, 1).astype(BF16)


def _att_proj(xb, w_att_b, cos_a, sin_a, sin_b, tm, sub):
    s = xb.shape[0]
    vmem = 2 * (tm * D_MODEL * 2 + D_MODEL * ATT_IN_COLS * 2 + tm * ATT_OUT_COLS * 2 + 3 * tm * LANES * 4) \
        + COMPILER_VMEM["att_proj"]
    row = pl.BlockSpec((tm, LANES), lambda i: (i, 0))
    w_specs = [pl.BlockSpec((D_MODEL, MXU_DIM), functools.partial(lambda i, wb: (0, wb), wb=wb))
               for wb in range(ATT_W_BLOCKS)]
    return pl.pallas_call(
        functools.partial(_att_proj_kernel, tm=tm, sub=sub),
        grid=(s // tm,),
        in_specs=[pl.BlockSpec((tm, D_MODEL), lambda i: (i, 0))] + w_specs + [row, row, row],
        out_specs=pl.BlockSpec((tm, ATT_OUT_COLS), lambda i: (i, 0)),
        out_shape=jax.ShapeDtypeStruct((s, ATT_OUT_COLS), BF16),
        compiler_params=_params(("arbitrary",), vmem),
        name="att_proj",
    )(xb, *([w_att_b] * ATT_W_BLOCKS), cos_a, sin_a, sin_b)


def _mixer_kernel(cd_ref, sink_ref, r_ref, kdec_ref, a_ref, dec_ref, qd_ref, g_ref, b_ref, wo_ref, wg_ref, wp_ref,
                  o_ref, wob_ref, wgb_ref, wpb_ref, state_ref, kv_ref, *, tm, layer):
    i = pl.program_id(0)
    wob_ref[...] = wo_ref[...].astype(BF16)
    wgb_ref[...] = wg_ref[...].astype(BF16)
    wpb_ref[...] = wp_ref[...].astype(BF16)
    n_chunks = tm // RET_CHUNK
    kv_cols = 4 * KV_WIDTH

    @pl.when(i == 0)
    def _():
        state_ref[...] = jnp.zeros_like(state_ref)
        kv_ref[0:WINDOW, :] = jnp.zeros((WINDOW, kv_cols), BF16)

    @pl.when(i > 0)
    def _():
        kv_ref[0:WINDOW, :] = kv_ref[tm:tm + WINDOW, :]

    kv_ref[WINDOW:tm + WINDOW, :] = a_ref[:, ATT_WIDTH:ATT_WIDTH + kv_cols]

    contract_last = (((1,), (1,)), ((), ()))
    contract_first = (((0,), (0,)), ((), ()))
    qi = lax.broadcasted_iota(jnp.int32, (WINDOW, 2 * WINDOW), 0)
    kj = lax.broadcasted_iota(jnp.int32, (WINDOW, 2 * WINDOW), 1)
    band = (kj > qi) & (kj <= qi + WINDOW)
    low_lanes = lax.broadcasted_iota(jnp.int32, (2 * WINDOW, LANES), 1) < HALF

    def chunk(c, carry):
        r0 = pl.multiple_of(c * RET_CHUNK, RET_CHUNK)
        rows = pl.ds(r0, RET_CHUNK)

        for h in range(RET_HEADS):
            lo = h * RET_HEAD_DIM
            hi = lo + RET_HEAD_DIM
            q = r_ref[rows, lo:hi]
            k = r_ref[rows, RET_WIDTH + lo:RET_WIDTH + hi]
            v = r_ref[rows, 2 * RET_WIDTH + lo:2 * RET_WIDTH + hi]
            silu_gate = r_ref[rows, 3 * RET_WIDTH + lo:3 * RET_WIDTH + hi].astype(F32)
            st = state_ref[h]
            sc = lax.dot_general(q, k, contract_last, preferred_element_type=F32) * dec_ref[h]
            inner = jnp.dot(sc.astype(BF16), v, preferred_element_type=F32)
            cross = jnp.dot(q, st.astype(BF16), preferred_element_type=F32) * qd_ref[h]
            state_ref[h] = st * cd_ref[h] + lax.dot_general(kdec_ref[rows, lo:hi], v, contract_first,
                                                            preferred_element_type=F32)
            y = inner + cross
            mu = jnp.mean(y, axis=-1, keepdims=True)
            d = y - mu
            var = jnp.mean(d * d, axis=-1, keepdims=True)
            yn = d * lax.rsqrt(var + GN_EPS) * g_ref[:, lo:hi] + b_ref[:, lo:hi]
            o_ref[rows, lo:hi] = (silu_gate * yn).astype(BF16)

        kk = kv_ref[pl.ds(r0, 2 * WINDOW), :]
        k_nat = kk[:, 0:LANES]
        v_nat = kk[:, LANES:2 * LANES]
        k_swp = kk[:, 2 * LANES:3 * LANES]
        v_swp = kk[:, 3 * LANES:4 * LANES]
        zero = jnp.zeros_like(k_nat)
        k_ext = ((jnp.where(low_lanes, k_nat, zero), jnp.where(low_lanes, zero, k_swp)),
                 (jnp.where(low_lanes, k_swp, zero), jnp.where(low_lanes, zero, k_nat)))
        v_ext = ((jnp.where(low_lanes, v_nat, zero), jnp.where(low_lanes, zero, v_swp)),
                 (jnp.where(low_lanes, v_swp, zero), jnp.where(low_lanes, zero, v_nat)))
        first_block = (i * n_chunks + c) == 0
        mask = band & ((kj >= WINDOW) | jnp.logical_not(first_block))
        pairs_per_kv = ATT_GROUP // 2
        for kvh in range(ATT_KV_HEADS):
            for pr in range(pairs_per_kv):
                pair = kvh * pairs_per_kv + pr
                q_pair = a_ref[rows, pair * LANES:(pair + 1) * LANES]
                o_pair = None
                for par in range(2):
                    sink = sink_ref[layer, 2 * pair + par] * LOG2_E
                    sc = lax.dot_general(q_pair, k_ext[kvh][par], contract_last, preferred_element_type=F32)
                    sc = jnp.where(mask, sc, -jnp.inf)
                    m = jnp.maximum(jnp.max(sc, axis=-1, keepdims=True), sink)
                    p = jnp.exp2(sc - m)
                    den = jnp.sum(p, axis=-1, keepdims=True) + jnp.exp2(sink - m)
                    o = jnp.dot(p.astype(BF16), v_ext[kvh][par], preferred_element_type=F32) * (1.0 / den)
                    o_pair = o if o_pair is None else o_pair + o
                o_ref[rows, RET_WIDTH + pair * LANES:RET_WIDTH + (pair + 1) * LANES] = o_pair.astype(BF16)
        return carry

    lax.fori_loop(0, n_chunks, chunk, 0, unroll=4)


def _mixer(rp, kdec, ap, ret_consts, gn_g, gn_b, sinks, w_out, w_gate, w_proj, layer, tm):
    s = rp.shape[0]
    decay, q_decay, chunk_decay = ret_consts
    smem = pl.BlockSpec(memory_space=pltpu.SMEM)
    const3 = lambda shape: pl.BlockSpec(shape, lambda i: (0, 0, 0))
    steps = s // tm
    slab_d, slab_p = _slab_rows(D_MODEL, steps), _slab_rows(PLE_DIM, steps)
    w_in_spec = lambda rows: pl.BlockSpec((None, rows, D_MODEL), lambda i: (layer, i, 0))
    w_out_spec = lambda rows: pl.BlockSpec((rows, D_MODEL), lambda i: (i, 0))
    w_shape = lambda rows: jax.ShapeDtypeStruct((rows, D_MODEL), BF16)
    vmem = 2 * (tm * (RET_COLS + RET_WIDTH) * 2 + tm * ATT_OUT_COLS * 2 + tm * D_MODEL * 2) \
        + 2 * (2 * slab_d + slab_p) * D_MODEL * 6 \
        + 2 * 4 * (decay.size + q_decay.size) \
        + RET_HEADS * RET_HEAD_DIM * RET_HEAD_DIM * 4 + (tm + WINDOW) * 4 * KV_WIDTH * 2 + COMPILER_VMEM["mixer"]
    return pl.pallas_call(
        functools.partial(_mixer_kernel, tm=tm, layer=layer),
        grid=(s // tm,),
        in_specs=[smem, smem,
                  pl.BlockSpec((tm, RET_COLS), lambda i: (i, 0)),
                  pl.BlockSpec((tm, RET_WIDTH), lambda i: (i, 0)),
                  pl.BlockSpec((tm, ATT_OUT_COLS), lambda i: (i, 0)),
                  const3(decay.shape), const3(q_decay.shape),
                  _layer_vec(layer, RET_WIDTH), _layer_vec(layer, RET_WIDTH),
                  w_in_spec(slab_d), w_in_spec(slab_d), w_in_spec(slab_p)],
        out_specs=[pl.BlockSpec((tm, D_MODEL), lambda i: (i, 0)),
                   w_out_spec(slab_d), w_out_spec(slab_d), w_out_spec(slab_p)],
        out_shape=[jax.ShapeDtypeStruct((s, D_MODEL), BF16), w_shape(D_MODEL), w_shape(D_MODEL), w_shape(PLE_DIM)],
        scratch_shapes=[pltpu.VMEM((RET_HEADS, RET_HEAD_DIM, RET_HEAD_DIM), F32),
                        pltpu.VMEM((tm + WINDOW, 4 * KV_WIDTH), BF16)],
        compiler_params=_params(("arbitrary",), vmem),
        name="mixer",
    )(chunk_decay, sinks, rp, kdec, ap, decay, q_decay, gn_g, gn_b, w_out, w_gate, w_proj)


def _retention_constants():
    c = RET_CHUNK
    log_g = jnp.log1p(-jnp.exp2(-5.0 - jnp.arange(RET_HEADS, dtype=F32)))
    idx = jnp.arange(c, dtype=F32)
    diff = idx[:, None] - idx[None, :]
    decay = jnp.where(diff[None] >= 0, jnp.exp(log_g[:, None, None] * jnp.maximum(diff, 0.0)[None]), 0.0)
    q_decay = jnp.exp(log_g[:, None] * (idx[None, :] + 1.0))
    k_decay = jnp.exp(log_g[:, None] * (c - 1.0 - idx[None, :]))
    chunk_decay = jnp.exp(log_g * c)
    wide = lambda t, lanes: jnp.broadcast_to(t[:, :, None], (RET_HEADS, c, lanes))
    return (decay, wide(q_decay, RET_HEAD_DIM), chunk_decay), wide(k_decay, LANES)


def _mix_out_kernel(cat_ref, x_ref, p_ref, wo_ref, wg_ref, wp_ref, g_ref, b_ref, bg_ref, wd_ref,
                    r_ref, xb_ref, wdb_ref, *, tm, sub):
    wdb_ref[...] = wd_ref[...].astype(BF16)
    for r in range(0, tm, sub):
        rows = slice(r, r + sub)
        z = DEEPNORM_ALPHA * x_ref[rows, :] + jnp.dot(cat_ref[rows, :], wo_ref[...], preferred_element_type=F32)
        x1 = _layer_norm(z, g_ref[...], b_ref[...])
        x1b = x1.astype(BF16)
        gate = jnp.dot(x1b, wg_ref[...], preferred_element_type=F32) + bg_ref[...]
        ple = jnp.dot(p_ref[rows, :].astype(BF16), wp_ref[...], preferred_element_type=F32) * _sigmoid(gate)
        r_ref[rows, :] = DEEPNORM_ALPHA * x1 + ple
        xb_ref[rows, :] = x1b


def _mix_out(cat, x, p, layer, w_out_b, w_gate_b, w_proj_b, ln_g, ln_b, b_gate, w_down, tm, sub):
    s = x.shape[0]
    row = lambda cols: pl.BlockSpec((tm, cols), lambda i: (i, 0))
    resident = lambda rows_: pl.BlockSpec((rows_, D_MODEL), lambda i: (0, 0), pipeline_mode=pl.Buffered(1))
    slab = _slab_rows(D_FF, s // tm)
    vmem = (2 * D_MODEL * D_MODEL + PLE_DIM * D_MODEL) * 2 + 2 * slab * D_MODEL * 6 \
        + 2 * tm * (D_MODEL * 2 + D_MODEL * 4 + PLE_DIM * 4 + D_MODEL * 4 + D_MODEL * 2) + COMPILER_VMEM["mix_out"]
    return pl.pallas_call(
        functools.partial(_mix_out_kernel, tm=tm, sub=sub),
        grid=(s // tm,),
        in_specs=[row(D_MODEL), row(D_MODEL),
                  pl.BlockSpec((None, tm, PLE_DIM), lambda i: (layer, i, 0)),
                  resident(D_MODEL), resident(D_MODEL), resident(PLE_DIM),
                  _layer_vec(layer, D_MODEL), _layer_vec(layer, D_MODEL), _layer_vec(layer, D_MODEL),
                  pl.BlockSpec((None, slab, D_MODEL), lambda i: (layer, i, 0))],
        out_specs=[row(D_MODEL), row(D_MODEL), pl.BlockSpec((slab, D_MODEL), lambda i: (i, 0))],
        out_shape=[jax.ShapeDtypeStruct((s, D_MODEL), F32), jax.ShapeDtypeStruct((s, D_MODEL), BF16),
                   jax.ShapeDtypeStruct((D_FF, D_MODEL), BF16)],
        compiler_params=_params(("arbitrary",), vmem),
        name="mix_out",
    )(cat, x, p, w_out_b, w_gate_b, w_proj_b, ln_g, ln_b, b_gate, w_down)


FFN_TN = 512
FFN_BLOCKS = D_FF // FFN_TN


def _ffn_up_kernel(x_ref, wg_ref, wu_ref, cwg_ref, cwu_ref, cbg_ref, cbu_ref, o_ref, carry_ref, wgb_ref, wub_ref,
                   *, tm, sub):
    @pl.when(pl.program_id(1) == 0)
    def _():
        wgb_ref[...] = wg_ref[...].astype(BF16)
        wub_ref[...] = wu_ref[...].astype(BF16)
        carry_ref[...] = jnp.zeros_like(carry_ref)

    def conv(prev, h, cw_ref, cb_ref):
        ext = jnp.concatenate([prev, h], axis=0)
        y = (ext * cw_ref[2:3, :] + pltpu.roll(ext, 1, 0) * cw_ref[1:2, :]
             + pltpu.roll(ext, 2, 0) * cw_ref[0:1, :] + cb_ref[...])
        return y[SUBLANES:]

    prev_g = carry_ref[0]
    prev_u = carry_ref[1]
    for r in range(0, tm, sub):
        x = x_ref[r:r + sub, :]
        hg = jnp.dot(x, wgb_ref[...], preferred_element_type=F32)
        hu = jnp.dot(x, wub_ref[...], preferred_element_type=F32)
        yg = conv(prev_g, hg, cwg_ref, cbg_ref)
        yu = conv(prev_u, hu, cwu_ref, cbu_ref)
        prev_g = hg[sub - SUBLANES:sub]
        prev_u = hu[sub - SUBLANES:sub]
        o_ref[r:r + sub, :] = (_silu(yg) * yu).astype(BF16)
    carry_ref[0] = prev_g
    carry_ref[1] = prev_u


def _ffn_up(xb, w_up, conv_w, conv_b, layer, tm, sub):
    s = xb.shape[0]
    nb = FFN_BLOCKS
    vmem = 2 * (tm * D_MODEL * 2 + 2 * D_MODEL * FFN_TN * 4 + tm * FFN_TN * 2) + 2 * D_MODEL * FFN_TN * 2 \
        + COMPILER_VMEM["ffn_up"]
    wspec = lambda off: pl.BlockSpec((None, D_MODEL, FFN_TN), lambda j, i: (layer, 0, j + off))
    cwspec = lambda off: pl.BlockSpec((None, CONV_WIDTH, FFN_TN), lambda j, i: (layer, 0, j + off))
    cbspec = lambda off: pl.BlockSpec((None, 1, FFN_TN), lambda j, i: (layer, 0, j + off))
    return pl.pallas_call(
        functools.partial(_ffn_up_kernel, tm=tm, sub=sub),
        grid=(nb, s // tm),
        in_specs=[pl.BlockSpec((tm, D_MODEL), lambda j, i: (i, 0)),
                  wspec(0), wspec(nb), cwspec(0), cwspec(nb), cbspec(0), cbspec(nb)],
        out_specs=pl.BlockSpec((tm, FFN_TN), lambda j, i: (i, j)),
        out_shape=jax.ShapeDtypeStruct((s, D_FF), BF16),
        scratch_shapes=[pltpu.VMEM((2, SUBLANES, FFN_TN), F32),
                        pltpu.VMEM((D_MODEL, FFN_TN), BF16), pltpu.VMEM((D_MODEL, FFN_TN), BF16)],
        compiler_params=_params(("arbitrary", "arbitrary"), vmem),
        name="ffn_up",
    )(xb, w_up, w_up, conv_w, conv_w, conv_b, conv_b)


def _ffn_down_kernel(a_ref, r_ref, w_ref, g_ref, b_ref, x_ref, xb_ref, *, tm, sub):
    for r in range(0, tm, sub):
        rows = slice(r, r + sub)
        z = r_ref[rows, :] + jnp.dot(a_ref[rows, :], w_ref[...], preferred_element_type=F32)
        x2 = _layer_norm(z, g_ref[...], b_ref[...])
        x_ref[rows, :] = x2
        xb_ref[rows, :] = x2.astype(BF16)


def _ffn_down(act, r, w_down_b, ln_g, ln_b, layer, tm, sub):
    s = r.shape[0]
    row = lambda cols: pl.BlockSpec((tm, cols), lambda i: (i, 0))
    vmem = D_FF * D_MODEL * 2 + 2 * tm * (D_FF * 2 + D_MODEL * 4 + D_MODEL * 4 + D_MODEL * 2) + COMPILER_VMEM["ffn_down"]
    return pl.pallas_call(
        functools.partial(_ffn_down_kernel, tm=tm, sub=sub),
        grid=(s // tm,),
        in_specs=[row(D_FF), row(D_MODEL),
                  pl.BlockSpec((D_FF, D_MODEL), lambda i: (0, 0), pipeline_mode=pl.Buffered(1)),
                  _layer_vec(layer, D_MODEL), _layer_vec(layer, D_MODEL)],
        out_specs=[row(D_MODEL), row(D_MODEL)],
        out_shape=[jax.ShapeDtypeStruct((s, D_MODEL), F32), jax.ShapeDtypeStruct((s, D_MODEL), BF16)],
        compiler_params=_params(("arbitrary",), vmem),
        name="ffn_down",
    )(act, r, w_down_b, ln_g, ln_b)


def _tiles(s):
    pick = lambda tm, sub: (min(tm, s), min(sub, s))
    return dict(tables=pick(1024, 1024), ret_proj=pick(2048, 1024), att_proj=pick(1024, 256),
                mixer=pick(1024, 128), mix_out=pick(512, 512), ffn_up=pick(1024, 1024), ffn_down=pick(512, 256))


def kernel(x, p, positions, w_in, w_out, ret_norm_g, ret_norm_b, attn_sinks, ln1_g, ln1_b, w_ffn_up,
           ffn_conv_w, ffn_conv_b, w_ffn_down, ln2_g, ln2_b, w_ple_gate, b_ple_gate, w_ple_proj):
    batch, s, d = x.shape
    assert batch == 1 and d == D_MODEL and s % RET_CHUNK == 0
    t = _tiles(s)
    assert all(s % tm == 0 and tm % sub == 0 for tm, sub in t.values())

    xf = x.reshape(s, d)
    pf = p.reshape(DEPTH, s, PLE_DIM)
    cos_r, sin_r, cos_a, sin_a, sin_b, xb = _rot_tables(positions.reshape(s), xf, t["tables"][0])
    ret_consts, k_decay = _retention_constants()
    vecs = lambda v: v.reshape(DEPTH, 1, -1)
    gn_g, gn_b, g1, b1, g2, b2 = (vecs(v) for v in (ret_norm_g, ret_norm_b, ln1_g, ln1_b, ln2_g, ln2_b))
    bg, conv_b = vecs(b_ple_gate), vecs(ffn_conv_b)

    for l in range(DEPTH):
        rp, kdec, w_att_b = _ret_proj(xb, w_in, l, cos_r, sin_r, k_decay, *t["ret_proj"])
        ap = _att_proj(xb, w_att_b, cos_a, sin_a, sin_b, *t["att_proj"])
        cat, w_out_b, w_gate_b, w_proj_b = _mixer(rp, kdec, ap, ret_consts, gn_g, gn_b, attn_sinks,
                                                  w_out, w_ple_gate, w_ple_proj, l, t["mixer"][0])
        r, xb, w_down_b = _mix_out(cat, xf, pf, l, w_out_b, w_gate_b, w_proj_b, g1, b1, bg, w_ffn_down,
                                   *t["mix_out"])
        act = _ffn_up(xb, w_ffn_up, ffn_conv_w, conv_b, l, *t["ffn_up"])
        xf, xb = _ffn_down(act, r, w_down_b, g2, b2, l, *t["ffn_down"])
    return xf.reshape(batch, s, d)
```

```python
import functools

import jax
import jax.numpy as jnp
from jax import lax
from jax.experimental import pallas as pl
from jax.experimental.pallas import tpu as pltpu

F32 = jnp.float32
BF16 = jnp.bfloat16

D_MODEL = 2048
DEPTH = 4
PLE_DIM = 256
RET_WIDTH = 1024
RET_HEADS = 4
RET_HEAD_DIM = 256
RET_CHUNK = 128
RET_ROT_BASE = 10000.0
ATT_HEAD_DIM = 64
ATT_WIDTH = 1024
ATT_HEADS = 16
ATT_KV_HEADS = 2
ATT_GROUP = ATT_HEADS // ATT_KV_HEADS
KV_WIDTH = 128
WINDOW = 128
ROPE_THETA = 10000.0
D_FF = 5632
CONV_WIDTH = 3
LN_EPS = 1e-5
GN_EPS = 1e-6
DEEPNORM_ALPHA = (2 * DEPTH) ** 0.25

LANES = 128
SUBLANES = 8
BF16_SUBLANE_ROWS = 16
MXU_DIM = 256
MIB = 1 << 20
VMEM_LIMIT_CAP = 58 * MIB
COMPILER_VMEM = dict(rot_tables=16 * MIB, ret_proj=12 * MIB, att_proj=12 * MIB, mixer=14 * MIB, mix_out=12 * MIB,
                     ffn_up=12 * MIB, ffn_down=8 * MIB)

RET_COLS = 4 * RET_WIDTH
ATT_IN_COLS = ATT_WIDTH + 2 * KV_WIDTH
ATT_OUT_COLS = ATT_WIDTH + 4 * KV_WIDTH
HALF = LANES // 2
LOG2_E = 1.4426950408889634
ATT_SCORE_SCALE = ATT_HEAD_DIM ** -0.5 * LOG2_E


def _params(semantics, vmem_bytes):
    return pltpu.CompilerParams(dimension_semantics=semantics,
                                vmem_limit_bytes=int(min(vmem_bytes, VMEM_LIMIT_CAP)))


def _layer_norm(z, g, b):
    t = z - jnp.mean(z[:, :LANES], axis=-1, keepdims=True)
    m1 = jnp.mean(t, axis=-1, keepdims=True)
    m2 = jnp.mean(t * t, axis=-1, keepdims=True)
    return (t - m1) * lax.rsqrt(m2 - m1 * m1 + LN_EPS) * g + b


def _sigmoid(v):
    return 0.5 + 0.5 * jnp.tanh(0.5 * v)


def _silu(v):
    t = 0.5 * v
    return t + t * jnp.tanh(t)


def _slab_rows(rows, steps):
    assert rows % steps == 0 and (rows // steps) % BF16_SUBLANE_ROWS == 0, (rows, steps)
    return rows // steps


def _layer_vec(layer, cols):
    return pl.BlockSpec((None, 1, cols), lambda *_: (layer, 0, 0))


def _rot_tables_kernel(pos_ref, fr_ref, fa_ref, x_ref, cr_ref, sr_ref, ca_ref, sa_ref, sb_ref, xb_ref):
    xb_ref[...] = x_ref[...].astype(BF16)
    pos = pos_ref[...].astype(F32)
    ang_r = pos * fr_ref[...]
    cr_ref[...] = jnp.cos(ang_r)
    sr_ref[...] = jnp.sin(ang_r)
    ang_a = pos * fa_ref[...]
    c = jnp.cos(ang_a)
    s = jnp.sin(ang_a)
    lane = lax.broadcasted_iota(jnp.int32, c.shape, 1)
    first_half = (lane & (ATT_HEAD_DIM // 2)) == 0
    ca_ref[...] = c
    sa_ref[...] = jnp.where(first_half, -s, 0.0)
    sb_ref[...] = jnp.where(first_half, 0.0, s)


def _rot_tables(positions, xf, tm):
    s = positions.shape[0]
    ret_inv_freq = 1.0 / (RET_ROT_BASE ** jnp.linspace(0.0, 1.0, RET_HEAD_DIM // 2, dtype=F32))
    att_inv_freq = ROPE_THETA ** (-jnp.arange(0, ATT_HEAD_DIM, 2, dtype=F32) / ATT_HEAD_DIM)
    fr = ret_inv_freq.reshape(1, LANES)
    fa = jnp.tile(att_inv_freq, LANES // (ATT_HEAD_DIM // 2)).reshape(1, LANES)
    tab = jax.ShapeDtypeStruct((s, LANES), F32)
    row = pl.BlockSpec((tm, LANES), lambda i: (i, 0))
    const = pl.BlockSpec((1, LANES), lambda i: (0, 0))
    wide = pl.BlockSpec((tm, D_MODEL), lambda i: (i, 0))
    return pl.pallas_call(
        _rot_tables_kernel,
        grid=(s // tm,),
        in_specs=[pl.BlockSpec((tm, 1), lambda i: (i, 0)), const, const, wide],
        out_specs=[row] * 5 + [wide],
        out_shape=[tab] * 5 + [jax.ShapeDtypeStruct((s, D_MODEL), BF16)],
        compiler_params=_params(("arbitrary",), 2 * tm * D_MODEL * 6 + COMPILER_VMEM["rot_tables"]),
        name="rot_tables",
    )(positions.reshape(s, 1), fr, fa, xf)


RET_TN = 512
RET_BLOCKS_PER_PART = RET_WIDTH // RET_TN
RET_HEADS_PER_BLOCK = RET_TN // RET_HEAD_DIM


def _ret_proj_kernel(x_ref, w_ref, c_ref, s_ref, kd_ref, wrow_ref, o_ref, kdec_ref, watt_ref, wb_ref, *, tm, sub):
    j = pl.program_id(1)
    part = j // RET_BLOCKS_PER_PART
    watt_ref[...] = wrow_ref[:, RET_COLS:].astype(BF16)

    def sub_blocks():
        wb_ref[...] = w_ref[...].astype(BF16)
        for r in range(0, tm, sub):
            rows = slice(r, r + sub)
            yield rows, jnp.dot(x_ref[rows, :], wb_ref[...], preferred_element_type=F32)

    def rotary(rows, acc, scale):
        c = c_ref[rows, :] * scale
        s = s_ref[rows, :] * scale
        for h in range(RET_HEADS_PER_BLOCK):
            lo = h * RET_HEAD_DIM
            mid = lo + RET_HEAD_DIM // 2
            a1 = acc[:, lo:mid]
            a2 = acc[:, mid:lo + RET_HEAD_DIM]
            yield h, lo, a1 * c - a2 * s
            yield h, mid, a1 * s + a2 * c

    @pl.when(part == 0)
    def _():
        for rows, acc in sub_blocks():
            for _, col, val in rotary(rows, acc, 1.0):
                o_ref[rows, col:col + LANES] = val.astype(BF16)

    @pl.when(part == 1)
    def _():
        for rows, acc in sub_blocks():
            for h, col, val in rotary(rows, acc, RET_HEAD_DIM ** -0.5):
                o_ref[rows, col:col + LANES] = val.astype(BF16)
                chunks = val.reshape(sub // RET_CHUNK, RET_CHUNK, LANES) * kd_ref[h][None]
                kdec_ref[rows, col:col + LANES] = chunks.reshape(sub, LANES).astype(BF16)

    @pl.when(part == 2)
    def _():
        for rows, acc in sub_blocks():
            o_ref[rows, :] = acc.astype(BF16)

    @pl.when(part == 3)
    def _():
        for rows, acc in sub_blocks():
            o_ref[rows, :] = _silu(acc).astype(BF16)


def _ret_proj(xb, w_in, layer, cos_r, sin_r, k_decay, tm, sub):
    s = xb.shape[0]
    vmem = 2 * (tm * D_MODEL * 2 + D_MODEL * RET_TN * 4 + 2 * tm * RET_TN * 2 + 2 * tm * LANES * 4) \
        + D_MODEL * RET_TN * 2 + COMPILER_VMEM["ret_proj"]
    k_block = lambda j: jnp.clip(j - RET_BLOCKS_PER_PART, 0, RET_BLOCKS_PER_PART - 1)
    nj = RET_COLS // RET_TN
    slab = _slab_rows(D_MODEL, (s // tm) * nj)
    return pl.pallas_call(
        functools.partial(_ret_proj_kernel, tm=tm, sub=sub),
        grid=(s // tm, nj),
        in_specs=[pl.BlockSpec((tm, D_MODEL), lambda i, j: (i, 0)),
                  pl.BlockSpec((None, D_MODEL, RET_TN), lambda i, j: (layer, 0, j)),
                  pl.BlockSpec((tm, LANES), lambda i, j: (i, 0)),
                  pl.BlockSpec((tm, LANES), lambda i, j: (i, 0)),
                  pl.BlockSpec((RET_HEADS_PER_BLOCK, RET_CHUNK, LANES), lambda i, j: (k_block(j), 0, 0)),
                  pl.BlockSpec((None, slab, RET_COLS + ATT_IN_COLS), lambda i, j: (layer, i * nj + j, 0))],
        out_specs=[pl.BlockSpec((tm, RET_TN), lambda i, j: (i, j)),
                   pl.BlockSpec((tm, RET_TN), lambda i, j: (i, k_block(j))),
                   pl.BlockSpec((slab, ATT_IN_COLS), lambda i, j: (i * nj + j, 0))],
        out_shape=[jax.ShapeDtypeStruct((s, RET_COLS), BF16), jax.ShapeDtypeStruct((s, RET_WIDTH), BF16),
                   jax.ShapeDtypeStruct((D_MODEL, ATT_IN_COLS), BF16)],
        scratch_shapes=[pltpu.VMEM((D_MODEL, RET_TN), BF16)],
        compiler_params=_params(("arbitrary", "arbitrary"), vmem),
        name="ret_proj",
    )(xb, w_in, cos_r, sin_r, k_decay, w_in)


ATT_W_BLOCKS = ATT_IN_COLS // MXU_DIM


def _att_proj_kernel(x_ref, *refs, tm, sub):
    w_refs = refs[:ATT_W_BLOCKS]
    c_ref, sa_ref, sb_ref, o_ref = refs[ATT_W_BLOCKS:]

    for r in range(0, tm, sub):
        rows = slice(r, r + sub)
        c = c_ref[rows, :]
        sa = sa_ref[rows, :]
        sb = sb_ref[rows, :]

        def rot(a):
            return (a * c + pltpu.roll(a, LANES - ATT_HEAD_DIM // 2, 1) * sa
                    + pltpu.roll(a, ATT_HEAD_DIM // 2, 1) * sb)

        x = x_ref[rows, :]
        for wb in range(ATT_W_BLOCKS):
            acc = jnp.dot(x, w_refs[wb][...], preferred_element_type=F32)
            for half in range(MXU_DIM // LANES):
                col = wb * MXU_DIM + half * LANES
                a = acc[:, half * LANES:(half + 1) * LANES]
                if col < ATT_WIDTH:
                    o_ref[rows, col:col + LANES] = (rot(a) * ATT_SCORE_SCALE).astype(BF16)
                elif col < ATT_WIDTH + KV_WIDTH:
                    k = rot(a)
                    o_ref[rows, col:col + LANES] = k.astype(BF16)
                    o_ref[rows, col + 2 * LANES:col + 3 * LANES] = pltpu.roll(k, HALF, 1).astype(BF16)
                else:
                    o_ref[rows, col:col + LANES] = a.astype(BF16)
                    o_ref[rows, col + 2 * LANES:col + 3 * LANES] = pltpu.roll(a, HALF, 1).astype(BF16)


def _att_proj(xb, w_att_b, cos_a, sin_a, sin_b, tm, sub):
    s = xb.shape[0]
    vmem = 2 * (tm * D_MODEL * 2 + D_MODEL * ATT_IN_COLS * 2 + tm * ATT_OUT_COLS * 2 + 3 * tm * LANES * 4) \
        + COMPILER_VMEM["att_proj"]
    row = pl.BlockSpec((tm, LANES), lambda i: (i, 0))
    w_specs = [pl.BlockSpec((D_MODEL, MXU_DIM), functools.partial(lambda i, wb: (0, wb), wb=wb))
               for wb in range(ATT_W_BLOCKS)]
    return pl.pallas_call(
        functools.partial(_att_proj_kernel, tm=tm, sub=sub),
        grid=(s // tm,),
        in_specs=[pl.BlockSpec((tm, D_MODEL), lambda i: (i, 0))] + w_specs + [row, row, row],
        out_specs=pl.BlockSpec((tm, ATT_OUT_COLS), lambda i: (i, 0)),
        out_shape=jax.ShapeDtypeStruct((s, ATT_OUT_COLS), BF16),
        compiler_params=_params(("arbitrary",), vmem),
        name="att_proj",
    )(xb, *([w_att_b] * ATT_W_BLOCKS), cos_a, sin_a, sin_b)


def _mixer_kernel(cd_ref, sink_ref, r_ref, kdec_ref, a_ref, dec_ref, qd_ref, g_ref, b_ref, wo_ref, wg_ref, wp_ref,
                  o_ref, wob_ref, wgb_ref, wpb_ref, state_ref, kv_ref, *, tm, layer):
    i = pl.program_id(0)
    wob_ref[...] = wo_ref[...].astype(BF16)
    wgb_ref[...] = wg_ref[...].astype(BF16)
    wpb_ref[...] = wp_ref[...].astype(BF16)
    n_chunks = tm // RET_CHUNK
    kv_cols = 4 * KV_WIDTH

    @pl.when(i == 0)
    def _():
        state_ref[...] = jnp.zeros_like(state_ref)
        kv_ref[0:WINDOW, :] = jnp.zeros((WINDOW, kv_cols), BF16)

    @pl.when(i > 0)
    def _():
        kv_ref[0:WINDOW, :] = kv_ref[tm:tm + WINDOW, :]

    kv_ref[WINDOW:tm + WINDOW, :] = a_ref[:, ATT_WIDTH:ATT_WIDTH + kv_cols]

    contract_last = (((1,), (1,)), ((), ()))
    contract_first = (((0,), (0,)), ((), ()))
    qi = lax.broadcasted_iota(jnp.int32, (WINDOW, 2 * WINDOW), 0)
    kj = lax.broadcasted_iota(jnp.int32, (WINDOW, 2 * WINDOW), 1)
    band = (kj > qi) & (kj <= qi + WINDOW)
    low_lanes = lax.broadcasted_iota(jnp.int32, (2 * WINDOW, LANES), 1) < HALF

    def chunk(c, carry):
        r0 = pl.multiple_of(c * RET_CHUNK, RET_CHUNK)
        rows = pl.ds(r0, RET_CHUNK)

        for h in range(RET_HEADS):
            lo = h * RET_HEAD_DIM
            hi = lo + RET_HEAD_DIM
            q = r_ref[rows, lo:hi]
            k = r_ref[rows, RET_WIDTH + lo:RET_WIDTH + hi]
            v = r_ref[rows, 2 * RET_WIDTH + lo:2 * RET_WIDTH + hi]
            silu_gate = r_ref[rows, 3 * RET_WIDTH + lo:3 * RET_WIDTH + hi].astype(F32)
            st = state_ref[h]
            sc = lax.dot_general(q, k, contract_last, preferred_element_type=F32) * dec_ref[h]
            inner = jnp.dot(sc.astype(BF16), v, preferred_element_type=F32)
            cross = jnp.dot(q, st.astype(BF16), preferred_element_type=F32) * qd_ref[h]
            state_ref[h] = st * cd_ref[h] + lax.dot_general(kdec_ref[rows, lo:hi], v, contract_first,
                                                            preferred_element_type=F32)
            y = inner + cross
            mu = jnp.mean(y, axis=-1, keepdims=True)
            d = y - mu
            var = jnp.mean(d * d, axis=-1, keepdims=True)
            yn = d * lax.rsqrt(var + GN_EPS) * g_ref[:, lo:hi] + b_ref[:, lo:hi]
            o_ref[rows, lo:hi] = (silu_gate * yn).astype(BF16)

        kk = kv_ref[pl.ds(r0, 2 * WINDOW), :]
        k_nat = kk[:, 0:LANES]
        v_nat = kk[:, LANES:2 * LANES]
        k_swp = kk[:, 2 * LANES:3 * LANES]
        v_swp = kk[:, 3 * LANES:4 * LANES]
        zero = jnp.zeros_like(k_nat)
        k_ext = ((jnp.where(low_lanes, k_nat, zero), jnp.where(low_lanes, zero, k_swp)),
                 (jnp.where(low_lanes, k_swp, zero), jnp.where(low_lanes, zero, k_nat)))
        v_ext = ((jnp.where(low_lanes, v_nat, zero), jnp.where(low_lanes, zero, v_swp)),
                 (jnp.where(low_lanes, v_swp, zero), jnp.where(low_lanes, zero, v_nat)))
        first_block = (i * n_chunks + c) == 0
        mask = band & ((kj >= WINDOW) | jnp.logical_not(first_block))
        pairs_per_kv = ATT_GROUP // 2
        for kvh in range(ATT_KV_HEADS):
            for pr in range(pairs_per_kv):
                pair = kvh * pairs_per_kv + pr
                q_pair = a_ref[rows, pair * LANES:(pair + 1) * LANES]
                o_pair = None
                for par in range(2):
                    sink = sink_ref[layer, 2 * pair + par] * LOG2_E
                    sc = lax.dot_general(q_pair, k_ext[kvh][par], contract_last, preferred_element_type=F32)
                    sc = jnp.where(mask, sc, -jnp.inf)
                    m = jnp.maximum(jnp.max(sc, axis=-1, keepdims=True), sink)
                    p = jnp.exp2(sc - m)
                    den = jnp.sum(p, axis=-1, keepdims=True) + jnp.exp2(sink - m)
                    o = jnp.dot(p.astype(BF16), v_ext[kvh][par], preferred_element_type=F32) * (1.0 / den)
                    o_pair = o if o_pair is None else o_pair + o
                o_ref[rows, RET_WIDTH + pair * LANES:RET_WIDTH + (pair + 1) * LANES] = o_pair.astype(BF16)
        return carry

    lax.fori_loop(0, n_chunks, chunk, 0, unroll=4)


def _mixer(rp, kdec, ap, ret_consts, gn_g, gn_b, sinks, w_out, w_gate, w_proj, layer, tm):
    s = rp.shape[0]
    decay, q_decay, chunk_decay = ret_consts
    smem = pl.BlockSpec(memory_space=pltpu.SMEM)
    const3 = lambda shape: pl.BlockSpec(shape, lambda i: (0, 0, 0))
    steps = s // tm
    slab_d, slab_p = _slab_rows(D_MODEL, steps), _slab_rows(PLE_DIM, steps)
    w_in_spec = lambda rows: pl.BlockSpec((None, rows, D_MODEL), lambda i: (layer, i, 0))
    w_out_spec = lambda rows: pl.BlockSpec((rows, D_MODEL), lambda i: (i, 0))
    w_shape = lambda rows: jax.ShapeDtypeStruct((rows, D_MODEL), BF16)
    vmem = 2 * (tm * (RET_COLS + RET_WIDTH) * 2 + tm * ATT_OUT_COLS * 2 + tm * D_MODEL * 2) \
        + 2 * (2 * slab_d + slab_p) * D_MODEL * 6 \
        + 2 * 4 * (decay.size + q_decay.size) \
        + RET_HEADS * RET_HEAD_DIM * RET_HEAD_DIM * 4 + (tm + WINDOW) * 4 * KV_WIDTH * 2 + COMPILER_VMEM["mixer"]
    return pl.pallas_call(
        functools.partial(_mixer_kernel, tm=tm, layer=layer),
        grid=(s // tm,),
        in_specs=[smem, smem,
                  pl.BlockSpec((tm, RET_COLS), lambda i: (i, 0)),
                  pl.BlockSpec((tm, RET_WIDTH), lambda i: (i, 0)),
                  pl.BlockSpec((tm, ATT_OUT_COLS), lambda i: (i, 0)),
                  const3(decay.shape), const3(q_decay.shape),
                  _layer_vec(layer, RET_WIDTH), _layer_vec(layer, RET_WIDTH),
                  w_in_spec(slab_d), w_in_spec(slab_d), w_in_spec(slab_p)],
        out_specs=[pl.BlockSpec((tm, D_MODEL), lambda i: (i, 0)),
                   w_out_spec(slab_d), w_out_spec(slab_d), w_out_spec(slab_p)],
        out_shape=[jax.ShapeDtypeStruct((s, D_MODEL), BF16), w_shape(D_MODEL), w_shape(D_MODEL), w_shape(PLE_DIM)],
        scratch_shapes=[pltpu.VMEM((RET_HEADS, RET_HEAD_DIM, RET_HEAD_DIM), F32),
                        pltpu.VMEM((tm + WINDOW, 4 * KV_WIDTH), BF16)],
        compiler_params=_params(("arbitrary",), vmem),
        name="mixer",
    )(chunk_decay, sinks, rp, kdec, ap, decay, q_decay, gn_g, gn_b, w_out, w_gate, w_proj)


def _retention_constants():
    c = RET_CHUNK
    log_g = jnp.log1p(-jnp.exp2(-5.0 - jnp.arange(RET_HEADS, dtype=F32)))
    idx = jnp.arange(c, dtype=F32)
    diff = idx[:, None] - idx[None, :]
    decay = jnp.where(diff[None] >= 0, jnp.exp(log_g[:, None, None] * jnp.maximum(diff, 0.0)[None]), 0.0)
    q_decay = jnp.exp(log_g[:, None] * (idx[None, :] + 1.0))
    k_decay = jnp.exp(log_g[:, None] * (c - 1.0 - idx[None, :]))
    chunk_decay = jnp.exp(log_g * c)
    wide = lambda t, lanes: jnp.broadcast_to(t[:, :, None], (RET_HEADS, c, lanes))
    return (decay, wide(q_decay, RET_HEAD_DIM), chunk_decay), wide(k_decay, LANES)


def _mix_out_kernel(cat_ref, x_ref, p_ref, wo_ref, wg_ref, wp_ref, g_ref, b_ref, bg_ref, wd_ref,
                    r_ref, xb_ref, wdb_ref, *, tm, sub):
    wdb_ref[...] = wd_ref[...].astype(BF16)
    for r in range(0, tm, sub):
        rows = slice(r, r + sub)
        z = DEEPNORM_ALPHA * x_ref[rows, :] + jnp.dot(cat_ref[rows, :], wo_ref[...], preferred_element_type=F32)
        x1 = _layer_norm(z, g_ref[...], b_ref[...])
        x1b = x1.astype(BF16)
        gate = jnp.dot(x1b, wg_ref[...], preferred_element_type=F32) + bg_ref[...]
        ple = jnp.dot(p_ref[rows, :].astype(BF16), wp_ref[...], preferred_element_type=F32) * _sigmoid(gate)
        r_ref[rows, :] = DEEPNORM_ALPHA * x1 + ple
        xb_ref[rows, :] = x1b


def _mix_out(cat, x, p, layer, w_out_b, w_gate_b, w_proj_b, ln_g, ln_b, b_gate, w_down, tm, sub):
    s = x.shape[0]
    row = lambda cols: pl.BlockSpec((tm, cols), lambda i: (i, 0))
    resident = lambda rows_: pl.BlockSpec((rows_, D_MODEL), lambda i: (0, 0), pipeline_mode=pl.Buffered(1))
    slab = _slab_rows(D_FF, s // tm)
    vmem = (2 * D_MODEL * D_MODEL + PLE_DIM * D_MODEL) * 2 + 2 * slab * D_MODEL * 6 \
        + 2 * tm * (D_MODEL * 2 + D_MODEL * 4 + PLE_DIM * 4 + D_MODEL * 4 + D_MODEL * 2) + COMPILER_VMEM["mix_out"]
    return pl.pallas_call(
        functools.partial(_mix_out_kernel, tm=tm, sub=sub),
        grid=(s // tm,),
        in_specs=[row(D_MODEL), row(D_MODEL),
                  pl.BlockSpec((None, tm, PLE_DIM), lambda i: (layer, i, 0)),
                  resident(D_MODEL), resident(D_MODEL), resident(PLE_DIM),
                  _layer_vec(layer, D_MODEL), _layer_vec(layer, D_MODEL), _layer_vec(layer, D_MODEL),
                  pl.BlockSpec((None, slab, D_MODEL), lambda i: (layer, i, 0))],
        out_specs=[row(D_MODEL), row(D_MODEL), pl.BlockSpec((slab, D_MODEL), lambda i: (i, 0))],
        out_shape=[jax.ShapeDtypeStruct((s, D_MODEL), F32), jax.ShapeDtypeStruct((s, D_MODEL), BF16),
                   jax.ShapeDtypeStruct((D_FF, D_MODEL), BF16)],
        compiler_params=_params(("arbitrary",), vmem),
        name="mix_out",
    )(cat, x, p, w_out_b, w_gate_b, w_proj_b, ln_g, ln_b, b_gate, w_down)


FFN_TN = 512
FFN_BLOCKS = D_FF // FFN_TN


def _ffn_up_kernel(x_ref, wg_ref, wu_ref, cwg_ref, cwu_ref, cbg_ref, cbu_ref, o_ref, carry_ref, wcat_ref,
                   *, tm, sub):
    @pl.when(pl.program_id(1) == 0)
    def _():
        wcat_ref[:, 0:FFN_TN] = wg_ref[...].astype(BF16)
        wcat_ref[:, FFN_TN:2 * FFN_TN] = wu_ref[...].astype(BF16)
        carry_ref[...] = jnp.zeros_like(carry_ref)

    def conv(prev, h, cw_ref, cb_ref):
        ext = jnp.concatenate([prev, h], axis=0)
        y = (ext * cw_ref[2:3, :] + pltpu.roll(ext, 1, 0) * cw_ref[1:2, :]
             + pltpu.roll(ext, 2, 0) * cw_ref[0:1, :] + cb_ref[...])
        return y[SUBLANES:]

    prev_g = carry_ref[0]
    prev_u = carry_ref[1]
    for r in range(0, tm, sub):
        h = jnp.dot(x_ref[r:r + sub, :], wcat_ref[...], preferred_element_type=F32)
        hg = h[:, 0:FFN_TN]
        hu = h[:, FFN_TN:2 * FFN_TN]
        yg = conv(prev_g, hg, cwg_ref, cbg_ref)
        yu = conv(prev_u, hu, cwu_ref, cbu_ref)
        prev_g = hg[sub - SUBLANES:sub]
        prev_u = hu[sub - SUBLANES:sub]
        o_ref[r:r + sub, :] = (_silu(yg) * yu).astype(BF16)
    carry_ref[0] = prev_g
    carry_ref[1] = prev_u


def _ffn_up(xb, w_up, conv_w, conv_b, layer, tm, sub):
    s = xb.shape[0]
    nb = FFN_BLOCKS
    vmem = 2 * (tm * D_MODEL * 2 + 2 * D_MODEL * FFN_TN * 4 + tm * FFN_TN * 2) + 2 * D_MODEL * FFN_TN * 2 \
        + COMPILER_VMEM["ffn_up"]
    wspec = lambda off: pl.BlockSpec((None, D_MODEL, FFN_TN), lambda j, i: (layer, 0, j + off))
    cwspec = lambda off: pl.BlockSpec((None, CONV_WIDTH, FFN_TN), lambda j, i: (layer, 0, j + off))
    cbspec = lambda off: pl.BlockSpec((None, 1, FFN_TN), lambda j, i: (layer, 0, j + off))
    return pl.pallas_call(
        functools.partial(_ffn_up_kernel, tm=tm, sub=sub),
        grid=(nb, s // tm),
        in_specs=[pl.BlockSpec((tm, D_MODEL), lambda j, i: (i, 0)),
                  wspec(0), wspec(nb), cwspec(0), cwspec(nb), cbspec(0), cbspec(nb)],
        out_specs=pl.BlockSpec((tm, FFN_TN), lambda j, i: (i, j)),
        out_shape=jax.ShapeDtypeStruct((s, D_FF), BF16),
        scratch_shapes=[pltpu.VMEM((2, SUBLANES, FFN_TN), F32),
                        pltpu.VMEM((D_MODEL, 2 * FFN_TN), BF16)],
        compiler_params=_params(("arbitrary", "arbitrary"), vmem),
        name="ffn_up",
    )(xb, w_up, w_up, conv_w, conv_w, conv_b, conv_b)


def _ffn_down_kernel(a_ref, r_ref, w_ref, g_ref, b_ref, x_ref, xb_ref, *, tm, sub):
    for r in range(0, tm, sub):
        rows = slice(r, r + sub)
        z = r_ref[rows, :] + jnp.dot(a_ref[rows, :], w_ref[...], preferred_element_type=F32)
        x2 = _layer_norm(z, g_ref[...], b_ref[...])
        x_ref[rows, :] = x2
        xb_ref[rows, :] = x2.astype(BF16)


def _ffn_down(act, r, w_down_b, ln_g, ln_b, layer, tm, sub):
    s = r.shape[0]
    row = lambda cols: pl.BlockSpec((tm, cols), lambda i: (i, 0))
    vmem = D_FF * D_MODEL * 2 + 2 * tm * (D_FF * 2 + D_MODEL * 4 + D_MODEL * 4 + D_MODEL * 2) + COMPILER_VMEM["ffn_down"]
    return pl.pallas_call(
        functools.partial(_ffn_down_kernel, tm=tm, sub=sub),
        grid=(s // tm,),
        in_specs=[row(D_FF), row(D_MODEL),
                  pl.BlockSpec((D_FF, D_MODEL), lambda i: (0, 0), pipeline_mode=pl.Buffered(1)),
                  _layer_vec(layer, D_MODEL), _layer_vec(layer, D_MODEL)],
        out_specs=[row(D_MODEL), row(D_MODEL)],
        out_shape=[jax.ShapeDtypeStruct((s, D_MODEL), F32), jax.ShapeDtypeStruct((s, D_MODEL), BF16)],
        compiler_params=_params(("arbitrary",), vmem),
        name="ffn_down",
    )(act, r, w_down_b, ln_g, ln_b)


def _tiles(s):
    pick = lambda tm, sub: (min(tm, s), min(sub, s))
    return dict(tables=pick(1024, 1024), ret_proj=pick(2048, 1024), att_proj=pick(1024, 256),
                mixer=pick(1024, 128), mix_out=pick(512, 512), ffn_up=pick(1024, 1024), ffn_down=pick(512, 256))


def kernel(x, p, positions, w_in, w_out, ret_norm_g, ret_norm_b, attn_sinks, ln1_g, ln1_b, w_ffn_up,
           ffn_conv_w, ffn_conv_b, w_ffn_down, ln2_g, ln2_b, w_ple_gate, b_ple_gate, w_ple_proj):
    batch, s, d = x.shape
    assert batch == 1 and d == D_MODEL and s % RET_CHUNK == 0
    t = _tiles(s)
    assert all(s % tm == 0 and tm % sub == 0 for tm, sub in t.values())

    xf = x.reshape(s, d)
    pf = p.reshape(DEPTH, s, PLE_DIM)
    cos_r, sin_r, cos_a, sin_a, sin_b, xb = _rot_tables(positions.reshape(s), xf, t["tables"][0])
    ret_consts, k_decay = _retention_constants()
    vecs = lambda v: v.reshape(DEPTH, 1, -1)
    gn_g, gn_b, g1, b1, g2, b2 = (vecs(v) for v in (ret_norm_g, ret_norm_b, ln1_g, ln1_b, ln2_g, ln2_b))
    bg, conv_b = vecs(b_ple_gate), vecs(ffn_conv_b)

    for l in range(DEPTH):
        rp, kdec, w_att_b = _ret_proj(xb, w_in, l, cos_r, sin_r, k_decay, *t["ret_proj"])
        ap = _att_proj(xb, w_att_b, cos_a, sin_a, sin_b, *t["att_proj"])
        cat, w_out_b, w_gate_b, w_proj_b = _mixer(rp, kdec, ap, ret_consts, gn_g, gn_b, attn_sinks,
                                                  w_out, w_ple_gate, w_ple_proj, l, t["mixer"][0])
        r, xb, w_down_b = _mix_out(cat, xf, pf, l, w_out_b, w_gate_b, w_proj_b, g1, b1, bg, w_ffn_down,
                                   *t["mix_out"])
        act = _ffn_up(xb, w_ffn_up, ffn_conv_w, conv_b, l, *t["ffn_up"])
        xf, xb = _ffn_down(act, r, w_down_b, g2, b2, l, *t["ffn_down"])
    return xf.reshape(batch, s, d)
```

```python
import functools

import jax
import jax.numpy as jnp
from jax import lax
from jax.experimental import pallas as pl
from jax.experimental.pallas import tpu as pltpu

F32 = jnp.float32
BF16 = jnp.bfloat16

D_MODEL = 2048
DEPTH = 4
PLE_DIM = 256
RET_WIDTH = 1024
RET_HEADS = 4
RET_HEAD_DIM = 256
RET_CHUNK = 128
RET_ROT_BASE = 10000.0
ATT_HEAD_DIM = 64
ATT_WIDTH = 1024
ATT_HEADS = 16
ATT_KV_HEADS = 2
ATT_GROUP = ATT_HEADS // ATT_KV_HEADS
KV_WIDTH = 128
WINDOW = 128
ROPE_THETA = 10000.0
D_FF = 5632
CONV_WIDTH = 3
LN_EPS = 1e-5
GN_EPS = 1e-6
DEEPNORM_ALPHA = (2 * DEPTH) ** 0.25

LANES = 128
SUBLANES = 8
BF16_SUBLANE_ROWS = 16
MXU_DIM = 256
MIB = 1 << 20
VMEM_LIMIT_CAP = 58 * MIB
COMPILER_VMEM = dict(rot_tables=16 * MIB, ret_proj=12 * MIB, att_proj=12 * MIB, mixer=14 * MIB, mix_out=12 * MIB,
                     ffn_up=12 * MIB, ffn_down=8 * MIB)

RET_COLS = 4 * RET_WIDTH
ATT_IN_COLS = ATT_WIDTH + 2 * KV_WIDTH
ATT_OUT_COLS = ATT_WIDTH + 4 * KV_WIDTH
HALF = LANES // 2
LOG2_E = 1.4426950408889634
ATT_SCORE_SCALE = ATT_HEAD_DIM ** -0.5 * LOG2_E


def _params(semantics, vmem_bytes):
    return pltpu.CompilerParams(dimension_semantics=semantics,
                                vmem_limit_bytes=int(min(vmem_bytes, VMEM_LIMIT_CAP)))


def _layer_norm(z, g, b):
    t = z - jnp.mean(z[:, :LANES], axis=-1, keepdims=True)
    m1 = jnp.mean(t, axis=-1, keepdims=True)
    m2 = jnp.mean(t * t, axis=-1, keepdims=True)
    return (t - m1) * lax.rsqrt(m2 - m1 * m1 + LN_EPS) * g + b


def _sigmoid(v):
    return 0.5 + 0.5 * jnp.tanh(0.5 * v)


def _silu(v):
    t = 0.5 * v
    return t + t * jnp.tanh(t)


def _slab_rows(rows, steps):
    assert rows % steps == 0 and (rows // steps) % BF16_SUBLANE_ROWS == 0, (rows, steps)
    return rows // steps


def _layer_vec(layer, cols):
    return pl.BlockSpec((None, 1, cols), lambda *_: (layer, 0, 0))


def _rot_tables_kernel(pos_ref, fr_ref, fa_ref, x_ref, cr_ref, sr_ref, ca_ref, sa_ref, sb_ref, xb_ref):
    xb_ref[...] = x_ref[...].astype(BF16)
    pos = pos_ref[...].astype(F32)
    ang_r = pos * fr_ref[...]
    cr_ref[...] = jnp.cos(ang_r)
    sr_ref[...] = jnp.sin(ang_r)
    ang_a = pos * fa_ref[...]
    c = jnp.cos(ang_a)
    s = jnp.sin(ang_a)
    lane = lax.broadcasted_iota(jnp.int32, c.shape, 1)
    first_half = (lane & (ATT_HEAD_DIM // 2)) == 0
    ca_ref[...] = c
    sa_ref[...] = jnp.where(first_half, -s, 0.0)
    sb_ref[...] = jnp.where(first_half, 0.0, s)


def _rot_tables(positions, xf, tm):
    s = positions.shape[0]
    ret_inv_freq = 1.0 / (RET_ROT_BASE ** jnp.linspace(0.0, 1.0, RET_HEAD_DIM // 2, dtype=F32))
    att_inv_freq = ROPE_THETA ** (-jnp.arange(0, ATT_HEAD_DIM, 2, dtype=F32) / ATT_HEAD_DIM)
    fr = ret_inv_freq.reshape(1, LANES)
    fa = jnp.tile(att_inv_freq, LANES // (ATT_HEAD_DIM // 2)).reshape(1, LANES)
    tab = jax.ShapeDtypeStruct((s, LANES), F32)
    row = pl.BlockSpec((tm, LANES), lambda i: (i, 0))
    const = pl.BlockSpec((1, LANES), lambda i: (0, 0))
    wide = pl.BlockSpec((tm, D_MODEL), lambda i: (i, 0))
    return pl.pallas_call(
        _rot_tables_kernel,
        grid=(s // tm,),
        in_specs=[pl.BlockSpec((tm, 1), lambda i: (i, 0)), const, const, wide],
        out_specs=[row] * 5 + [wide],
        out_shape=[tab] * 5 + [jax.ShapeDtypeStruct((s, D_MODEL), BF16)],
        compiler_params=_params(("arbitrary",), 2 * tm * D_MODEL * 6 + COMPILER_VMEM["rot_tables"]),
        name="rot_tables",
    )(positions.reshape(s, 1), fr, fa, xf)


RET_TN = 512
RET_BLOCKS_PER_PART = RET_WIDTH // RET_TN
RET_HEADS_PER_BLOCK = RET_TN // RET_HEAD_DIM


def _ret_proj_kernel(x_ref, w_ref, c_ref, s_ref, kd_ref, wrow_ref, o_ref, kdec_ref, watt_ref, wb_ref, *, tm, sub):
    j = pl.program_id(1)
    part = j // RET_BLOCKS_PER_PART
    watt_ref[...] = wrow_ref[:, RET_COLS:].astype(BF16)

    def sub_blocks():
        wb_ref[...] = w_ref[...].astype(BF16)
        for r in range(0, tm, sub):
            rows = slice(r, r + sub)
            yield rows, jnp.dot(x_ref[rows, :], wb_ref[...], preferred_element_type=F32)

    def rotary(rows, acc, scale):
        c = c_ref[rows, :] * scale
        s = s_ref[rows, :] * scale
        for h in range(RET_HEADS_PER_BLOCK):
            lo = h * RET_HEAD_DIM
            mid = lo + RET_HEAD_DIM // 2
            a1 = acc[:, lo:mid]
            a2 = acc[:, mid:lo + RET_HEAD_DIM]
            yield h, lo, a1 * c - a2 * s
            yield h, mid, a1 * s + a2 * c

    @pl.when(part == 0)
    def _():
        for rows, acc in sub_blocks():
            for _, col, val in rotary(rows, acc, 1.0):
                o_ref[rows, col:col + LANES] = val.astype(BF16)

    @pl.when(part == 1)
    def _():
        for rows, acc in sub_blocks():
            for h, col, val in rotary(rows, acc, RET_HEAD_DIM ** -0.5):
                o_ref[rows, col:col + LANES] = val.astype(BF16)
                chunks = val.reshape(sub // RET_CHUNK, RET_CHUNK, LANES) * kd_ref[h][None]
                kdec_ref[rows, col:col + LANES] = chunks.reshape(sub, LANES).astype(BF16)

    @pl.when(part == 2)
    def _():
        for rows, acc in sub_blocks():
            o_ref[rows, :] = acc.astype(BF16)

    @pl.when(part == 3)
    def _():
        for rows, acc in sub_blocks():
            o_ref[rows, :] = _silu(acc).astype(BF16)


def _ret_proj(xb, w_in, layer, cos_r, sin_r, k_decay, tm, sub):
    s = xb.shape[0]
    vmem = 2 * (tm * D_MODEL * 2 + D_MODEL * RET_TN * 4 + 2 * tm * RET_TN * 2 + 2 * tm * LANES * 4) \
        + D_MODEL * RET_TN * 2 + COMPILER_VMEM["ret_proj"]
    k_block = lambda j: jnp.clip(j - RET_BLOCKS_PER_PART, 0, RET_BLOCKS_PER_PART - 1)
    nj = RET_COLS // RET_TN
    slab = _slab_rows(D_MODEL, (s // tm) * nj)
    return pl.pallas_call(
        functools.partial(_ret_proj_kernel, tm=tm, sub=sub),
        grid=(s // tm, nj),
        in_specs=[pl.BlockSpec((tm, D_MODEL), lambda i, j: (i, 0)),
                  pl.BlockSpec((None, D_MODEL, RET_TN), lambda i, j: (layer, 0, j)),
                  pl.BlockSpec((tm, LANES), lambda i, j: (i, 0)),
                  pl.BlockSpec((tm, LANES), lambda i, j: (i, 0)),
                  pl.BlockSpec((RET_HEADS_PER_BLOCK, RET_CHUNK, LANES), lambda i, j: (k_block(j), 0, 0)),
                  pl.BlockSpec((None, slab, RET_COLS + ATT_IN_COLS), lambda i, j: (layer, i * nj + j, 0))],
        out_specs=[pl.BlockSpec((tm, RET_TN), lambda i, j: (i, j)),
                   pl.BlockSpec((tm, RET_TN), lambda i, j: (i, k_block(j))),
                   pl.BlockSpec((slab, ATT_IN_COLS), lambda i, j: (i * nj + j, 0))],
        out_shape=[jax.ShapeDtypeStruct((s, RET_COLS), BF16), jax.ShapeDtypeStruct((s, RET_WIDTH), BF16),
                   jax.ShapeDtypeStruct((D_MODEL, ATT_IN_COLS), BF16)],
        scratch_shapes=[pltpu.VMEM((D_MODEL, RET_TN), BF16)],
        compiler_params=_params(("arbitrary", "arbitrary"), vmem),
        name="ret_proj",
    )(xb, w_in, cos_r, sin_r, k_decay, w_in)


ATT_W_BLOCKS = ATT_IN_COLS // MXU_DIM


def _att_proj_kernel(x_ref, w_ref, c_ref, sa_ref, sb_ref, o_ref, *, tm, sub):

    for r in range(0, tm, sub):
        rows = slice(r, r + sub)
        c = c_ref[rows, :]
        sa = sa_ref[rows, :]
        sb = sb_ref[rows, :]

        def rot(a):
            return (a * c + pltpu.roll(a, LANES - ATT_HEAD_DIM // 2, 1) * sa
                    + pltpu.roll(a, ATT_HEAD_DIM // 2, 1) * sb)

        x = x_ref[rows, :]
        acc_all = jnp.dot(x, w_ref[...], preferred_element_type=F32)
        for wb in range(ATT_W_BLOCKS):
            acc = acc_all[:, wb * MXU_DIM:(wb + 1) * MXU_DIM]
            for half in range(MXU_DIM // LANES):
                col = wb * MXU_DIM + half * LANES
                a = acc[:, half * LANES:(half + 1) * LANES]
                if col < ATT_WIDTH:
                    o_ref[rows, col:col + LANES] = (rot(a) * ATT_SCORE_SCALE).astype(BF16)
                elif col < ATT_WIDTH + KV_WIDTH:
                    k = rot(a)
                    o_ref[rows, col:col + LANES] = k.astype(BF16)
                    o_ref[rows, col + 2 * LANES:col + 3 * LANES] = pltpu.roll(k, HALF, 1).astype(BF16)
                else:
                    o_ref[rows, col:col + LANES] = a.astype(BF16)
                    o_ref[rows, col + 2 * LANES:col + 3 * LANES] = pltpu.roll(a, HALF, 1).astype(BF16)


def _att_proj(xb, w_att_b, cos_a, sin_a, sin_b, tm, sub):
    s = xb.shape[0]
    vmem = 2 * (tm * D_MODEL * 2 + D_MODEL * ATT_IN_COLS * 2 + tm * ATT_OUT_COLS * 2 + 3 * tm * LANES * 4) \
        + COMPILER_VMEM["att_proj"]
    row = pl.BlockSpec((tm, LANES), lambda i: (i, 0))
    w_specs = [pl.BlockSpec((D_MODEL, ATT_IN_COLS), lambda i: (0, 0))]
    return pl.pallas_call(
        functools.partial(_att_proj_kernel, tm=tm, sub=sub),
        grid=(s // tm,),
        in_specs=[pl.BlockSpec((tm, D_MODEL), lambda i: (i, 0))] + w_specs + [row, row, row],
        out_specs=pl.BlockSpec((tm, ATT_OUT_COLS), lambda i: (i, 0)),
        out_shape=jax.ShapeDtypeStruct((s, ATT_OUT_COLS), BF16),
        compiler_params=_params(("arbitrary",), vmem),
        name="att_proj",
    )(xb, w_att_b, cos_a, sin_a, sin_b)


def _mixer_kernel(cd_ref, sink_ref, r_ref, kdec_ref, a_ref, dec_ref, qd_ref, g_ref, b_ref, wo_ref, wg_ref, wp_ref,
                  o_ref, wob_ref, wgb_ref, wpb_ref, state_ref, kv_ref, *, tm, layer):
    i = pl.program_id(0)
    wob_ref[...] = wo_ref[...].astype(BF16)
    wgb_ref[...] = wg_ref[...].astype(BF16)
    wpb_ref[...] = wp_ref[...].astype(BF16)
    n_chunks = tm // RET_CHUNK
    kv_cols = 4 * KV_WIDTH

    @pl.when(i == 0)
    def _():
        state_ref[...] = jnp.zeros_like(state_ref)
        kv_ref[0:WINDOW, :] = jnp.zeros((WINDOW, kv_cols), BF16)

    @pl.when(i > 0)
    def _():
        kv_ref[0:WINDOW, :] = kv_ref[tm:tm + WINDOW, :]

    kv_ref[WINDOW:tm + WINDOW, :] = a_ref[:, ATT_WIDTH:ATT_WIDTH + kv_cols]

    contract_last = (((1,), (1,)), ((), ()))
    contract_first = (((0,), (0,)), ((), ()))
    qi = lax.broadcasted_iota(jnp.int32, (WINDOW, 2 * WINDOW), 0)
    kj = lax.broadcasted_iota(jnp.int32, (WINDOW, 2 * WINDOW), 1)
    band = (kj > qi) & (kj <= qi + WINDOW)
    low_lanes = lax.broadcasted_iota(jnp.int32, (2 * WINDOW, LANES), 1) < HALF

    def chunk(c, carry):
        r0 = pl.multiple_of(c * RET_CHUNK, RET_CHUNK)
        rows = pl.ds(r0, RET_CHUNK)

        for h in range(RET_HEADS):
            lo = h * RET_HEAD_DIM
            hi = lo + RET_HEAD_DIM
            q = r_ref[rows, lo:hi]
            k = r_ref[rows, RET_WIDTH + lo:RET_WIDTH + hi]
            v = r_ref[rows, 2 * RET_WIDTH + lo:2 * RET_WIDTH + hi]
            silu_gate = r_ref[rows, 3 * RET_WIDTH + lo:3 * RET_WIDTH + hi].astype(F32)
            st = state_ref[h]
            sc = lax.dot_general(q, k, contract_last, preferred_element_type=F32) * dec_ref[h]
            inner = jnp.dot(sc.astype(BF16), v, preferred_element_type=F32)
            cross = jnp.dot(q, st.astype(BF16), preferred_element_type=F32) * qd_ref[h]
            state_ref[h] = st * cd_ref[h] + lax.dot_general(kdec_ref[rows, lo:hi], v, contract_first,
                                                            preferred_element_type=F32)
            y = inner + cross
            mu = jnp.mean(y, axis=-1, keepdims=True)
            d = y - mu
            var = jnp.mean(d * d, axis=-1, keepdims=True)
            yn = d * lax.rsqrt(var + GN_EPS) * g_ref[:, lo:hi] + b_ref[:, lo:hi]
            o_ref[rows, lo:hi] = (silu_gate * yn).astype(BF16)

        kk = kv_ref[pl.ds(r0, 2 * WINDOW), :]
        k_nat = kk[:, 0:LANES]
        v_nat = kk[:, LANES:2 * LANES]
        k_swp = kk[:, 2 * LANES:3 * LANES]
        v_swp = kk[:, 3 * LANES:4 * LANES]
        zero = jnp.zeros_like(k_nat)
        k_ext = ((jnp.where(low_lanes, k_nat, zero), jnp.where(low_lanes, zero, k_swp)),
                 (jnp.where(low_lanes, k_swp, zero), jnp.where(low_lanes, zero, k_nat)))
        v_ext = ((jnp.where(low_lanes, v_nat, zero), jnp.where(low_lanes, zero, v_swp)),
                 (jnp.where(low_lanes, v_swp, zero), jnp.where(low_lanes, zero, v_nat)))
        first_block = (i * n_chunks + c) == 0
        mask = band & ((kj >= WINDOW) | jnp.logical_not(first_block))
        pairs_per_kv = ATT_GROUP // 2
        for kvh in range(ATT_KV_HEADS):
            for pr in range(pairs_per_kv):
                pair = kvh * pairs_per_kv + pr
                q_pair = a_ref[rows, pair * LANES:(pair + 1) * LANES]
                o_pair = None
                for par in range(2):
                    sink = sink_ref[layer, 2 * pair + par] * LOG2_E
                    sc = lax.dot_general(q_pair, k_ext[kvh][par], contract_last, preferred_element_type=F32)
                    sc = jnp.where(mask, sc, -jnp.inf)
                    m = jnp.maximum(jnp.max(sc, axis=-1, keepdims=True), sink)
                    p = jnp.exp2(sc - m)
                    den = jnp.sum(p, axis=-1, keepdims=True) + jnp.exp2(sink - m)
                    o = jnp.dot(p.astype(BF16), v_ext[kvh][par], preferred_element_type=F32) * (1.0 / den)
                    o_pair = o if o_pair is None else o_pair + o
                o_ref[rows, RET_WIDTH + pair * LANES:RET_WIDTH + (pair + 1) * LANES] = o_pair.astype(BF16)
        return carry

    lax.fori_loop(0, n_chunks, chunk, 0, unroll=4)


def _mixer(rp, kdec, ap, ret_consts, gn_g, gn_b, sinks, w_out, w_gate, w_proj, layer, tm):
    s = rp.shape[0]
    decay, q_decay, chunk_decay = ret_consts
    smem = pl.BlockSpec(memory_space=pltpu.SMEM)
    const3 = lambda shape: pl.BlockSpec(shape, lambda i: (0, 0, 0))
    steps = s // tm
    slab_d, slab_p = _slab_rows(D_MODEL, steps), _slab_rows(PLE_DIM, steps)
    w_in_spec = lambda rows: pl.BlockSpec((None, rows, D_MODEL), lambda i: (layer, i, 0))
    w_out_spec = lambda rows: pl.BlockSpec((rows, D_MODEL), lambda i: (i, 0))
    w_shape = lambda rows: jax.ShapeDtypeStruct((rows, D_MODEL), BF16)
    vmem = 2 * (tm * (RET_COLS + RET_WIDTH) * 2 + tm * ATT_OUT_COLS * 2 + tm * D_MODEL * 2) \
        + 2 * (2 * slab_d + slab_p) * D_MODEL * 6 \
        + 2 * 4 * (decay.size + q_decay.size) \
        + RET_HEADS * RET_HEAD_DIM * RET_HEAD_DIM * 4 + (tm + WINDOW) * 4 * KV_WIDTH * 2 + COMPILER_VMEM["mixer"]
    return pl.pallas_call(
        functools.partial(_mixer_kernel, tm=tm, layer=layer),
        grid=(s // tm,),
        in_specs=[smem, smem,
                  pl.BlockSpec((tm, RET_COLS), lambda i: (i, 0)),
                  pl.BlockSpec((tm, RET_WIDTH), lambda i: (i, 0)),
                  pl.BlockSpec((tm, ATT_OUT_COLS), lambda i: (i, 0)),
                  const3(decay.shape), const3(q_decay.shape),
                  _layer_vec(layer, RET_WIDTH), _layer_vec(layer, RET_WIDTH),
                  w_in_spec(slab_d), w_in_spec(slab_d), w_in_spec(slab_p)],
        out_specs=[pl.BlockSpec((tm, D_MODEL), lambda i: (i, 0)),
                   w_out_spec(slab_d), w_out_spec(slab_d), w_out_spec(slab_p)],
        out_shape=[jax.ShapeDtypeStruct((s, D_MODEL), BF16), w_shape(D_MODEL), w_shape(D_MODEL), w_shape(PLE_DIM)],
        scratch_shapes=[pltpu.VMEM((RET_HEADS, RET_HEAD_DIM, RET_HEAD_DIM), F32),
                        pltpu.VMEM((tm + WINDOW, 4 * KV_WIDTH), BF16)],
        compiler_params=_params(("arbitrary",), vmem),
        name="mixer",
    )(chunk_decay, sinks, rp, kdec, ap, decay, q_decay, gn_g, gn_b, w_out, w_gate, w_proj)


def _retention_constants():
    c = RET_CHUNK
    log_g = jnp.log1p(-jnp.exp2(-5.0 - jnp.arange(RET_HEADS, dtype=F32)))
    idx = jnp.arange(c, dtype=F32)
    diff = idx[:, None] - idx[None, :]
    decay = jnp.where(diff[None] >= 0, jnp.exp(log_g[:, None, None] * jnp.maximum(diff, 0.0)[None]), 0.0)
    q_decay = jnp.exp(log_g[:, None] * (idx[None, :] + 1.0))
    k_decay = jnp.exp(log_g[:, None] * (c - 1.0 - idx[None, :]))
    chunk_decay = jnp.exp(log_g * c)
    wide = lambda t, lanes: jnp.broadcast_to(t[:, :, None], (RET_HEADS, c, lanes))
    return (decay, wide(q_decay, RET_HEAD_DIM), chunk_decay), wide(k_decay, LANES)


def _mix_out_kernel(cat_ref, x_ref, p_ref, wo_ref, wg_ref, wp_ref, g_ref, b_ref, bg_ref, wd_ref,
                    r_ref, xb_ref, wdb_ref, *, tm, sub):
    wdb_ref[...] = wd_ref[...].astype(BF16)
    for r in range(0, tm, sub):
        rows = slice(r, r + sub)
        z = DEEPNORM_ALPHA * x_ref[rows, :] + jnp.dot(cat_ref[rows, :], wo_ref[...], preferred_element_type=F32)
        x1 = _layer_norm(z, g_ref[...], b_ref[...])
        x1b = x1.astype(BF16)
        gate = jnp.dot(x1b, wg_ref[...], preferred_element_type=F32) + bg_ref[...]
        ple = jnp.dot(p_ref[rows, :].astype(BF16), wp_ref[...], preferred_element_type=F32) * _sigmoid(gate)
        r_ref[rows, :] = DEEPNORM_ALPHA * x1 + ple
        xb_ref[rows, :] = x1b


def _mix_out(cat, x, p, layer, w_out_b, w_gate_b, w_proj_b, ln_g, ln_b, b_gate, w_down, tm, sub):
    s = x.shape[0]
    row = lambda cols: pl.BlockSpec((tm, cols), lambda i: (i, 0))
    resident = lambda rows_: pl.BlockSpec((rows_, D_MODEL), lambda i: (0, 0), pipeline_mode=pl.Buffered(1))
    slab = _slab_rows(D_FF, s // tm)
    vmem = (2 * D_MODEL * D_MODEL + PLE_DIM * D_MODEL) * 2 + 2 * slab * D_MODEL * 6 \
        + 2 * tm * (D_MODEL * 2 + D_MODEL * 4 + PLE_DIM * 4 + D_MODEL * 4 + D_MODEL * 2) + COMPILER_VMEM["mix_out"]
    return pl.pallas_call(
        functools.partial(_mix_out_kernel, tm=tm, sub=sub),
        grid=(s // tm,),
        in_specs=[row(D_MODEL), row(D_MODEL),
                  pl.BlockSpec((None, tm, PLE_DIM), lambda i: (layer, i, 0)),
                  resident(D_MODEL), resident(D_MODEL), resident(PLE_DIM),
                  _layer_vec(layer, D_MODEL), _layer_vec(layer, D_MODEL), _layer_vec(layer, D_MODEL),
                  pl.BlockSpec((None, slab, D_MODEL), lambda i: (layer, i, 0))],
        out_specs=[row(D_MODEL), row(D_MODEL), pl.BlockSpec((slab, D_MODEL), lambda i: (i, 0))],
        out_shape=[jax.ShapeDtypeStruct((s, D_MODEL), F32), jax.ShapeDtypeStruct((s, D_MODEL), BF16),
                   jax.ShapeDtypeStruct((D_FF, D_MODEL), BF16)],
        compiler_params=_params(("arbitrary",), vmem),
        name="mix_out",
    )(cat, x, p, w_out_b, w_gate_b, w_proj_b, ln_g, ln_b, b_gate, w_down)


FFN_TN = 512
FFN_BLOCKS = D_FF // FFN_TN


def _ffn_up_kernel(x_ref, wg_ref, wu_ref, cwg_ref, cwu_ref, cbg_ref, cbu_ref, o_ref, carry_ref, wcat_ref,
                   *, tm, sub):
    @pl.when(pl.program_id(1) == 0)
    def _():
        wcat_ref[:, 0:FFN_TN] = wg_ref[...].astype(BF16)
        wcat_ref[:, FFN_TN:2 * FFN_TN] = wu_ref[...].astype(BF16)
        carry_ref[...] = jnp.zeros_like(carry_ref)

    def conv(prev, h, cw_ref, cb_ref):
        ext = jnp.concatenate([prev, h], axis=0)
        y = (ext * cw_ref[2:3, :] + pltpu.roll(ext, 1, 0) * cw_ref[1:2, :]
             + pltpu.roll(ext, 2, 0) * cw_ref[0:1, :] + cb_ref[...])
        return y[SUBLANES:]

    prev_g = carry_ref[0]
    prev_u = carry_ref[1]
    for r in range(0, tm, sub):
        h = jnp.dot(x_ref[r:r + sub, :], wcat_ref[...], preferred_element_type=F32)
        hg = h[:, 0:FFN_TN]
        hu = h[:, FFN_TN:2 * FFN_TN]
        yg = conv(prev_g, hg, cwg_ref, cbg_ref)
        yu = conv(prev_u, hu, cwu_ref, cbu_ref)
        prev_g = hg[sub - SUBLANES:sub]
        prev_u = hu[sub - SUBLANES:sub]
        o_ref[r:r + sub, :] = (_silu(yg) * yu).astype(BF16)
    carry_ref[0] = prev_g
    carry_ref[1] = prev_u


def _ffn_up(xb, w_up, conv_w, conv_b, layer, tm, sub):
    s = xb.shape[0]
    nb = FFN_BLOCKS
    vmem = 2 * (tm * D_MODEL * 2 + 2 * D_MODEL * FFN_TN * 4 + tm * FFN_TN * 2) + 2 * D_MODEL * FFN_TN * 2 \
        + COMPILER_VMEM["ffn_up"]
    wspec = lambda off: pl.BlockSpec((None, D_MODEL, FFN_TN), lambda j, i: (layer, 0, j + off))
    cwspec = lambda off: pl.BlockSpec((None, CONV_WIDTH, FFN_TN), lambda j, i: (layer, 0, j + off))
    cbspec = lambda off: pl.BlockSpec((None, 1, FFN_TN), lambda j, i: (layer, 0, j + off))
    return pl.pallas_call(
        functools.partial(_ffn_up_kernel, tm=tm, sub=sub),
        grid=(nb, s // tm),
        in_specs=[pl.BlockSpec((tm, D_MODEL), lambda j, i: (i, 0)),
                  wspec(0), wspec(nb), cwspec(0), cwspec(nb), cbspec(0), cbspec(nb)],
        out_specs=pl.BlockSpec((tm, FFN_TN), lambda j, i: (i, j)),
        out_shape=jax.ShapeDtypeStruct((s, D_FF), BF16),
        scratch_shapes=[pltpu.VMEM((2, SUBLANES, FFN_TN), F32),
                        pltpu.VMEM((D_MODEL, 2 * FFN_TN), BF16)],
        compiler_params=_params(("arbitrary", "arbitrary"), vmem),
        name="ffn_up",
    )(xb, w_up, w_up, conv_w, conv_w, conv_b, conv_b)


def _ffn_down_kernel(a_ref, r_ref, w_ref, g_ref, b_ref, x_ref, xb_ref, *, tm, sub):
    for r in range(0, tm, sub):
        rows = slice(r, r + sub)
        z = r_ref[rows, :] + jnp.dot(a_ref[rows, :], w_ref[...], preferred_element_type=F32)
        x2 = _layer_norm(z, g_ref[...], b_ref[...])
        x_ref[rows, :] = x2
        xb_ref[rows, :] = x2.astype(BF16)


def _ffn_down(act, r, w_down_b, ln_g, ln_b, layer, tm, sub):
    s = r.shape[0]
    row = lambda cols: pl.BlockSpec((tm, cols), lambda i: (i, 0))
    vmem = D_FF * D_MODEL * 2 + 2 * tm * (D_FF * 2 + D_MODEL * 4 + D_MODEL * 4 + D_MODEL * 2) + COMPILER_VMEM["ffn_down"]
    return pl.pallas_call(
        functools.partial(_ffn_down_kernel, tm=tm, sub=sub),
        grid=(s // tm,),
        in_specs=[row(D_FF), row(D_MODEL),
                  pl.BlockSpec((D_FF, D_MODEL), lambda i: (0, 0), pipeline_mode=pl.Buffered(1)),
                  _layer_vec(layer, D_MODEL), _layer_vec(layer, D_MODEL)],
        out_specs=[row(D_MODEL), row(D_MODEL)],
        out_shape=[jax.ShapeDtypeStruct((s, D_MODEL), F32), jax.ShapeDtypeStruct((s, D_MODEL), BF16)],
        compiler_params=_params(("arbitrary",), vmem),
        name="ffn_down",
    )(act, r, w_down_b, ln_g, ln_b)


def _tiles(s):
    pick = lambda tm, sub: (min(tm, s), min(sub, s))
    return dict(tables=pick(1024, 1024), ret_proj=pick(2048, 1024), att_proj=pick(1024, 256),
                mixer=pick(1024, 128), mix_out=pick(512, 512), ffn_up=pick(1024, 1024), ffn_down=pick(512, 256))


def kernel(x, p, positions, w_in, w_out, ret_norm_g, ret_norm_b, attn_sinks, ln1_g, ln1_b, w_ffn_up,
           ffn_conv_w, ffn_conv_b, w_ffn_down, ln2_g, ln2_b, w_ple_gate, b_ple_gate, w_ple_proj):
    batch, s, d = x.shape
    assert batch == 1 and d == D_MODEL and s % RET_CHUNK == 0
    t = _tiles(s)
    assert all(s % tm == 0 and tm % sub == 0 for tm, sub in t.values())

    xf = x.reshape(s, d)
    pf = p.reshape(DEPTH, s, PLE_DIM)
    cos_r, sin_r, cos_a, sin_a, sin_b, xb = _rot_tables(positions.reshape(s), xf, t["tables"][0])
    ret_consts, k_decay = _retention_constants()
    vecs = lambda v: v.reshape(DEPTH, 1, -1)
    gn_g, gn_b, g1, b1, g2, b2 = (vecs(v) for v in (ret_norm_g, ret_norm_b, ln1_g, ln1_b, ln2_g, ln2_b))
    bg, conv_b = vecs(b_ple_gate), vecs(ffn_conv_b)

    for l in range(DEPTH):
        rp, kdec, w_att_b = _ret_proj(xb, w_in, l, cos_r, sin_r, k_decay, *t["ret_proj"])
        ap = _att_proj(xb, w_att_b, cos_a, sin_a, sin_b, *t["att_proj"])
        cat, w_out_b, w_gate_b, w_proj_b = _mixer(rp, kdec, ap, ret_consts, gn_g, gn_b, attn_sinks,
                                                  w_out, w_ple_gate, w_ple_proj, l, t["mixer"][0])
        r, xb, w_down_b = _mix_out(cat, xf, pf, l, w_out_b, w_gate_b, w_proj_b, g1, b1, bg, w_ffn_down,
                                   *t["mix_out"])
        act = _ffn_up(xb, w_ffn_up, ffn_conv_w, conv_b, l, *t["ffn_up"])
        xf, xb = _ffn_down(act, r, w_down_b, g2, b2, l, *t["ffn_down"])
    return xf.reshape(batch, s, d)
```
